```python
import math, functools
import jax, jax.numpy as jnp
from jax import lax
import numpy as np

D_MODEL = 2048
BATCH = 4
SEQ = 8192
DEPTH = 2

DN_ALPHA = (2 * DEPTH) ** 0.25
DN_BETA = (8 * DEPTH) ** -0.25
LN_EPS = 1e-5
ADA_INIT = 0.1

MIX_WIDTH = D_MODEL // 2

POOL_WINDOWS = (2, 4, 8, 16)
POOL_GROUPS = len(POOL_WINDOWS)
POOL_GROUP_DIM = MIX_WIDTH // POOL_GROUPS

RWKV_HEAD_DIM = 64
RWKV_HEADS = MIX_WIDTH // RWKV_HEAD_DIM
RWKV_DECAY_LORA = 64
RWKV_AAA_LORA = 64
RWKV_MV_LORA = 32
RWKV_GATE_LORA = 160
RWKV_LN_EPS = RWKV_HEAD_DIM * 1e-5

GDN_HEAD_DIM = 128
GDN_HEADS = MIX_WIDTH // GDN_HEAD_DIM
GDN_CONV = 4
GDN_CHUNK = 64
GDN_EPS = 1e-6

D_FF = ((-(-(8 * D_MODEL) // 3) + 255) // 256) * 256

N_POOL = MIX_WIDTH
N_RWKV = 3 * MIX_WIDTH + RWKV_DECAY_LORA + RWKV_AAA_LORA + RWKV_GATE_LORA
N_GDN = 4 * MIX_WIDTH + 2 * GDN_HEADS
N_GATE = 3 * D_MODEL
N_IN = N_POOL + N_RWKV + N_GDN + N_GATE
IN_SPLITS = [N_POOL, N_POOL + N_RWKV, N_POOL + N_RWKV + N_GDN]
RWKV_SPLITS = [MIX_WIDTH, 2 * MIX_WIDTH, 3 * MIX_WIDTH,
               3 * MIX_WIDTH + RWKV_DECAY_LORA,
               3 * MIX_WIDTH + RWKV_DECAY_LORA + RWKV_AAA_LORA]

kernel_name = "hybrid_pool_rwkv7_gdn_deepnorm_adaln"


def _layer_norm(x, w, b):
    xf = x.astype(jnp.float32)
    mu = xf.mean(-1, keepdims=True)
    var = jnp.mean(jnp.square(xf - mu), -1, keepdims=True)
    return ((xf - mu) * lax.rsqrt(var + LN_EPS) * w + b).astype(x.dtype)


def _token_shift(p):
    return jnp.pad(p, ((0, 0), (1, 0), (0, 0)))[:, :-1]


def _pool_mixer(p, w_group, scale):
    B, T, _ = p.shape
    pg = p.reshape(B, T, POOL_GROUPS, POOL_GROUP_DIM)
    cs = jnp.cumsum(pg.astype(jnp.float32), axis=1)
    t = jnp.arange(T)
    pooled = []
    for g, win in enumerate(POOL_WINDOWS):
        csg = cs[:, :, g]
        prev = jnp.pad(csg, ((0, 0), (win, 0), (0, 0)))[:, :T]
        cnt = jnp.minimum(t + 1, win).astype(jnp.float32)[None, :, None]
        pooled.append((csg - prev) / cnt)
    pooled = jnp.stack(pooled, axis=2).astype(p.dtype) - pg
    y = jnp.einsum('btgc,gcd->btgd', pooled, w_group).reshape(B, T, MIX_WIDTH)
    return y * scale


def _rwkv7_scan(r, decay, k, v, kk, a):
    B, T, H, N = r.shape

    def step(S, inp):
        r_t, d_t, k_t, v_t, kk_t, a_t = inp
        sa = jnp.einsum('bhvk,bhk->bhv', S, -kk_t)
        S = (S * d_t[:, :, None, :] + sa[..., None] * (kk_t * a_t)[:, :, None, :]
             + v_t[..., None] * k_t[:, :, None, :])
        return S, jnp.einsum('bhvk,bhk->bhv', S, r_t)

    xs = tuple(jnp.moveaxis(z, 1, 0) for z in (r, decay, k, v, kk, a))
    S0 = jnp.zeros((B, H, N, N), jnp.float32)
    _, y = lax.scan(step, S0, xs)
    return jnp.moveaxis(y, 0, 1)


def _rwkv7_mixer(p, mu, w0, w2, a0, a2, g2, k_k, k_a, r_k, ln_w, ln_b, v_first, vres):
    f32 = jnp.float32
    B, T, _ = p.shape
    H, N = RWKV_HEADS, RWKV_HEAD_DIM
    xs = p + (_token_shift(p) - p) * mu
    r, k, v, w_lo, a_lo, g_lo = jnp.split(xs, RWKV_SPLITS, axis=-1)
    w = -jax.nn.softplus(-(w0 + jnp.tanh(w_lo) @ w2).astype(f32)) - 0.5
    a = jax.nn.sigmoid(a0 + a_lo @ a2)
    g = jax.nn.sigmoid(g_lo) @ g2
    v_layer = v
    if vres is not None:
        v0, v1, v2 = vres
        v = v + (v_first - v) * jax.nn.sigmoid(v0 + (v @ v1) @ v2)
    heads = lambda z: z.reshape(B, T, H, N).astype(f32)
    r, k, v, a = map(heads, (r, k, v, a))
    decay = jnp.exp(-jnp.exp(w)).reshape(B, T, H, N)
    kk = k * k_k.reshape(H, N).astype(f32)
    kk = kk / jnp.maximum(jnp.sqrt(jnp.sum(kk * kk, -1, keepdims=True)), 1e-12)
    k = k * (1.0 + (a - 1.0) * k_a.reshape(H, N).astype(f32))
    y = _rwkv7_scan(r, decay, k, v, kk, a)
    mean = y.mean(-1, keepdims=True)
    var = jnp.mean(jnp.square(y - mean), -1, keepdims=True)
    y = (y - mean) * lax.rsqrt(var + RWKV_LN_EPS) * ln_w.reshape(H, N) + ln_b.reshape(H, N)
    y = y + jnp.sum(r * k * r_k.astype(f32), -1, keepdims=True) * v
    y = y.reshape(B, T, MIX_WIDTH).astype(p.dtype) * g
    return y, v_layer


def _causal_dwconv(x, w):
    K = w.shape[0]
    xp = jnp.pad(x, ((0, 0), (K - 1, 0), (0, 0)))
    return lax.conv_general_dilated(xp, w[:, None, :], window_strides=(1,), padding='VALID',
                                    dimension_numbers=('NWC', 'WIO', 'NWC'),
                                    feature_group_count=x.shape[-1])


def _chunk_gated_delta_rule(q, k, v, g, beta):
    f32 = jnp.float32
    B, T, H, Dk = q.shape
    Dv = v.shape[-1]
    L = GDN_CHUNK
    NC = T // L

    def chunks(z):
        return jnp.moveaxis(z.reshape(B, NC, L, H, *z.shape[3:]), 3, 1)

    q, k, v, g, beta = map(chunks, (q, k, v, g, beta))
    G = jnp.cumsum(g, axis=-1)
    idx = jnp.arange(L)
    causal = idx[:, None] >= idx[None, :]
    strict = idx[:, None] > idx[None, :]
    diff = G[..., :, None] - G[..., None, :]
    gamma = jnp.where(causal, jnp.exp(jnp.where(causal, diff, 0.0)), 0.0)
    kb = k * beta[..., None]
    m = jnp.where(strict, jnp.einsum('bhnid,bhnjd->bhnij', kb, k) * gamma, 0.0)
    unit_lower = m + jnp.eye(L, dtype=f32)
    solve = functools.partial(lax.linalg.triangular_solve, left_side=True, lower=True,
                              unit_diagonal=True)
    u = solve(unit_lower, v * beta[..., None])
    w = solve(unit_lower, kb * jnp.exp(G)[..., None])
    a_qk = jnp.einsum('bhnid,bhnjd->bhnij', q, k) * gamma

    def step(S, inp):
        q_c, k_c, u_c, w_c, G_c, a_c = inp
        v_new = u_c - jnp.einsum('bhld,bhde->bhle', w_c, S)
        o = (jnp.einsum('bhld,bhde->bhle', q_c * jnp.exp(G_c)[..., None], S)
             + jnp.einsum('bhls,bhse->bhle', a_c, v_new))
        g_last = G_c[..., -1:]
        S = (S * jnp.exp(g_last)[..., None]
             + jnp.einsum('bhld,bhle->bhde', k_c * jnp.exp(g_last - G_c)[..., None], v_new))
        return S, o

    xs = tuple(jnp.moveaxis(z, 2, 0) for z in (q, k, u, w, G, a_qk))
    S0 = jnp.zeros((B, H, Dk, Dv), f32)
    _, o = lax.scan(step, S0, xs)
    return o.transpose(1, 0, 3, 2, 4).reshape(B, T, H, Dv)


def _gdn_mixer(p, conv_w, a_log, dt_bias, norm_w):
    f32 = jnp.float32
    B, T, _ = p.shape
    C, H, Dh = MIX_WIDTH, GDN_HEADS, GDN_HEAD_DIM
    qkv = jax.nn.silu(_causal_dwconv(p[..., :3 * C], conv_w))
    z, a_raw, b_raw = jnp.split(p[..., 3 * C:], [C, C + H], axis=-1)
    q, k, v = (t.reshape(B, T, H, Dh).astype(f32) for t in jnp.split(qkv, 3, axis=-1))
    q = q * lax.rsqrt(jnp.sum(q * q, -1, keepdims=True) + GDN_EPS) * (Dh ** -0.5)
    k = k * lax.rsqrt(jnp.sum(k * k, -1, keepdims=True) + GDN_EPS)
    g = -jnp.exp(a_log.astype(f32)) * jax.nn.softplus(a_raw.astype(f32) + dt_bias)
    beta = jax.nn.sigmoid(b_raw.astype(f32))
    o = _chunk_gated_delta_rule(q, k, v, g, beta)
    o = o * lax.rsqrt(jnp.mean(o * o, -1, keepdims=True) + GDN_EPS) * norm_w
    o = o * jax.nn.silu(z.reshape(B, T, H, Dh).astype(f32))
    return o.reshape(B, T, C).astype(p.dtype)


def setup_inputs(seed: int = 0) -> dict:
    key = jax.random.key(seed)
    ks = iter(jax.random.split(key, 48))
    f32 = jnp.float32
    L, D, C = DEPTH, D_MODEL, MIX_WIDTH

    def nrm(shape, scale):
        return jax.random.normal(next(ks), shape, f32) * scale

    def unif(shape, lo, hi):
        return jax.random.uniform(next(ks), shape, f32, lo, hi)

    x = nrm((BATCH, SEQ, D), 1.0)
    c = nrm((BATCH, D), 1.0)
    ada_w = nrm((L, D, 6 * D), ADA_INIT * D ** -0.5)
    ada_b = nrm((L, 6 * D), 0.01)
    w_in = nrm((L, D, N_IN), D ** -0.5)
    pool_w = nrm((L, POOL_GROUPS, POOL_GROUP_DIM, POOL_GROUP_DIM), POOL_GROUP_DIM ** -0.5)
    pool_scale = 1.0 + nrm((L, C), 0.02)
    rwkv_mu = unif((L, N_RWKV), 0.0, 1.0)
    rwkv_w0 = unif((L, C), -6.0, 0.0)
    rwkv_w2 = nrm((L, RWKV_DECAY_LORA, C), 0.1 * RWKV_DECAY_LORA ** -0.5)
    rwkv_a0 = nrm((L, C), 0.1)
    rwkv_a2 = nrm((L, RWKV_AAA_LORA, C), 0.1 * RWKV_AAA_LORA ** -0.5)
    rwkv_g2 = nrm((L, RWKV_GATE_LORA, C), RWKV_GATE_LORA ** -0.5)
    rwkv_k_k = 0.85 + nrm((L, C), 0.02)
    rwkv_k_a = 1.0 + nrm((L, C), 0.02)
    rwkv_r_k = nrm((L, RWKV_HEADS, RWKV_HEAD_DIM), 0.1)
    rwkv_ln_w = 1.0 + nrm((L, C), 0.02)
    rwkv_ln_b = nrm((L, C), 0.01)
    rwkv_v0 = nrm((L - 1, C), 0.1)
    rwkv_v1 = nrm((L - 1, C, RWKV_MV_LORA), C ** -0.5)
    rwkv_v2 = nrm((L - 1, RWKV_MV_LORA, C), 0.1 * RWKV_MV_LORA ** -0.5)
    gdn_conv_w = nrm((L, GDN_CONV, 3 * C), GDN_CONV ** -0.5)
    gdn_a_log = jnp.log(unif((L, GDN_HEADS), 1.0, 16.0))
    dt = jnp.exp(unif((L, GDN_HEADS), math.log(1e-3), math.log(1e-1)))
    gdn_dt_bias = dt + jnp.log(-jnp.expm1(-dt))
    gdn_norm_w = 1.0 + nrm((L, GDN_HEAD_DIM), 0.02)
    w_branch_a = nrm((L, C, D), C ** -0.5)
    w_branch_b = nrm((L, C, D), C ** -0.5)
    w_branch_c = nrm((L, C, D), C ** -0.5)
    w_out = nrm((L, D, D), DN_BETA * D ** -0.5)
    ln1_w = 1.0 + nrm((L, D), 0.02)
    ln1_b = nrm((L, D), 0.01)
    ffn_w_up = nrm((L, D, 2 * D_FF), D ** -0.5)
    ffn_w_down = nrm((L, D_FF, D), DN_BETA * D_FF ** -0.5)
    ln2_w = 1.0 + nrm((L, D), 0.02)
    ln2_b = nrm((L, D), 0.01)
    return {"x": x, "c": c, "ada_w": ada_w, "ada_b": ada_b, "w_in": w_in,
            "pool_w": pool_w, "pool_scale": pool_scale,
            "rwkv_mu": rwkv_mu, "rwkv_w0": rwkv_w0, "rwkv_w2": rwkv_w2, "rwkv_a0": rwkv_a0,
            "rwkv_a2": rwkv_a2, "rwkv_g2": rwkv_g2, "rwkv_k_k": rwkv_k_k, "rwkv_k_a": rwkv_k_a,
            "rwkv_r_k": rwkv_r_k, "rwkv_ln_w": rwkv_ln_w, "rwkv_ln_b": rwkv_ln_b,
            "rwkv_v0": rwkv_v0, "rwkv_v1": rwkv_v1, "rwkv_v2": rwkv_v2,
            "gdn_conv_w": gdn_conv_w, "gdn_a_log": gdn_a_log, "gdn_dt_bias": gdn_dt_bias,
            "gdn_norm_w": gdn_norm_w, "w_branch_a": w_branch_a, "w_branch_b": w_branch_b,
            "w_branch_c": w_branch_c, "w_out": w_out, "ln1_w": ln1_w, "ln1_b": ln1_b,
            "ffn_w_up": ffn_w_up, "ffn_w_down": ffn_w_down, "ln2_w": ln2_w, "ln2_b": ln2_b}


def reference(x, c, ada_w, ada_b, w_in, pool_w, pool_scale, rwkv_mu, rwkv_w0, rwkv_w2, rwkv_a0,
              rwkv_a2, rwkv_g2, rwkv_k_k, rwkv_k_a, rwkv_r_k, rwkv_ln_w, rwkv_ln_b, rwkv_v0,
              rwkv_v1, rwkv_v2, gdn_conv_w, gdn_a_log, gdn_dt_bias, gdn_norm_w, w_branch_a,
              w_branch_b, w_branch_c, w_out, ln1_w, ln1_b, ffn_w_up, ffn_w_down, ln2_w, ln2_b):
    cond = jax.nn.silu(c)
    v_first = None
    for l in range(DEPTH):
        mod = cond @ ada_w[l] + ada_b[l]
        sh_m, sc_m, gt_m, sh_f, sc_f, gt_f = jnp.split(mod[:, None, :], 6, axis=-1)

        h = x * (1.0 + sc_m) + sh_m
        proj = h @ w_in[l]
        p_pool, p_rwkv, p_gdn, p_gate = jnp.split(proj, IN_SPLITS, axis=-1)
        y_a = _pool_mixer(p_pool, pool_w[l], pool_scale[l])
        vres = None if l == 0 else (rwkv_v0[l - 1], rwkv_v1[l - 1], rwkv_v2[l - 1])
        y_b, v_l = _rwkv7_mixer(p_rwkv, rwkv_mu[l], rwkv_w0[l], rwkv_w2[l], rwkv_a0[l],
                                rwkv_a2[l], rwkv_g2[l], rwkv_k_k[l], rwkv_k_a[l], rwkv_r_k[l],
                                rwkv_ln_w[l], rwkv_ln_b[l], v_first, vres)
        if l == 0:
            v_first = v_l
        y_c = _gdn_mixer(p_gdn, gdn_conv_w[l], gdn_a_log[l], gdn_dt_bias[l], gdn_norm_w[l])
        g_a, g_b, g_c = jnp.split(jax.nn.sigmoid(p_gate), 3, axis=-1)
        merged = g_a * (y_a @ w_branch_a[l]) + g_b * (y_b @ w_branch_b[l]) + g_c * (y_c @ w_branch_c[l])
        x = _layer_norm(DN_ALPHA * x + (1.0 + gt_m) * (merged @ w_out[l]), ln1_w[l], ln1_b[l])

        h = x * (1.0 + sc_f) + sh_f
        gate, up = jnp.split(h @ ffn_w_up[l], 2, axis=-1)
        ffn = (jax.nn.silu(gate) * up) @ ffn_w_down[l]
        x = _layer_norm(DN_ALPHA * x + (1.0 + gt_f) * ffn, ln2_w[l], ln2_b[l])
    return x
```

```python
import functools

import jax
import jax.numpy as jnp
from jax import lax
from jax.experimental import pallas as pl
from jax.experimental.pallas import tpu as pltpu

F32 = jnp.float32
BF16 = jnp.bfloat16

DEPTH = 2
DN_ALPHA = (2 * DEPTH) ** 0.25
LN_EPS = 1e-5
MIX = 1024
POOL_WINDOWS = (2, 4, 8, 16)
POOL_GROUP = 256
RWKV_HEAD = 64
RWKV_LN_EPS = RWKV_HEAD * 1e-5
GDN_HEAD = 128
GDN_HEADS = 8
GDN_EPS = 1e-6
CHUNK = 64
LANES = 128
N_SMALL = 384
GDN_G_LANE = 32
GDN_B_LANE = 40
VMEM_LIMIT = 56 * 1024 * 1024

TM_MM = 1024
TM_LN = 512
TM_PREP = 256
TM_CORE = 256


def _dot(a, b):
    return jnp.dot(a.astype(BF16), b.astype(BF16), preferred_element_type=F32)


def _dot_nt(a, b):
    return lax.dot_general(a.astype(BF16), b.astype(BF16), (((1,), (1,)), ((), ())),
                           preferred_element_type=F32)


def _split(a, n):
    terms, rest = [], a
    for _ in range(n):
        t = rest.astype(BF16)
        terms.append(t)
        rest = rest - t.astype(F32)
    return terms


def _dot3(a, b, nt=False):
    d = _dot_nt if nt else _dot
    a1, a2 = _split(a, 2)
    b1, b2 = _split(b, 2)
    return d(a1, b1) + (d(a1, b2) + d(a2, b1))


def _dot_sel_l(sel, b, n):
    out = None
    for t in _split(b, n):
        p = jnp.dot(sel, t, preferred_element_type=F32)
        out = p if out is None else out + p
    return out


def _dot_sel_r(a, sel, n):
    out = None
    for t in _split(a, n):
        p = jnp.dot(t, sel, preferred_element_type=F32)
        out = p if out is None else out + p
    return out


def _sigmoid(x):
    return 1.0 / (1.0 + jnp.exp(-x))


def _silu(x):
    return x * _sigmoid(x)


def _softplus(x):
    return jnp.maximum(x, 0.0) + jnp.log(1.0 + jnp.exp(-jnp.abs(x)))


def _iota2(shape, dim):
    return lax.broadcasted_iota(jnp.int32, shape, dim)


def _block_masks(n):
    r = _iota2((n, n), 0)
    c = _iota2((n, n), 1)
    same = lambda s: (r >> s) == (c >> s)
    m = {
        "eye": (r == c).astype(F32),
        "bd8": same(3),
        "off16": same(4) & jnp.logical_not(same(3)),
        "off32": same(5) & jnp.logical_not(same(4)),
        "off64": same(6) & jnp.logical_not(same(5)),
        "bd64": same(6),
    }
    m["strict"] = m["bd64"] & (r > c)
    m["incl"] = m["bd64"] & (r >= c)
    return m


def _chunk_cumsum_matrix(n):
    r = _iota2((n, n), 0)
    c = _iota2((n, n), 1)
    return (((r >> 6) == (c >> 6)) & (c <= r)).astype(BF16)


def _tri_inv(a, m):
    a8 = jnp.where(m["bd8"], a, 0.0)
    s = m["eye"] + a8
    p = _dot3(a8, a8)
    s = s + _dot3(p, s)
    p = _dot3(p, p)
    s = s + _dot3(p, s)
    for key in ("off16", "off32", "off64"):
        low = jnp.where(m[key], a, 0.0)
        s = s + _dot3(s, _dot3(low, s))
    return s


def _shift_rows(x, halo, s):
    xr = pltpu.roll(x, s, 0)
    hr = pltpu.roll(halo, s, 0)
    rows = _iota2(halo.shape, 0)
    top = jnp.where(rows < s, hr, xr[:8])
    return jnp.concatenate([top, xr[8:]], axis=0)


def _resid_ln(x, y, gt, lnw, lnb):
    z = DN_ALPHA * x + (1.0 + gt) * y
    mu = jnp.mean(z, axis=-1, keepdims=True)
    zc = z - mu
    var = jnp.mean(zc * zc, axis=-1, keepdims=True)
    return zc * lax.rsqrt(var + LN_EPS) * lnw + lnb


def _params(*sem):
    return pltpu.CompilerParams(dimension_semantics=sem, vmem_limit_bytes=VMEM_LIMIT)


def _ada_kernel(c_ref, w_ref, b_ref, o_ref):
    c = c_ref[...]
    o_ref[0] = _dot3(_silu(c), w_ref[0]) + b_ref[0]


def _ada(c, ada_w, ada_b):
    nl, d, n = ada_w.shape
    b = c.shape[0]
    tn = 512
    cp = jnp.zeros((8, d), F32).at[:b].set(c)
    out = pl.pallas_call(
        _ada_kernel,
        grid=(nl, n // tn),
        in_specs=[pl.BlockSpec((8, d), lambda l, j: (0, 0)),
                  pl.BlockSpec((1, d, tn), lambda l, j: (l, 0, j)),
                  pl.BlockSpec((1, 1, tn), lambda l, j: (l, 0, j))],
        out_specs=pl.BlockSpec((1, 8, tn), lambda l, j: (l, 0, j)),
        out_shape=jax.ShapeDtypeStruct((nl, 8, n), F32),
        compiler_params=_params("parallel", "parallel"),
        name="ada_mod",
    )(cp, ada_w, ada_b.reshape(nl, 1, n))
    return out[:, :b]


def _modulate_kernel(x_ref, sc_ref, sh_ref, h_ref):
    h_ref[...] = (x_ref[...] * (1.0 + sc_ref[0]) + sh_ref[0]).astype(BF16)


def _modulate(x, sc, sh, seq):
    m, d = x.shape
    tm = TM_LN
    per = seq // tm
    mod = pl.BlockSpec((1, 1, d), lambda i: (i // per, 0, 0))
    return pl.pallas_call(
        _modulate_kernel,
        grid=(m // tm,),
        in_specs=[pl.BlockSpec((tm, d), lambda i: (i, 0)), mod, mod],
        out_specs=pl.BlockSpec((tm, d), lambda i: (i, 0)),
        out_shape=jax.ShapeDtypeStruct((m, d), BF16),
        compiler_params=_params("parallel"),
        name="modulate",
    )(x, sc, sh)


def _mm_kernel(a_ref, b_ref, o_ref, *, act):
    acc = jnp.dot(a_ref[...], b_ref[...], preferred_element_type=F32)
    if act == "sigmoid":
        acc = _sigmoid(acc)
    o_ref[...] = acc.astype(o_ref.dtype)


def _matmul(a, b, *, tn, out_dtype, act=None, name):
    m, k = a.shape
    n = b.shape[1]
    tm = TM_MM
    return pl.pallas_call(
        functools.partial(_mm_kernel, act=act),
        grid=(m // tm, n // tn),
        in_specs=[pl.BlockSpec((tm, k), lambda i, j: (i, 0)),
                  pl.BlockSpec((k, tn), lambda i, j: (0, j))],
        out_specs=pl.BlockSpec((tm, tn), lambda i, j: (i, j)),
        out_shape=jax.ShapeDtypeStruct((m, n), out_dtype),
        compiler_params=_params("parallel", "parallel"),
        name=name,
    )(a, b)


def _pool_kernel(p_ref, halo_ref, w_ref, scale_ref, o_ref, *, seq, tm):
    t0 = (pl.program_id(0) * tm) % seq
    x = p_ref[...]
    halo = jnp.where(t0 == 0, 0.0, halo_ref[...])
    ext = jnp.concatenate([halo, x], axis=0)
    sums = [ext]
    for sh in (1, 2, 4, 8):
        sums.append(sums[-1] + pltpu.roll(sums[-1], sh, 0))
    t = t0 + _iota2((tm, POOL_GROUP), 0)
    ys = []
    for g, win in enumerate(POOL_WINDOWS):
        cols = slice(g * POOL_GROUP, (g + 1) * POOL_GROUP)
        cnt = jnp.minimum(t + 1, win).astype(F32)
        pooled = sums[g + 1][16:, cols] / cnt - x[:, cols]
        ys.append(_dot(pooled, w_ref[g]))
    o_ref[...] = (jnp.concatenate(ys, axis=1) * scale_ref[...]).astype(BF16)


def _pool(p, pool_w, pool_scale, seq):
    m = p.shape[0]
    tm = TM_LN
    return pl.pallas_call(
        functools.partial(_pool_kernel, seq=seq, tm=tm),
        grid=(m // tm,),
        in_specs=[pl.BlockSpec((tm, MIX), lambda i: (i, 0)),
                  pl.BlockSpec((16, MIX), lambda i: (jnp.maximum(i * (tm // 16) - 1, 0), 0)),
                  pl.BlockSpec(pool_w.shape, lambda i: (0, 0, 0)),
                  pl.BlockSpec((1, MIX), lambda i: (0, 0))],
        out_specs=pl.BlockSpec((tm, MIX), lambda i: (i, 0)),
        out_shape=jax.ShapeDtypeStruct((m, MIX), BF16),
        compiler_params=_params("parallel"),
        name="pool_mixer",
    )(p, p, pool_w.astype(BF16), pool_scale.reshape(1, MIX))


def _rwkv_prep_kernel(*refs, seq, tm, has_vres):
    (pr, pk, pv, hr, hk, hv, ps, hs, mur, muk, muv, mus, w0, a0, kkp, kap,
     w2p, a2p, g2p) = refs[:19]
    if has_vres:
        vf, v0, v1p, v2p = refs[19:23]
        outs = refs[23:]
    else:
        outs = refs[19:]
    r_o, k_o, v_o, kk_o, a_o, ld_o, g_o = outs

    first = (pl.program_id(0) * tm) % seq == 0

    def lerp(p_ref, h_ref, mu_ref):
        p = p_ref[...]
        prev = _shift_rows(p, jnp.where(first, 0.0, h_ref[...]), 1)
        return p + (prev - p) * mu_ref[...]

    xr = lerp(pr, hr, mur)
    xk = lerp(pk, hk, muk)
    xv = lerp(pv, hv, muv)
    xs = lerp(ps, hs, mus)

    w = -_softplus(-(w0[...] + _dot(jnp.tanh(xs), w2p[...]))) - 0.5
    a = _sigmoid(a0[...] + _dot(xs, a2p[...]))
    g = _dot(_sigmoid(xs), g2p[...])
    if has_vres:
        gate = _sigmoid(v0[...] + _dot(_dot(xv, v1p[...]), v2p[...]))
        xv = xv + (vf[...] - xv) * gate

    kk = xk * kkp[...]
    r2 = _iota2((LANES, LANES), 0)
    c2 = _iota2((LANES, LANES), 1)
    head_ones = ((r2 >> 6) == (c2 >> 6)).astype(BF16)
    kk2 = kk * kk
    ss = jnp.concatenate(
        [_dot_sel_r(kk2[:, i * LANES:(i + 1) * LANES], head_ones, 2) for i in range(MIX // LANES)],
        axis=1)
    kk = kk / jnp.maximum(jnp.sqrt(ss), 1e-12)

    r_o[...] = xr
    k_o[...] = xk * (1.0 + (a - 1.0) * kap[...])
    v_o[...] = xv
    kk_o[...] = kk
    a_o[...] = a
    ld_o[...] = -jnp.exp(w)
    g_o[...] = g


def _rwkv_prep(p, psm, wts, v_first, seq):
    m = p.shape[0]
    tm = TM_PREP
    has_vres = v_first is not None
    tile = lambda cb: pl.BlockSpec((tm, MIX), lambda i, cb=cb: (i, cb))
    halo = lambda cb: pl.BlockSpec((8, MIX), lambda i, cb=cb: (jnp.maximum(i * (tm // 8) - 1, 0), cb))
    vec = pl.BlockSpec((1, MIX), lambda i: (0, 0))
    full = lambda arr: pl.BlockSpec(arr.shape, lambda i: (0,) * arr.ndim)
    in_specs = [tile(1), tile(2), tile(3), halo(1), halo(2), halo(3),
                pl.BlockSpec((tm, N_SMALL), lambda i: (i, 0)),
                pl.BlockSpec((8, N_SMALL), lambda i: (jnp.maximum(i * (tm // 8) - 1, 0), 0)),
                vec, vec, vec, pl.BlockSpec((1, N_SMALL), lambda i: (0, 0)),
                vec, vec, vec, vec,
                full(wts["w2p"]), full(wts["a2p"]), full(wts["g2p"])]
    args = [p, p, p, p, p, p, psm, psm, wts["mu_r"], wts["mu_k"], wts["mu_v"], wts["mu_s"],
            wts["w0"], wts["a0"], wts["k_k"], wts["k_a"], wts["w2p"], wts["a2p"], wts["g2p"]]
    if has_vres:
        in_specs += [pl.BlockSpec((tm, MIX), lambda i: (i, 0)), vec, full(wts["v1p"]), full(wts["v2p"])]
        args += [v_first, wts["v0"], wts["v1p"], wts["v2p"]]
    out = pl.BlockSpec((tm, MIX), lambda i: (i, 0))
    return pl.pallas_call(
        functools.partial(_rwkv_prep_kernel, seq=seq, tm=tm, has_vres=has_vres),
        grid=(m // tm,),
        in_specs=in_specs,
        out_specs=[out] * 7,
        out_shape=[jax.ShapeDtypeStruct((m, MIX), F32)] * 7,
        compiler_params=_params("parallel"),
        name="rwkv_prep",
    )(*args)


def _rwkv_core_kernel(r_ref, k_ref, v_ref, kk_ref, a_ref, ld_ref, g_ref, rk_ref, lnw_ref, lnb_ref,
                      o_ref, h_ref):
    @pl.when(pl.program_id(2) == 0)
    def _():
        h_ref[...] = jnp.zeros_like(h_ref)

    rows = r_ref.shape[0]
    m = _block_masks(LANES)
    lane = _iota2((1, LANES), 1)
    m0 = (lane < RWKV_HEAD).astype(F32)
    m1 = 1.0 - m0
    r2 = _iota2((LANES, LANES), 0)
    c2 = _iota2((LANES, LANES), 1)
    head_ones = ((r2 >> 6) == (c2 >> 6)).astype(BF16)
    stack = lambda x: jnp.concatenate([x * m0, x * m1], axis=0)
    unstack = lambda x: x[:CHUNK] + x[CHUNK:]

    ld = ld_ref[...]
    gcum = _dot_sel_l(_chunk_cumsum_matrix(rows), ld, 3)

    for c in range(rows // CHUNK):
        sl = slice(c * CHUNK, (c + 1) * CHUNK)
        gc, ldc = gcum[sl], ld[sl]
        rc, kc, vc, kkc = r_ref[sl, :], k_ref[sl, :], v_ref[sl, :], kk_ref[sl, :]
        bc = kkc * a_ref[sl, :]
        glast = gc[CHUNK - 1:CHUNK]
        e_neg = jnp.exp(-gc)
        e_rest = jnp.exp(glast - gc)
        a_hat = -kkc * jnp.exp(gc - ldc)
        r_hat = rc * jnp.exp(gc)
        a_st, r_st = stack(a_hat), stack(r_hat)
        b_hat, k_hat = bc * e_neg, kc * e_neg
        big = _dot3(jnp.concatenate([a_st, r_st], axis=0),
                    jnp.concatenate([b_hat, b_hat, k_hat, k_hat], axis=0), nt=True)
        a_ab = jnp.where(m["strict"], big[:LANES, :LANES], 0.0)
        a_ak = jnp.where(m["strict"], big[:LANES, LANES:], 0.0)
        a_rb = jnp.where(m["incl"], big[LANES:, :LANES], 0.0)
        a_rk = jnp.where(m["incl"], big[LANES:, LANES:], 0.0)
        t_inv = _tri_inv(a_ab, m)
        v_st = stack(vc)
        x1 = _dot3(jnp.concatenate([a_ak, a_rk], axis=0), v_st)
        x2 = _dot3(t_inv, jnp.concatenate([a_st, x1[:LANES]], axis=1))
        x3 = _dot3(a_rb, x2)
        q_t = unstack(r_st + x3[:, :LANES])
        y0 = unstack(x3[:, LANES:] + x1[LANES:])
        w_t = unstack(x2[:, :LANES])
        u_t = unstack(x2[:, LANES:])
        b_bar_t = (bc * e_rest).T
        k_bar_t = (kc * e_rest).T
        phi = m["eye"] * jnp.exp(glast) + jnp.where(m["bd64"], _dot3(b_bar_t, w_t), 0.0)
        psi = jnp.where(m["bd64"],
                        _dot3(jnp.concatenate([b_bar_t, k_bar_t], axis=1),
                              jnp.concatenate([u_t, vc], axis=0)), 0.0)
        h = h_ref[...]
        y = y0 + _dot3(q_t, h)
        h_ref[...] = _dot3(phi, h) + psi

        mean = _dot_sel_r(y, head_ones, 2) * (1.0 / RWKV_HEAD)
        yc = y - mean
        var = _dot_sel_r(yc * yc, head_ones, 2) * (1.0 / RWKV_HEAD)
        yn = yc * lax.rsqrt(var + RWKV_LN_EPS) * lnw_ref[...] + lnb_ref[...]
        bonus = _dot_sel_r(rc * kc * rk_ref[...], head_ones, 2) * vc
        o_ref[sl, :] = ((yn + bonus) * g_ref[sl, :]).astype(BF16)


def _rwkv_core(prep, wts, batch, seq):
    m = prep[0].shape[0]
    tm = TM_CORE
    per = seq // tm
    tile = pl.BlockSpec((tm, LANES), lambda b, p, j: (b * per + j, p))
    vec = pl.BlockSpec((1, LANES), lambda b, p, j: (0, p))
    return pl.pallas_call(
        _rwkv_core_kernel,
        grid=(batch, MIX // LANES, per),
        in_specs=[tile] * 7 + [vec] * 3,
        out_specs=tile,
        out_shape=jax.ShapeDtypeStruct((m, MIX), BF16),
        scratch_shapes=[pltpu.VMEM((LANES, LANES), F32)],
        compiler_params=_params("parallel", "parallel", "arbitrary"),
        name="rwkv_core",
    )(*prep, wts["r_k"], wts["ln_w"], wts["ln_b"])


def _gdn_prep_kernel(pq, pk, pv, hq, hk, hv, ps, cwq, cwk, cwv, alog, dtb,
                     q_o, k_o, v_o, gb_o, *, seq, tm):
    first = (pl.program_id(0) * tm) % seq == 0
    ones = jnp.ones((LANES, LANES), BF16)

    def conv_silu(p_ref, h_ref, w_ref):
        x = p_ref[...]
        halo = jnp.where(first, 0.0, h_ref[...])
        w = w_ref[...]
        y = x * w[3:4]
        for s in (1, 2, 3):
            y = y + _shift_rows(x, halo, s) * w[3 - s:4 - s]
        return _silu(y)

    def l2norm(y, scale):
        blocks = []
        for i in range(GDN_HEADS):
            blk = y[:, i * LANES:(i + 1) * LANES]
            ss = _dot_sel_r(blk * blk, ones, 2)
            blocks.append(blk * (lax.rsqrt(ss + GDN_EPS) * scale))
        return jnp.concatenate(blocks, axis=1)

    q_o[...] = l2norm(conv_silu(pq, hq, cwq), GDN_HEAD ** -0.5)
    k_o[...] = l2norm(conv_silu(pk, hk, cwk), 1.0)
    v_o[...] = conv_silu(pv, hv, cwv)

    raw = ps[:, N_SMALL - LANES:]
    gval = -jnp.exp(alog[...]) * _softplus(raw + dtb[...])
    gcum = _dot_sel_l(_chunk_cumsum_matrix(tm), gval, 3)
    lane = _iota2(raw.shape, 1)
    is_g = (lane >= GDN_G_LANE) & (lane < GDN_G_LANE + GDN_HEADS)
    is_b = (lane >= GDN_B_LANE) & (lane < GDN_B_LANE + GDN_HEADS)
    gb_o[...] = jnp.where(is_g, gcum, jnp.where(is_b, _sigmoid(raw), 0.0))


def _gdn_prep(p, psm, wts, seq):
    m = p.shape[0]
    tm = TM_PREP
    tile = lambda cb: pl.BlockSpec((tm, MIX), lambda i, cb=cb: (i, cb))
    halo = lambda cb: pl.BlockSpec((8, MIX), lambda i, cb=cb: (jnp.maximum(i * (tm // 8) - 1, 0), cb))
    cw = pl.BlockSpec((4, MIX), lambda i: (0, 0))
    lv = pl.BlockSpec((1, LANES), lambda i: (0, 0))
    out = pl.BlockSpec((tm, MIX), lambda i: (i, 0))
    return pl.pallas_call(
        functools.partial(_gdn_prep_kernel, seq=seq, tm=tm),
        grid=(m // tm,),
        in_specs=[tile(4), tile(5), tile(6), halo(4), halo(5), halo(6),
                  pl.BlockSpec((tm, N_SMALL), lambda i: (i, 0)), cw, cw, cw, lv, lv],
        out_specs=[out, out, out, pl.BlockSpec((tm, LANES), lambda i: (i, 0))],
        out_shape=[jax.ShapeDtypeStruct((m, MIX), F32)] * 3 + [jax.ShapeDtypeStruct((m, LANES), F32)],
        compiler_params=_params("parallel"),
        name="gdn_prep",
    )(p, p, p, p, p, p, psm, wts["cw_q"], wts["cw_k"], wts["cw_v"], wts["alog"], wts["dtb"])


def _gdn_core_kernel(q_ref, k_ref, v_ref, gb_ref, z_ref, nw_ref, o_ref, s_ref):
    head = pl.program_id(1)

    @pl.when(pl.program_id(2) == 0)
    def _():
        s_ref[...] = jnp.zeros_like(s_ref)

    rows = q_ref.shape[0]
    m = _block_masks(LANES)
    ones = jnp.ones((LANES, LANES), BF16)

    ri = _iota2((LANES, 2 * LANES), 0)
    ci = _iota2((LANES, 2 * LANES), 1)
    sel = (ri == head + jnp.where(ci < LANES, GDN_G_LANE, GDN_B_LANE)).astype(BF16)
    gbb = _dot_sel_r(gb_ref[...], sel, 3)

    for sc in range(rows // LANES):
        sl = slice(sc * LANES, (sc + 1) * LANES)
        gf = gbb[sl, :LANES]
        beta = gbb[sl, LANES:]
        q, k, v = q_ref[sl, :], k_ref[sl, :], v_ref[sl, :]
        diff = gf - gf.T
        gamma = jnp.where(m["incl"], jnp.exp(jnp.where(m["incl"], diff, 0.0)), 0.0)
        kb, vb = k * beta, v * beta
        kq = _dot3(jnp.concatenate([kb, q], axis=0), k, nt=True)
        a_neg = -jnp.where(m["strict"], kq[:LANES] * gamma, 0.0)
        a_qk = kq[LANES:] * gamma
        t_inv = _tri_inv(a_neg, m)
        e_g = jnp.exp(gf)
        uw = _dot3(t_inv, jnp.concatenate([vb, kb * e_g], axis=1))
        x3 = _dot3(a_qk, uw)
        y0 = x3[:, :LANES]
        q_t = q * e_g - x3[:, LANES:]
        for c in range(LANES // CHUNK):
            cs = slice(c * CHUNK, (c + 1) * CHUNK)
            glast = gf[(c + 1) * CHUNK - 1:(c + 1) * CHUNK]
            k_bar_t = (k[cs] * jnp.exp(glast - gf[cs])).T
            pp = _dot3(k_bar_t, uw[cs])
            phi = m["eye"] * jnp.exp(glast) - pp[:, LANES:]
            psi = pp[:, :LANES]
            s = s_ref[...]
            y = y0[cs] + _dot3(q_t[cs], s)
            s_ref[...] = _dot3(phi, s) + psi

            ms = _dot_sel_r(y * y, ones, 2) * (1.0 / GDN_HEAD)
            out = y * lax.rsqrt(ms + GDN_EPS) * nw_ref[...]
            os_ = slice(sc * LANES + c * CHUNK, sc * LANES + (c + 1) * CHUNK)
            o_ref[os_, :] = (out * _silu(z_ref[os_, :])).astype(BF16)


def _gdn_core(q, k, v, gb, p, norm_w, batch, seq):
    m = q.shape[0]
    tm = TM_CORE
    per = seq // tm
    tile = pl.BlockSpec((tm, LANES), lambda b, h, j: (b * per + j, h))
    z_col0 = 7 * (MIX // LANES)
    return pl.pallas_call(
        _gdn_core_kernel,
        grid=(batch, GDN_HEADS, per),
        in_specs=[tile, tile, tile,
                  pl.BlockSpec((tm, LANES), lambda b, h, j: (b * per + j, 0)),
                  pl.BlockSpec((tm, LANES), lambda b, h, j: (b * per + j, z_col0 + h)),
                  pl.BlockSpec((1, LANES), lambda b, h, j: (0, 0))],
        out_specs=tile,
        out_shape=jax.ShapeDtypeStruct((m, MIX), BF16),
        scratch_shapes=[pltpu.VMEM((LANES, LANES), F32)],
        compiler_params=_params("parallel", "parallel", "arbitrary"),
        name="gdn_core",
    )(q, k, v, gb, p, norm_w.reshape(1, LANES))


def _merge_kernel(ya, yb, yc, wa, wb, wc, ga, gb, gc, o_ref):
    dot = lambda y, w: jnp.dot(y[...], w[...], preferred_element_type=F32)
    acc = ga[...].astype(F32) * dot(ya, wa)
    acc = acc + gb[...].astype(F32) * dot(yb, wb)
    acc = acc + gc[...].astype(F32) * dot(yc, wc)
    o_ref[...] = acc.astype(BF16)


def _merge(ya, yb, yc, wa, wb, wc, gates):
    m = ya.shape[0]
    d = wa.shape[1]
    tm, tn = TM_LN, 1024
    nj = d // tn
    y = pl.BlockSpec((tm, MIX), lambda i, j: (i, 0))
    w = pl.BlockSpec((MIX, tn), lambda i, j: (0, j))
    g = lambda br: pl.BlockSpec((tm, tn), lambda i, j, br=br: (i, br * nj + j))
    return pl.pallas_call(
        _merge_kernel,
        grid=(m // tm, nj),
        in_specs=[y, y, y, w, w, w, g(0), g(1), g(2)],
        out_specs=pl.BlockSpec((tm, tn), lambda i, j: (i, j)),
        out_shape=jax.ShapeDtypeStruct((m, d), BF16),
        compiler_params=_params("parallel", "parallel"),
        name="merge",
    )(ya, yb, yc, wa, wb, wc, gates, gates, gates)


def _proj_ln_kernel(*refs, nk, emit_h):
    a_ref, w_ref, x_ref, gt_ref, lnw_ref, lnb_ref = refs[:6]
    if emit_h:
        sc_ref, sh_ref, xo_ref, ho_ref, acc_ref = refs[6:]
    else:
        xo_ref, acc_ref = refs[6:]
    kk = pl.program_id(1)

    @pl.when(kk == 0)
    def _():
        acc_ref[...] = jnp.zeros_like(acc_ref)

    acc_ref[...] += jnp.dot(a_ref[...], w_ref[...], preferred_element_type=F32)

    @pl.when(kk == nk - 1)
    def _():
        xn = _resid_ln(x_ref[...], acc_ref[...], gt_ref[0], lnw_ref[...], lnb_ref[...])
        xo_ref[...] = xn
        if emit_h:
            ho_ref[...] = (xn * (1.0 + sc_ref[0]) + sh_ref[0]).astype(BF16)


def _proj_ln(a, w, x, gt, lnw, lnb, nxt, seq, *, tk, name):
    m, k = a.shape
    d = w.shape[1]
    tm = TM_LN
    per = seq // tm
    nk = k // tk
    emit_h = nxt is not None
    row = pl.BlockSpec((tm, d), lambda i, kk: (i, 0))
    mod = pl.BlockSpec((1, 1, d), lambda i, kk: (i // per, 0, 0))
    vec = pl.BlockSpec((1, d), lambda i, kk: (0, 0))
    in_specs = [pl.BlockSpec((tm, tk), lambda i, kk: (i, kk)),
                pl.BlockSpec((tk, d), lambda i, kk: (kk, 0)), row, mod, vec, vec]
    args = [a, w, x, gt, lnw.reshape(1, d), lnb.reshape(1, d)]
    out_specs, out_shape = [row], [jax.ShapeDtypeStruct((m, d), F32)]
    if emit_h:
        in_specs += [mod, mod]
        args += list(nxt)
        out_specs.append(row)
        out_shape.append(jax.ShapeDtypeStruct((m, d), BF16))
    res = pl.pallas_call(
        functools.partial(_proj_ln_kernel, nk=nk, emit_h=emit_h),
        grid=(m // tm, nk),
        in_specs=in_specs,
        out_specs=out_specs,
        out_shape=out_shape,
        scratch_shapes=[pltpu.VMEM((tm, d), F32)],
        compiler_params=_params("parallel", "arbitrary"),
        name=name,
    )(*args)
    return (res[0], res[1]) if emit_h else (res[0], None)


def _swiglu_kernel(a_ref, wg_ref, wu_ref, o_ref):
    a = a_ref[...]
    g = jnp.dot(a, wg_ref[...], preferred_element_type=F32)
    u = jnp.dot(a, wu_ref[...], preferred_element_type=F32)
    o_ref[...] = (_silu(g) * u).astype(BF16)


def _swiglu(h, w_up):
    m, k = h.shape
    dff = w_up.shape[1] // 2
    tm, tn = TM_MM, 512
    nj = dff // tn
    return pl.pallas_call(
        _swiglu_kernel,
        grid=(m // tm, nj),
        in_specs=[pl.BlockSpec((tm, k), lambda i, j: (i, 0)),
                  pl.BlockSpec((k, tn), lambda i, j: (0, j)),
                  pl.BlockSpec((k, tn), lambda i, j: (0, nj + j))],
        out_specs=pl.BlockSpec((tm, tn), lambda i, j: (i, j)),
        out_shape=jax.ShapeDtypeStruct((m, dff), BF16),
        compiler_params=_params("parallel", "parallel"),
        name="ffn_up_swiglu",
    )(h, w_up, w_up)


def _layer_weights(l, w_in, rwkv_mu, rwkv_w0, rwkv_w2, rwkv_a0, rwkv_a2, rwkv_g2, rwkv_k_k, rwkv_k_a,
                   rwkv_r_k, rwkv_ln_w, rwkv_ln_b, rwkv_v0, rwkv_v1, rwkv_v2, gdn_conv_w, gdn_a_log,
                   gdn_dt_bias):
    c = MIX
    wl = w_in[l]
    o_rwkv = c
    o_lora = o_rwkv + 3 * c
    n_lora = rwkv_w2.shape[1] + rwkv_a2.shape[1] + rwkv_g2.shape[1]
    o_gdn = o_lora + n_lora
    o_ab = o_gdn + 4 * c
    o_gate = o_ab + 2 * GDN_HEADS
    pad_cols = N_SMALL - n_lora - 2 * GDN_HEADS
    w_main = jnp.concatenate([wl[:, :o_lora], wl[:, o_gdn:o_ab]], axis=1).astype(BF16)
    assert n_lora - (N_SMALL - LANES) == GDN_G_LANE and GDN_B_LANE == GDN_G_LANE + GDN_HEADS
    w_small = jnp.concatenate([wl[:, o_lora:o_gdn], wl[:, o_ab:o_gate],
                               jnp.zeros((wl.shape[0], pad_cols), wl.dtype)], axis=1).astype(BF16)
    w_gate = wl[:, o_gate:].astype(BF16)

    mu = rwkv_mu[l]
    row = lambda v: v.reshape(1, -1)
    n_w, n_a, n_g = rwkv_w2.shape[1], rwkv_a2.shape[1], rwkv_g2.shape[1]
    padrows = lambda w, lo: jnp.zeros((N_SMALL, c), F32).at[lo:lo + w.shape[0]].set(w).astype(BF16)
    lane_vec = lambda v, lo: jnp.zeros((1, LANES), F32).at[0, lo:lo + v.shape[0]].set(v)
    rw = {
        "mu_r": row(mu[:c]), "mu_k": row(mu[c:2 * c]), "mu_v": row(mu[2 * c:3 * c]),
        "mu_s": jnp.zeros((1, N_SMALL), F32).at[0, :n_lora].set(mu[3 * c:]),
        "w0": row(rwkv_w0[l]), "a0": row(rwkv_a0[l]), "k_k": row(rwkv_k_k[l]), "k_a": row(rwkv_k_a[l]),
        "w2p": padrows(rwkv_w2[l], 0), "a2p": padrows(rwkv_a2[l], n_w), "g2p": padrows(rwkv_g2[l], n_w + n_a),
        "r_k": row(rwkv_r_k[l]), "ln_w": row(rwkv_ln_w[l]), "ln_b": row(rwkv_ln_b[l]),
    }
    if l > 0:
        nv = rwkv_v1.shape[2]
        rw["v0"] = row(rwkv_v0[l - 1])
        rw["v1p"] = jnp.zeros((c, LANES), F32).at[:, :nv].set(rwkv_v1[l - 1]).astype(BF16)
        rw["v2p"] = jnp.zeros((LANES, c), F32).at[:nv].set(rwkv_v2[l - 1]).astype(BF16)
    cw = gdn_conv_w[l]
    gd = {"cw_q": cw[:, :c], "cw_k": cw[:, c:2 * c], "cw_v": cw[:, 2 * c:],
          "alog": lane_vec(gdn_a_log[l], GDN_G_LANE), "dtb": lane_vec(gdn_dt_bias[l], GDN_G_LANE)}
    return w_main, w_small, w_gate, rw, gd


def kernel(x, c, ada_w, ada_b, w_in, pool_w, pool_scale, rwkv_mu, rwkv_w0, rwkv_w2, rwkv_a0, rwkv_a2, rwkv_g2, rwkv_k_k, rwkv_k_a, rwkv_r_k, rwkv_ln_w, rwkv_ln_b, rwkv_v0, rwkv_v1, rwkv_v2, gdn_conv_w, gdn_a_log, gdn_dt_bias, gdn_norm_w, w_branch_a, w_branch_b, w_branch_c, w_out, ln1_w, ln1_b, ffn_w_up, ffn_w_down, ln2_w, ln2_b):
    batch, seq, d = x.shape
    assert seq % TM_MM == 0 and d == 2 * MIX
    m = batch * seq
    xf = x.reshape(m, d)

    mod = _ada(c, ada_w, ada_b)
    mods = [[mod[l, :, i * d:(i + 1) * d].reshape(batch, 1, d) for i in range(6)]
            for l in range(DEPTH)]

    h = _modulate(xf, mods[0][1], mods[0][0], seq)
    v_first = None
    for l in range(DEPTH):
        sh_m, sc_m, gt_m, sh_f, sc_f, gt_f = mods[l]
        w_main, w_small, w_gate, rw, gd = _layer_weights(
            l, w_in, rwkv_mu, rwkv_w0, rwkv_w2, rwkv_a0, rwkv_a2, rwkv_g2, rwkv_k_k, rwkv_k_a,
            rwkv_r_k, rwkv_ln_w, rwkv_ln_b, rwkv_v0, rwkv_v1, rwkv_v2, gdn_conv_w, gdn_a_log,
            gdn_dt_bias)

        p = _matmul(h, w_main, tn=1024, out_dtype=F32, name="in_proj_main")
        psm = _matmul(h, w_small, tn=N_SMALL, out_dtype=F32, name="in_proj_small")
        gates = _matmul(h, w_gate, tn=1024, out_dtype=BF16, act="sigmoid", name="in_proj_gates")

        y_a = _pool(p, pool_w[l], pool_scale[l], seq)
        prep = _rwkv_prep(p, psm, rw, v_first, seq)
        if l == 0:
            v_first = prep[2]
        y_b = _rwkv_core(prep, rw, batch, seq)
        gq, gk, gv, ggb = _gdn_prep(p, psm, gd, seq)
        y_c = _gdn_core(gq, gk, gv, ggb, p, gdn_norm_w[l], batch, seq)

        merged = _merge(y_a, y_b, y_c, w_branch_a[l].astype(BF16), w_branch_b[l].astype(BF16),
                        w_branch_c[l].astype(BF16), gates)
        xf, h = _proj_ln(merged, w_out[l].astype(BF16), xf, gt_m, ln1_w[l], ln1_b[l],
                         (sc_f, sh_f), seq, tk=1024, name="out_proj_ln")

        act = _swiglu(h, ffn_w_up[l].astype(BF16))
        nxt = (mods[l + 1][1], mods[l + 1][0]) if l + 1 < DEPTH else None
        xf, h = _proj_ln(act, ffn_w_down[l].astype(BF16), xf, gt_f, ln2_w[l], ln2_b[l],
                         nxt, seq, tk=512, name="ffn_down_ln")
    return xf.reshape(batch, seq, d)
```

```python
import functools

import jax
import jax.numpy as jnp
from jax import lax
from jax.experimental import pallas as pl
from jax.experimental.pallas import tpu as pltpu

F32 = jnp.float32
BF16 = jnp.bfloat16

DEPTH = 2
DN_ALPHA = (2 * DEPTH) ** 0.25
LN_EPS = 1e-5
MIX = 1024
POOL_WINDOWS = (2, 4, 8, 16)
POOL_GROUP = 256
RWKV_HEAD = 64
RWKV_LN_EPS = RWKV_HEAD * 1e-5
GDN_HEAD = 128
GDN_HEADS = 8
GDN_EPS = 1e-6
CHUNK = 64
LANES = 128
N_SMALL = 384
GDN_G_LANE = 32
GDN_B_LANE = 40
VMEM_LIMIT = 56 * 1024 * 1024

TM_MM = 1024
TM_LN = 512
TM_PREP = 256
RB_RWKV = 128
RB_GDN = 128


def _dot(a, b):
    return jnp.dot(a.astype(BF16), b.astype(BF16), preferred_element_type=F32)


def _dot_nt(a, b):
    return lax.dot_general(a.astype(BF16), b.astype(BF16), (((1,), (1,)), ((), ())),
                           preferred_element_type=F32)


def _split(a, n):
    terms, rest = [], a
    for _ in range(n):
        t = rest.astype(BF16)
        terms.append(t)
        rest = rest - t.astype(F32)
    return terms


def _dot3(a, b, nt=False):
    d = _dot_nt if nt else _dot
    a1, a2 = _split(a, 2)
    b1, b2 = _split(b, 2)
    return d(a1, b1) + (d(a1, b2) + d(a2, b1))


def _dot_sel_l(sel, b, n):
    out = None
    for t in _split(b, n):
        p = jnp.dot(sel, t, preferred_element_type=F32)
        out = p if out is None else out + p
    return out


def _dot_sel_r(a, sel, n):
    out = None
    for t in _split(a, n):
        p = jnp.dot(t, sel, preferred_element_type=F32)
        out = p if out is None else out + p
    return out


def _sigmoid(x):
    return 1.0 / (1.0 + jnp.exp(-x))


def _silu(x):
    return x * _sigmoid(x)


def _softplus(x):
    return jnp.maximum(x, 0.0) + jnp.log(1.0 + jnp.exp(-jnp.abs(x)))


def _iota2(shape, dim):
    return lax.broadcasted_iota(jnp.int32, shape, dim)


def _block_masks(n):
    r = _iota2((n, n), 0)
    c = _iota2((n, n), 1)
    same = lambda s: (r >> s) == (c >> s)
    m = {
        "eye": (r == c).astype(F32),
        "bd8": same(3),
        "off16": same(4) & jnp.logical_not(same(3)),
        "off32": same(5) & jnp.logical_not(same(4)),
        "off64": same(6) & jnp.logical_not(same(5)),
        "bd64": same(6),
    }
    m["strict"] = m["bd64"] & (r > c)
    m["incl"] = m["bd64"] & (r >= c)
    return m


def _chunk_cumsum_matrix(n):
    r = _iota2((n, n), 0)
    c = _iota2((n, n), 1)
    return (((r >> 6) == (c >> 6)) & (c <= r)).astype(BF16)


def _tri_inv(mats, m):
    a8 = [jnp.where(m["bd8"], a, 0.0) for a in mats]
    s = [m["eye"] + x for x in a8]
    p = [_dot(x, x) for x in a8]
    q = [_dot(pi, jnp.concatenate([si, pi], axis=1)) for pi, si in zip(p, s)]
    s = [si + qi[:, :LANES] for si, qi in zip(s, q)]
    s = [si + _dot(qi[:, LANES:], si) for si, qi in zip(s, q)]
    for key in ("off16", "off32", "off64"):
        t = [_dot(jnp.where(m[key], a, 0.0), si) for a, si in zip(mats, s)]
        s = [si + _dot(si, ti) for si, ti in zip(s, t)]
    return s


def _shift_rows(x, halo, s):
    xr = pltpu.roll(x, s, 0)
    hr = pltpu.roll(halo, s, 0)
    rows = _iota2(halo.shape, 0)
    top = jnp.where(rows < s, hr, xr[:8])
    return jnp.concatenate([top, xr[8:]], axis=0)


def _resid_ln(x, y, gt, lnw, lnb):
    z = DN_ALPHA * x + (1.0 + gt) * y
    mu = jnp.mean(z, axis=-1, keepdims=True)
    zc = z - mu
    var = jnp.mean(zc * zc, axis=-1, keepdims=True)
    return zc * lax.rsqrt(var + LN_EPS) * lnw + lnb


def _params(*sem):
    return pltpu.CompilerParams(dimension_semantics=sem, vmem_limit_bytes=VMEM_LIMIT)


def _ada_kernel(c_ref, w_ref, b_ref, o_ref):
    c = c_ref[...]
    o_ref[0] = _dot3(_silu(c), w_ref[0]) + b_ref[0]


def _ada(c, ada_w, ada_b):
    nl, d, n = ada_w.shape
    b = c.shape[0]
    tn = 512
    cp = jnp.zeros((8, d), F32).at[:b].set(c)
    out = pl.pallas_call(
        _ada_kernel,
        grid=(nl, n // tn),
        in_specs=[pl.BlockSpec((8, d), lambda l, j: (0, 0)),
                  pl.BlockSpec((1, d, tn), lambda l, j: (l, 0, j)),
                  pl.BlockSpec((1, 1, tn), lambda l, j: (l, 0, j))],
        out_specs=pl.BlockSpec((1, 8, tn), lambda l, j: (l, 0, j)),
        out_shape=jax.ShapeDtypeStruct((nl, 8, n), F32),
        compiler_params=_params("parallel", "parallel"),
        name="ada_mod",
    )(cp, ada_w, ada_b.reshape(nl, 1, n))
    return out[:, :b]


def _modulate_kernel(x_ref, sc_ref, sh_ref, h_ref):
    h_ref[...] = (x_ref[...] * (1.0 + sc_ref[0]) + sh_ref[0]).astype(BF16)


def _modulate(x, sc, sh, seq):
    m, d = x.shape
    tm = TM_LN
    per = seq // tm
    mod = pl.BlockSpec((1, 1, d), lambda i: (i // per, 0, 0))
    return pl.pallas_call(
        _modulate_kernel,
        grid=(m // tm,),
        in_specs=[pl.BlockSpec((tm, d), lambda i: (i, 0)), mod, mod],
        out_specs=pl.BlockSpec((tm, d), lambda i: (i, 0)),
        out_shape=jax.ShapeDtypeStruct((m, d), BF16),
        compiler_params=_params("parallel"),
        name="modulate",
    )(x, sc, sh)


def _mm_kernel(a_ref, b_ref, o_ref, *, act):
    acc = jnp.dot(a_ref[...], b_ref[...], preferred_element_type=F32)
    if act == "sigmoid":
        acc = _sigmoid(acc)
    o_ref[...] = acc.astype(o_ref.dtype)


def _matmul(a, b, *, tn, out_dtype, act=None, name):
    m, k = a.shape
    n = b.shape[1]
    tm = TM_MM
    return pl.pallas_call(
        functools.partial(_mm_kernel, act=act),
        grid=(m // tm, n // tn),
        in_specs=[pl.BlockSpec((tm, k), lambda i, j: (i, 0)),
                  pl.BlockSpec((k, tn), lambda i, j: (0, j))],
        out_specs=pl.BlockSpec((tm, tn), lambda i, j: (i, j)),
        out_shape=jax.ShapeDtypeStruct((m, n), out_dtype),
        compiler_params=_params("parallel", "parallel"),
        name=name,
    )(a, b)


def _pool_kernel(p_ref, halo_ref, w_ref, scale_ref, o_ref, *, seq, tm):
    t0 = (pl.program_id(0) * tm) % seq
    x = p_ref[...]
    halo = jnp.where(t0 == 0, 0.0, halo_ref[...])
    ext = jnp.concatenate([halo, x], axis=0)
    sums = [ext]
    for sh in (1, 2, 4, 8):
        sums.append(sums[-1] + pltpu.roll(sums[-1], sh, 0))
    t = t0 + _iota2((tm, POOL_GROUP), 0)
    ys = []
    for g, win in enumerate(POOL_WINDOWS):
        cols = slice(g * POOL_GROUP, (g + 1) * POOL_GROUP)
        cnt = jnp.minimum(t + 1, win).astype(F32)
        pooled = sums[g + 1][16:, cols] / cnt - x[:, cols]
        ys.append(_dot(pooled, w_ref[g]))
    o_ref[...] = (jnp.concatenate(ys, axis=1) * scale_ref[...]).astype(BF16)


def _pool(p, pool_w, pool_scale, seq):
    m = p.shape[0]
    tm = TM_LN
    return pl.pallas_call(
        functools.partial(_pool_kernel, seq=seq, tm=tm),
        grid=(m // tm,),
        in_specs=[pl.BlockSpec((tm, MIX), lambda i: (i, 0)),
                  pl.BlockSpec((16, MIX), lambda i: (jnp.maximum(i * (tm // 16) - 1, 0), 0)),
                  pl.BlockSpec(pool_w.shape, lambda i: (0, 0, 0)),
                  pl.BlockSpec((1, MIX), lambda i: (0, 0))],
        out_specs=pl.BlockSpec((tm, MIX), lambda i: (i, 0)),
        out_shape=jax.ShapeDtypeStruct((m, MIX), BF16),
        compiler_params=_params("parallel"),
        name="pool_mixer",
    )(p, p, pool_w.astype(BF16), pool_scale.reshape(1, MIX))


def _rwkv_prep_kernel(*refs, seq, tm, has_vres):
    (pr, pk, pv, hr, hk, hv, ps, hs, mur, muk, muv, mus, w0, a0, kkp, kap,
     w2p, a2p, g2p) = refs[:19]
    if has_vres:
        vf, v0, v1p, v2p = refs[19:23]
        outs = refs[23:]
    else:
        outs = refs[19:]
    r_o, k_o, v_o, kk_o, a_o, ld_o, g_o = outs

    first = (pl.program_id(0) * tm) % seq == 0

    def lerp(p_ref, h_ref, mu_ref):
        p = p_ref[...]
        prev = _shift_rows(p, jnp.where(first, 0.0, h_ref[...]), 1)
        return p + (prev - p) * mu_ref[...]

    xr = lerp(pr, hr, mur)
    xk = lerp(pk, hk, muk)
    xv = lerp(pv, hv, muv)
    xs = lerp(ps, hs, mus)

    w = -_softplus(-(w0[...] + _dot(jnp.tanh(xs), w2p[...]))) - 0.5
    a = _sigmoid(a0[...] + _dot(xs, a2p[...]))
    g = _dot(_sigmoid(xs), g2p[...])
    if has_vres:
        gate = _sigmoid(v0[...] + _dot(_dot(xv, v1p[...]), v2p[...]))
        xv = xv + (vf[...] - xv) * gate

    kk = xk * kkp[...]
    r2 = _iota2((LANES, LANES), 0)
    c2 = _iota2((LANES, LANES), 1)
    head_ones = ((r2 >> 6) == (c2 >> 6)).astype(BF16)
    kk2 = kk * kk
    ss = jnp.concatenate(
        [_dot_sel_r(kk2[:, i * LANES:(i + 1) * LANES], head_ones, 2) for i in range(MIX // LANES)],
        axis=1)
    kk = kk / jnp.maximum(jnp.sqrt(ss), 1e-12)

    r_o[...] = xr
    k_o[...] = xk * (1.0 + (a - 1.0) * kap[...])
    v_o[...] = xv
    kk_o[...] = kk
    a_o[...] = a
    ld_o[...] = -jnp.exp(w)
    g_o[...] = g


def _rwkv_prep(p, psm, wts, v_first, seq):
    m = p.shape[0]
    tm = TM_PREP
    has_vres = v_first is not None
    tile = lambda cb: pl.BlockSpec((tm, MIX), lambda i, cb=cb: (i, cb))
    halo = lambda cb: pl.BlockSpec((8, MIX), lambda i, cb=cb: (jnp.maximum(i * (tm // 8) - 1, 0), cb))
    vec = pl.BlockSpec((1, MIX), lambda i: (0, 0))
    full = lambda arr: pl.BlockSpec(arr.shape, lambda i: (0,) * arr.ndim)
    in_specs = [tile(1), tile(2), tile(3), halo(1), halo(2), halo(3),
                pl.BlockSpec((tm, N_SMALL), lambda i: (i, 0)),
                pl.BlockSpec((8, N_SMALL), lambda i: (jnp.maximum(i * (tm // 8) - 1, 0), 0)),
                vec, vec, vec, pl.BlockSpec((1, N_SMALL), lambda i: (0, 0)),
                vec, vec, vec, vec,
                full(wts["w2p"]), full(wts["a2p"]), full(wts["g2p"])]
    args = [p, p, p, p, p, p, psm, psm, wts["mu_r"], wts["mu_k"], wts["mu_v"], wts["mu_s"],
            wts["w0"], wts["a0"], wts["k_k"], wts["k_a"], wts["w2p"], wts["a2p"], wts["g2p"]]
    if has_vres:
        in_specs += [pl.BlockSpec((tm, MIX), lambda i: (i, 0)), vec, full(wts["v1p"]), full(wts["v2p"])]
        args += [v_first, wts["v0"], wts["v1p"], wts["v2p"]]
    out = pl.BlockSpec((tm, MIX), lambda i: (i, 0))
    return pl.pallas_call(
        functools.partial(_rwkv_prep_kernel, seq=seq, tm=tm, has_vres=has_vres),
        grid=(m // tm,),
        in_specs=in_specs,
        out_specs=[out] * 7,
        out_shape=[jax.ShapeDtypeStruct((m, MIX), F32)] * 7,
        compiler_params=_params("parallel"),
        name="rwkv_prep",
    )(*args)


def _rwkv_core_kernel(r_ref, k_ref, v_ref, kk_ref, a_ref, ld_ref, g_ref, rk_ref, lnw_ref, lnb_ref,
                      o_ref, h_ref):
    @pl.when(pl.program_id(1) == 0)
    def _():
        h_ref[...] = jnp.zeros_like(h_ref)

    nb, rb, _ = r_ref.shape
    nch = rb // CHUNK
    units = [(b, c) for c in range(nch) for b in range(nb)]
    m = _block_masks(LANES)
    lane = _iota2((1, LANES), 1)
    m0 = (lane < RWKV_HEAD).astype(F32)
    m1 = 1.0 - m0
    r2 = _iota2((LANES, LANES), 0)
    c2 = _iota2((LANES, LANES), 1)
    head_ones = ((r2 >> 6) == (c2 >> 6)).astype(BF16)
    stack = lambda x: jnp.concatenate([x * m0, x * m1], axis=0)
    unstack = lambda x: x[:CHUNK] + x[CHUNK:]
    cat0 = lambda *xs: jnp.concatenate(xs, axis=0)
    cat1 = lambda *xs: jnp.concatenate(xs, axis=1)

    cum = _chunk_cumsum_matrix(rb)
    gcum = [_dot_sel_l(cum, ld_ref[b], 3) for b in range(nb)]

    def load(ref, u):
        b, c = u
        return ref[b, c * CHUNK:(c + 1) * CHUNK, :]

    a_st, r_st, rhs, v_st, bbar_t, kbar_t, dlast, vs = [], [], [], [], [], [], [], []
    for u in units:
        b, c = u
        gc = gcum[b][c * CHUNK:(c + 1) * CHUNK]
        ldc, kc, vc, kkc = load(ld_ref, u), load(k_ref, u), load(v_ref, u), load(kk_ref, u)
        bc = kkc * load(a_ref, u)
        glast = gc[CHUNK - 1:CHUNK]
        e_neg = jnp.exp(-gc)
        e_rest = jnp.exp(glast - gc)
        a_st.append(stack(-kkc * jnp.exp(gc - ldc)))
        r_st.append(stack(load(r_ref, u) * jnp.exp(gc)))
        b_hat, k_hat = bc * e_neg, kc * e_neg
        rhs.append(cat0(b_hat, b_hat, k_hat, k_hat))
        v_st.append(stack(vc))
        bbar_t.append((bc * e_rest).T)
        kbar_t.append((kc * e_rest).T)
        dlast.append(jnp.exp(glast))
        vs.append(vc)

    big = [_dot_nt(cat0(a, r), x) for a, r, x in zip(a_st, r_st, rhs)]
    a_ab = [jnp.where(m["strict"], x[:LANES, :LANES], 0.0) for x in big]
    a_kr = [cat0(jnp.where(m["strict"], x[:LANES, LANES:], 0.0),
                 jnp.where(m["incl"], x[LANES:, LANES:], 0.0)) for x in big]
    a_rb = [jnp.where(m["incl"], x[LANES:, :LANES], 0.0) for x in big]
    t_inv = _tri_inv(a_ab, m)
    x1 = [_dot(a, v) for a, v in zip(a_kr, v_st)]
    x2 = [_dot(t, cat1(a, x[:LANES])) for t, a, x in zip(t_inv, a_st, x1)]
    x3 = [_dot(a, x) for a, x in zip(a_rb, x2)]
    q_t = [unstack(r + x[:, :LANES]) for r, x in zip(r_st, x3)]
    y0 = [unstack(x[:, LANES:] + xx[LANES:]) for x, xx in zip(x3, x1)]
    w_t = [unstack(x[:, :LANES]) for x in x2]
    u_t = [unstack(x[:, LANES:]) for x in x2]
    phi = [m["eye"] * d + jnp.where(m["bd64"], _dot(bt, w), 0.0) for d, bt, w in zip(dlast, bbar_t, w_t)]
    psi = [jnp.where(m["bd64"], _dot(cat1(bt, kt), cat0(u, v)), 0.0)
           for bt, kt, u, v in zip(bbar_t, kbar_t, u_t, vs)]

    ys = {}
    hs = [h_ref[b] for b in range(nb)]
    for i, (b, c) in enumerate(units):
        ys[(b, c)] = y0[i] + _dot3(q_t[i], hs[b])
        hs[b] = _dot3(phi[i], hs[b]) + psi[i]
    for b in range(nb):
        h_ref[b] = hs[b]

    y = cat0(*[ys[(b, c)] for b in range(nb) for c in range(nch)])
    flat = lambda ref: ref[...].reshape(nb * rb, LANES)
    rr, kk_, vv = flat(r_ref), flat(k_ref), flat(v_ref)
    mean = _dot_sel_r(y, head_ones, 2) * (1.0 / RWKV_HEAD)
    yc = y - mean
    var = _dot_sel_r(yc * yc, head_ones, 2) * (1.0 / RWKV_HEAD)
    yn = yc * lax.rsqrt(var + RWKV_LN_EPS) * lnw_ref[...] + lnb_ref[...]
    bonus = _dot_sel_r(rr * kk_ * rk_ref[...], head_ones, 2) * vv
    o_ref[...] = ((yn + bonus) * flat(g_ref)).astype(BF16).reshape(nb, rb, LANES)


def _rwkv_core(prep, wts, batch, seq):
    rb = RB_RWKV
    tile = pl.BlockSpec((batch, rb, LANES), lambda p, j: (0, j, p))
    vec = pl.BlockSpec((1, LANES), lambda p, j: (0, p))
    out = pl.pallas_call(
        _rwkv_core_kernel,
        grid=(MIX // LANES, seq // rb),
        in_specs=[tile] * 7 + [vec] * 3,
        out_specs=tile,
        out_shape=jax.ShapeDtypeStruct((batch, seq, MIX), BF16),
        scratch_shapes=[pltpu.VMEM((batch, LANES, LANES), F32)],
        compiler_params=_params("parallel", "arbitrary"),
        name="rwkv_core",
    )(*[a.reshape(batch, seq, MIX) for a in prep], wts["r_k"], wts["ln_w"], wts["ln_b"])
    return out.reshape(batch * seq, MIX)


def _gdn_prep_kernel(pq, pk, pv, hq, hk, hv, ps, cwq, cwk, cwv, alog, dtb,
                     q_o, k_o, v_o, gb_o, *, seq, tm):
    first = (pl.program_id(0) * tm) % seq == 0
    ones = jnp.ones((LANES, LANES), BF16)

    def conv_silu(p_ref, h_ref, w_ref):
        x = p_ref[...]
        halo = jnp.where(first, 0.0, h_ref[...])
        w = w_ref[...]
        y = x * w[3:4]
        for s in (1, 2, 3):
            y = y + _shift_rows(x, halo, s) * w[3 - s:4 - s]
        return _silu(y)

    def l2norm(y, scale):
        blocks = []
        for i in range(GDN_HEADS):
            blk = y[:, i * LANES:(i + 1) * LANES]
            ss = _dot_sel_r(blk * blk, ones, 2)
            blocks.append(blk * (lax.rsqrt(ss + GDN_EPS) * scale))
        return jnp.concatenate(blocks, axis=1)

    q_o[...] = l2norm(conv_silu(pq, hq, cwq), GDN_HEAD ** -0.5)
    k_o[...] = l2norm(conv_silu(pk, hk, cwk), 1.0)
    v_o[...] = conv_silu(pv, hv, cwv)

    raw = ps[:, N_SMALL - LANES:]
    gval = -jnp.exp(alog[...]) * _softplus(raw + dtb[...])
    gcum = _dot_sel_l(_chunk_cumsum_matrix(tm), gval, 3)
    lane = _iota2(raw.shape, 1)
    is_g = (lane >= GDN_G_LANE) & (lane < GDN_G_LANE + GDN_HEADS)
    is_b = (lane >= GDN_B_LANE) & (lane < GDN_B_LANE + GDN_HEADS)
    gb_o[...] = jnp.where(is_g, gcum, jnp.where(is_b, _sigmoid(raw), 0.0))


def _gdn_prep(p, psm, wts, seq):
    m = p.shape[0]
    tm = TM_PREP
    tile = lambda cb: pl.BlockSpec((tm, MIX), lambda i, cb=cb: (i, cb))
    halo = lambda cb: pl.BlockSpec((8, MIX), lambda i, cb=cb: (jnp.maximum(i * (tm // 8) - 1, 0), cb))
    cw = pl.BlockSpec((4, MIX), lambda i: (0, 0))
    lv = pl.BlockSpec((1, LANES), lambda i: (0, 0))
    out = pl.BlockSpec((tm, MIX), lambda i: (i, 0))
    return pl.pallas_call(
        functools.partial(_gdn_prep_kernel, seq=seq, tm=tm),
        grid=(m // tm,),
        in_specs=[tile(4), tile(5), tile(6), halo(4), halo(5), halo(6),
                  pl.BlockSpec((tm, N_SMALL), lambda i: (i, 0)), cw, cw, cw, lv, lv],
        out_specs=[out, out, out, pl.BlockSpec((tm, LANES), lambda i: (i, 0))],
        out_shape=[jax.ShapeDtypeStruct((m, MIX), F32)] * 3 + [jax.ShapeDtypeStruct((m, LANES), F32)],
        compiler_params=_params("parallel"),
        name="gdn_prep",
    )(p, p, p, p, p, p, psm, wts["cw_q"], wts["cw_k"], wts["cw_v"], wts["alog"], wts["dtb"])


def _gdn_core_kernel(q_ref, k_ref, v_ref, gb_ref, z_ref, nw_ref, o_ref, s_ref):
    @pl.when(pl.program_id(1) == 0)
    def _():
        s_ref[...] = jnp.zeros_like(s_ref)

    nb, rb, width = q_ref.shape
    nh = width // LANES
    units = [(b, hh) for hh in range(nh) for b in range(nb)]
    m = _block_masks(LANES)
    ones = jnp.ones((LANES, LANES), BF16)
    cat0 = lambda *xs: jnp.concatenate(xs, axis=0)
    cat1 = lambda *xs: jnp.concatenate(xs, axis=1)

    ri = _iota2((LANES, 2 * nh * LANES), 0)
    ci = _iota2((LANES, 2 * nh * LANES), 1)
    src = nh * pl.program_id(0) + (ci >> 8) + jnp.where(((ci >> 7) & 1) == 0, GDN_G_LANE, GDN_B_LANE)
    sel = (ri == src).astype(BF16)
    gbb = [_dot_sel_r(gb_ref[b], sel, 3) for b in range(nb)]

    gf, ks, lhs, rhs, e_g, qs, gamma = [], [], [], [], [], [], []
    for b, hh in units:
        hl = slice(hh * LANES, (hh + 1) * LANES)
        g = gbb[b][:, 2 * hh * LANES:(2 * hh + 1) * LANES]
        beta = gbb[b][:, (2 * hh + 1) * LANES:(2 * hh + 2) * LANES]
        q, k, v = q_ref[b, :, hl], k_ref[b, :, hl], v_ref[b, :, hl]
        diff = g - g.T
        gamma.append(jnp.where(m["incl"], jnp.exp(jnp.where(m["incl"], diff, 0.0)), 0.0))
        eg = jnp.exp(g)
        kb = k * beta
        gf.append(g)
        ks.append(k)
        qs.append(q)
        e_g.append(eg)
        lhs.append(cat0(kb, q))
        rhs.append(cat1(v * beta, kb * eg))

    kq = [_dot_nt(x, k) for x, k in zip(lhs, ks)]
    a_neg = [-jnp.where(m["strict"], x[:LANES] * gm, 0.0) for x, gm in zip(kq, gamma)]
    a_qk = [x[LANES:] * gm for x, gm in zip(kq, gamma)]
    t_inv = _tri_inv(a_neg, m)
    uw = [_dot(t, x) for t, x in zip(t_inv, rhs)]
    x3 = [_dot(a, x) for a, x in zip(a_qk, uw)]
    q_t = [q * eg - x[:, LANES:] for q, eg, x in zip(qs, e_g, x3)]
    phi, psi = {}, {}
    nch = rb // CHUNK
    for c in range(nch):
        cs = slice(c * CHUNK, (c + 1) * CHUNK)
        for i in range(len(units)):
            glast = gf[i][(c + 1) * CHUNK - 1:(c + 1) * CHUNK]
            k_bar_t = (ks[i][cs] * jnp.exp(glast - gf[i][cs])).T
            pp = _dot(k_bar_t, uw[i][cs])
            phi[(i, c)] = m["eye"] * jnp.exp(glast) - pp[:, LANES:]
            psi[(i, c)] = pp[:, :LANES]

    ys = [[] for _ in units]
    st = [s_ref[hh, b] for b, hh in units]
    for c in range(nch):
        cs = slice(c * CHUNK, (c + 1) * CHUNK)
        for i in range(len(units)):
            ys[i].append(x3[i][cs, :LANES] + _dot3(q_t[i][cs], st[i]))
            st[i] = _dot3(phi[(i, c)], st[i]) + psi[(i, c)]
    for i, (b, hh) in enumerate(units):
        s_ref[hh, b] = st[i]

    for hh in range(nh):
        hl = slice(hh * LANES, (hh + 1) * LANES)
        y = cat0(*[yc for i, (b, h2) in enumerate(units) if h2 == hh for yc in ys[i]])
        ms = _dot_sel_r(y * y, ones, 2) * (1.0 / GDN_HEAD)
        out = y * lax.rsqrt(ms + GDN_EPS) * nw_ref[...]
        z = z_ref[:, :, hl].reshape(nb * rb, LANES)
        o_ref[:, :, hl] = (out * _silu(z)).astype(BF16).reshape(nb, rb, LANES)


def _gdn_core(q, k, v, gb, p, norm_w, batch, seq):
    rb, nh = RB_GDN, 2
    width = nh * LANES
    tile = pl.BlockSpec((batch, rb, width), lambda h, j: (0, j, h))
    z_col0 = 7 * (MIX // width)
    r3 = lambda a: a.reshape(batch, seq, a.shape[-1])
    out = pl.pallas_call(
        _gdn_core_kernel,
        grid=(GDN_HEADS // nh, seq // rb),
        in_specs=[tile, tile, tile,
                  pl.BlockSpec((batch, rb, LANES), lambda h, j: (0, j, 0)),
                  pl.BlockSpec((batch, rb, width), lambda h, j: (0, j, z_col0 + h)),
                  pl.BlockSpec((1, LANES), lambda h, j: (0, 0))],
        out_specs=tile,
        out_shape=jax.ShapeDtypeStruct((batch, seq, MIX), BF16),
        scratch_shapes=[pltpu.VMEM((nh, batch, LANES, LANES), F32)],
        compiler_params=_params("parallel", "arbitrary"),
        name="gdn_core",
    )(r3(q), r3(k), r3(v), r3(gb), r3(p), norm_w.reshape(1, LANES))
    return out.reshape(batch * seq, MIX)


def _merge_kernel(ya, yb, yc, wa, wb, wc, ga, gb, gc, o_ref):
    dot = lambda y, w: jnp.dot(y[...], w[...], preferred_element_type=F32)
    acc = ga[...].astype(F32) * dot(ya, wa)
    acc = acc + gb[...].astype(F32) * dot(yb, wb)
    acc = acc + gc[...].astype(F32) * dot(yc, wc)
    o_ref[...] = acc.astype(BF16)


def _merge(ya, yb, yc, wa, wb, wc, gates):
    m = ya.shape[0]
    d = wa.shape[1]
    tm, tn = TM_LN, 1024
    nj = d // tn
    y = pl.BlockSpec((tm, MIX), lambda i, j: (i, 0))
    w = pl.BlockSpec((MIX, tn), lambda i, j: (0, j))
    g = lambda br: pl.BlockSpec((tm, tn), lambda i, j, br=br: (i, br * nj + j))
    return pl.pallas_call(
        _merge_kernel,
        grid=(m // tm, nj),
        in_specs=[y, y, y, w, w, w, g(0), g(1), g(2)],
        out_specs=pl.BlockSpec((tm, tn), lambda i, j: (i, j)),
        out_shape=jax.ShapeDtypeStruct((m, d), BF16),
        compiler_params=_params("parallel", "parallel"),
        name="merge",
    )(ya, yb, yc, wa, wb, wc, gates, gates, gates)


def _proj_ln_kernel(*refs, nk, emit_h):
    a_ref, w_ref, x_ref, gt_ref, lnw_ref, lnb_ref = refs[:6]
    if emit_h:
        sc_ref, sh_ref, xo_ref, ho_ref, acc_ref = refs[6:]
    else:
        xo_ref, acc_ref = refs[6:]
    kk = pl.program_id(1)

    @pl.when(kk == 0)
    def _():
        acc_ref[...] = jnp.zeros_like(acc_ref)

    acc_ref[...] += jnp.dot(a_ref[...], w_ref[...], preferred_element_type=F32)

    @pl.when(kk == nk - 1)
    def _():
        xn = _resid_ln(x_ref[...], acc_ref[...], gt_ref[0], lnw_ref[...], lnb_ref[...])
        xo_ref[...] = xn
        if emit_h:
            ho_ref[...] = (xn * (1.0 + sc_ref[0]) + sh_ref[0]).astype(BF16)


def _proj_ln(a, w, x, gt, lnw, lnb, nxt, seq, *, tk, name):
    m, k = a.shape
    d = w.shape[1]
    tm = TM_LN
    per = seq // tm
    nk = k // tk
    emit_h = nxt is not None
    row = pl.BlockSpec((tm, d), lambda i, kk: (i, 0))
    mod = pl.BlockSpec((1, 1, d), lambda i, kk: (i // per, 0, 0))
    vec = pl.BlockSpec((1, d), lambda i, kk: (0, 0))
    in_specs = [pl.BlockSpec((tm, tk), lambda i, kk: (i, kk)),
                pl.BlockSpec((tk, d), lambda i, kk: (kk, 0)), row, mod, vec, vec]
    args = [a, w, x, gt, lnw.reshape(1, d), lnb.reshape(1, d)]
    out_specs, out_shape = [row], [jax.ShapeDtypeStruct((m, d), F32)]
    if emit_h:
        in_specs += [mod, mod]
        args += list(nxt)
        out_specs.append(row)
        out_shape.append(jax.ShapeDtypeStruct((m, d), BF16))
    res = pl.pallas_call(
        functools.partial(_proj_ln_kernel, nk=nk, emit_h=emit_h),
        grid=(m // tm, nk),
        in_specs=in_specs,
        out_specs=out_specs,
        out_shape=out_shape,
        scratch_shapes=[pltpu.VMEM((tm, d), F32)],
        compiler_params=_params("parallel", "arbitrary"),
        name=name,
    )(*args)
    return (res[0], res[1]) if emit_h else (res[0], None)


def _swiglu_kernel(a_ref, wg_ref, wu_ref, o_ref):
    a = a_ref[...]
    g = jnp.dot(a, wg_ref[...], preferred_element_type=F32)
    u = jnp.dot(a, wu_ref[...], preferred_element_type=F32)
    o_ref[...] = (_silu(g) * u).astype(BF16)


def _swiglu(h, w_up):
    m, k = h.shape
    dff = w_up.shape[1] // 2
    tm, tn = TM_MM, 512
    nj = dff // tn
    return pl.pallas_call(
        _swiglu_kernel,
        grid=(m // tm, nj),
        in_specs=[pl.BlockSpec((tm, k), lambda i, j: (i, 0)),
                  pl.BlockSpec((k, tn), lambda i, j: (0, j)),
                  pl.BlockSpec((k, tn), lambda i, j: (0, nj + j))],
        out_specs=pl.BlockSpec((tm, tn), lambda i, j: (i, j)),
        out_shape=jax.ShapeDtypeStruct((m, dff), BF16),
        compiler_params=_params("parallel", "parallel"),
        name="ffn_up_swiglu",
    )(h, w_up, w_up)


def _layer_weights(l, w_in, rwkv_mu, rwkv_w0, rwkv_w2, rwkv_a0, rwkv_a2, rwkv_g2, rwkv_k_k, rwkv_k_a,
                   rwkv_r_k, rwkv_ln_w, rwkv_ln_b, rwkv_v0, rwkv_v1, rwkv_v2, gdn_conv_w, gdn_a_log,
                   gdn_dt_bias):
    c = MIX
    wl = w_in[l]
    o_rwkv = c
    o_lora = o_rwkv + 3 * c
    n_lora = rwkv_w2.shape[1] + rwkv_a2.shape[1] + rwkv_g2.shape[1]
    o_gdn = o_lora + n_lora
    o_ab = o_gdn + 4 * c
    o_gate = o_ab + 2 * GDN_HEADS
    pad_cols = N_SMALL - n_lora - 2 * GDN_HEADS
    w_main = jnp.concatenate([wl[:, :o_lora], wl[:, o_gdn:o_ab]], axis=1).astype(BF16)
    assert n_lora - (N_SMALL - LANES) == GDN_G_LANE and GDN_B_LANE == GDN_G_LANE + GDN_HEADS
    w_small = jnp.concatenate([wl[:, o_lora:o_gdn], wl[:, o_ab:o_gate],
                               jnp.zeros((wl.shape[0], pad_cols), wl.dtype)], axis=1).astype(BF16)
    w_gate = wl[:, o_gate:].astype(BF16)

    mu = rwkv_mu[l]
    row = lambda v: v.reshape(1, -1)
    n_w, n_a, n_g = rwkv_w2.shape[1], rwkv_a2.shape[1], rwkv_g2.shape[1]
    padrows = lambda w, lo: jnp.zeros((N_SMALL, c), F32).at[lo:lo + w.shape[0]].set(w).astype(BF16)
    lane_vec = lambda v, lo: jnp.zeros((1, LANES), F32).at[0, lo:lo + v.shape[0]].set(v)
    rw = {
        "mu_r": row(mu[:c]), "mu_k": row(mu[c:2 * c]), "mu_v": row(mu[2 * c:3 * c]),
        "mu_s": jnp.zeros((1, N_SMALL), F32).at[0, :n_lora].set(mu[3 * c:]),
        "w0": row(rwkv_w0[l]), "a0": row(rwkv_a0[l]), "k_k": row(rwkv_k_k[l]), "k_a": row(rwkv_k_a[l]),
        "w2p": padrows(rwkv_w2[l], 0), "a2p": padrows(rwkv_a2[l], n_w), "g2p": padrows(rwkv_g2[l], n_w + n_a),
        "r_k": row(rwkv_r_k[l]), "ln_w": row(rwkv_ln_w[l]), "ln_b": row(rwkv_ln_b[l]),
    }
    if l > 0:
        nv = rwkv_v1.shape[2]
        rw["v0"] = row(rwkv_v0[l - 1])
        rw["v1p"] = jnp.zeros((c, LANES), F32).at[:, :nv].set(rwkv_v1[l - 1]).astype(BF16)
        rw["v2p"] = jnp.zeros((LANES, c), F32).at[:nv].set(rwkv_v2[l - 1]).astype(BF16)
    cw = gdn_conv_w[l]
    gd = {"cw_q": cw[:, :c], "cw_k": cw[:, c:2 * c], "cw_v": cw[:, 2 * c:],
          "alog": lane_vec(gdn_a_log[l], GDN_G_LANE), "dtb": lane_vec(gdn_dt_bias[l], GDN_G_LANE)}
    return w_main, w_small, w_gate, rw, gd


def kernel(x, c, ada_w, ada_b, w_in, pool_w, pool_scale, rwkv_mu, rwkv_w0, rwkv_w2, rwkv_a0, rwkv_a2, rwkv_g2, rwkv_k_k, rwkv_k_a, rwkv_r_k, rwkv_ln_w, rwkv_ln_b, rwkv_v0, rwkv_v1, rwkv_v2, gdn_conv_w, gdn_a_log, gdn_dt_bias, gdn_norm_w, w_branch_a, w_branch_b, w_branch_c, w_out, ln1_w, ln1_b, ffn_w_up, ffn_w_down, ln2_w, ln2_b):
    batch, seq, d = x.shape
    assert seq % TM_MM == 0 and d == 2 * MIX
    m = batch * seq
    xf = x.reshape(m, d)

    mod = _ada(c, ada_w, ada_b)
    mods = [[mod[l, :, i * d:(i + 1) * d].reshape(batch, 1, d) for i in range(6)]
            for l in range(DEPTH)]

    h = _modulate(xf, mods[0][1], mods[0][0], seq)
    v_first = None
    for l in range(DEPTH):
        sh_m, sc_m, gt_m, sh_f, sc_f, gt_f = mods[l]
        w_main, w_small, w_gate, rw, gd = _layer_weights(
            l, w_in, rwkv_mu, rwkv_w0, rwkv_w2, rwkv_a0, rwkv_a2, rwkv_g2, rwkv_k_k, rwkv_k_a,
            rwkv_r_k, rwkv_ln_w, rwkv_ln_b, rwkv_v0, rwkv_v1, rwkv_v2, gdn_conv_w, gdn_a_log,
            gdn_dt_bias)

        p = _matmul(h, w_main, tn=1024, out_dtype=F32, name="in_proj_main")
        psm = _matmul(h, w_small, tn=N_SMALL, out_dtype=F32, name="in_proj_small")
        gates = _matmul(h, w_gate, tn=1024, out_dtype=BF16, act="sigmoid", name="in_proj_gates")

        y_a = _pool(p, pool_w[l], pool_scale[l], seq)
        prep = _rwkv_prep(p, psm, rw, v_first, seq)
        if l == 0:
            v_first = prep[2]
        y_b = _rwkv_core(prep, rw, batch, seq)
        gq, gk, gv, ggb = _gdn_prep(p, psm, gd, seq)
        y_c = _gdn_core(gq, gk, gv, ggb, p, gdn_norm_w[l], batch, seq)

        merged = _merge(y_a, y_b, y_c, w_branch_a[l].astype(BF16), w_branch_b[l].astype(BF16),
                        w_branch_c[l].astype(BF16), gates)
        xf, h = _proj_ln(merged, w_out[l].astype(BF16), xf, gt_m, ln1_w[l], ln1_b[l],
                         (sc_f, sh_f), seq, tk=1024, name="out_proj_ln")

        act = _swiglu(h, ffn_w_up[l].astype(BF16))
        nxt = (mods[l + 1][1], mods[l + 1][0]) if l + 1 < DEPTH else None
        xf, h = _proj_ln(act, ffn_w_down[l].astype(BF16), xf, gt_f, ln2_w[l], ln2_b[l],
                         nxt, seq, tk=512, name="ffn_down_ln")
    return xf.reshape(batch, seq, d)
```

```python
import functools

import jax
import jax.numpy as jnp
from jax import lax
from jax.experimental import pallas as pl
from jax.experimental.pallas import tpu as pltpu

F32 = jnp.float32
BF16 = jnp.bfloat16

DEPTH = 2
DN_ALPHA = (2 * DEPTH) ** 0.25
LN_EPS = 1e-5
MIX = 1024
POOL_WINDOWS = (2, 4, 8, 16)
POOL_GROUP = 256
RWKV_HEAD = 64
RWKV_LN_EPS = RWKV_HEAD * 1e-5
GDN_HEAD = 128
GDN_HEADS = 8
GDN_EPS = 1e-6
CHUNK = 64
LANES = 128
N_SMALL = 384
GDN_G_LANE = 32
GDN_B_LANE = 40
VMEM_LIMIT = 56 * 1024 * 1024

TM_MM = 1024
TM_LN = 512
TM_PREP = 256
RB_RWKV = 256
RB_GDN = 128
GDN_HEADS_PER_STEP = 4
CORE_GROUP = 16
CORE_STAGGER = 1


def _dot(a, b):
    return jnp.dot(a.astype(BF16), b.astype(BF16), preferred_element_type=F32)


def _dot_nt(a, b):
    return lax.dot_general(a.astype(BF16), b.astype(BF16), (((1,), (1,)), ((), ())),
                           preferred_element_type=F32)


def _split(a, n):
    terms, rest = [], a
    for _ in range(n):
        t = rest.astype(BF16)
        terms.append(t)
        rest = rest - t.astype(F32)
    return terms


def _dot3(a, b, nt=False):
    d = _dot_nt if nt else _dot
    a1, a2 = _split(a, 2)
    b1, b2 = _split(b, 2)
    return d(a1, b1) + (d(a1, b2) + d(a2, b1))


def _dot_sel_l(sel, b, n):
    out = None
    for t in _split(b, n):
        p = jnp.dot(sel, t, preferred_element_type=F32)
        out = p if out is None else out + p
    return out


def _dot_sel_r(a, sel, n):
    out = None
    for t in _split(a, n):
        p = jnp.dot(t, sel, preferred_element_type=F32)
        out = p if out is None else out + p
    return out


def _sigmoid(x):
    return 1.0 / (1.0 + jnp.exp(-x))


def _silu(x):
    return x * _sigmoid(x)


def _softplus(x):
    return jnp.maximum(x, 0.0) + jnp.log(1.0 + jnp.exp(-jnp.abs(x)))


def _iota2(shape, dim):
    return lax.broadcasted_iota(jnp.int32, shape, dim)


def _block_masks(n):
    r = _iota2((n, n), 0)
    c = _iota2((n, n), 1)
    same = lambda s: (r >> s) == (c >> s)
    m = {
        "eye": (r == c).astype(F32),
        "bd8": same(3),
        "off16": same(4) & jnp.logical_not(same(3)),
        "off32": same(5) & jnp.logical_not(same(4)),
        "off64": same(6) & jnp.logical_not(same(5)),
        "bd64": same(6),
    }
    m["strict"] = m["bd64"] & (r > c)
    m["incl"] = m["bd64"] & (r >= c)
    return m


def _chunk_cumsum_matrix(n):
    r = _iota2((n, n), 0)
    c = _iota2((n, n), 1)
    return (((r >> 6) == (c >> 6)) & (c <= r)).astype(BF16)


def _tri_inv(mats, m):
    a8 = [jnp.where(m["bd8"], a, 0.0) for a in mats]
    s = [m["eye"] + x for x in a8]
    p = [_dot(x, x) for x in a8]
    yield
    q = [_dot(pi, jnp.concatenate([si, pi], axis=1)) for pi, si in zip(p, s)]
    yield
    s = [si + qi[:, :LANES] for si, qi in zip(s, q)]
    s = [si + _dot(qi[:, LANES:], si) for si, qi in zip(s, q)]
    yield
    for key in ("off16", "off32", "off64"):
        t = [_dot(jnp.where(m[key], a, 0.0), si) for a, si in zip(mats, s)]
        yield
        s = [si + _dot(si, ti) for si, ti in zip(s, t)]
        yield
    return s


def _interleave(gens, stagger):
    active, pending, tick = [], list(gens), 0
    while active or pending:
        if pending and tick % stagger == 0:
            active.append(pending.pop(0))
        tick += 1
        for g in list(active):
            try:
                next(g)
            except StopIteration:
                active.remove(g)


def _shift_rows(x, halo, s):
    xr = pltpu.roll(x, s, 0)
    hr = pltpu.roll(halo, s, 0)
    rows = _iota2(halo.shape, 0)
    top = jnp.where(rows < s, hr, xr[:8])
    return jnp.concatenate([top, xr[8:]], axis=0)


def _resid_ln(x, y, gt, lnw, lnb):
    z = DN_ALPHA * x + (1.0 + gt) * y
    mu = jnp.mean(z, axis=-1, keepdims=True)
    zc = z - mu
    var = jnp.mean(zc * zc, axis=-1, keepdims=True)
    return zc * lax.rsqrt(var + LN_EPS) * lnw + lnb


def _params(*sem):
    return pltpu.CompilerParams(dimension_semantics=sem, vmem_limit_bytes=VMEM_LIMIT)


def _ada_kernel(c_ref, w_ref, b_ref, o_ref):
    c = c_ref[...]
    o_ref[0] = _dot3(_silu(c), w_ref[0]) + b_ref[0]


def _ada(c, ada_w, ada_b):
    nl, d, n = ada_w.shape
    b = c.shape[0]
    tn = 512
    cp = jnp.zeros((8, d), F32).at[:b].set(c)
    out = pl.pallas_call(
        _ada_kernel,
        grid=(nl, n // tn),
        in_specs=[pl.BlockSpec((8, d), lambda l, j: (0, 0)),
                  pl.BlockSpec((1, d, tn), lambda l, j: (l, 0, j)),
                  pl.BlockSpec((1, 1, tn), lambda l, j: (l, 0, j))],
        out_specs=pl.BlockSpec((1, 8, tn), lambda l, j: (l, 0, j)),
        out_shape=jax.ShapeDtypeStruct((nl, 8, n), F32),
        compiler_params=_params("parallel", "parallel"),
        name="ada_mod",
    )(cp, ada_w, ada_b.reshape(nl, 1, n))
    return out[:, :b]


def _modulate_kernel(x_ref, sc_ref, sh_ref, h_ref):
    h_ref[...] = (x_ref[...] * (1.0 + sc_ref[0]) + sh_ref[0]).astype(BF16)


def _modulate(x, sc, sh, seq):
    m, d = x.shape
    tm = TM_LN
    per = seq // tm
    mod = pl.BlockSpec((1, 1, d), lambda i: (i // per, 0, 0))
    return pl.pallas_call(
        _modulate_kernel,
        grid=(m // tm,),
        in_specs=[pl.BlockSpec((tm, d), lambda i: (i, 0)), mod, mod],
        out_specs=pl.BlockSpec((tm, d), lambda i: (i, 0)),
        out_shape=jax.ShapeDtypeStruct((m, d), BF16),
        compiler_params=_params("parallel"),
        name="modulate",
    )(x, sc, sh)


def _mm_kernel(a_ref, b_ref, o_ref, *, act):
    acc = jnp.dot(a_ref[...], b_ref[...], preferred_element_type=F32)
    if act == "sigmoid":
        acc = _sigmoid(acc)
    o_ref[...] = acc.astype(o_ref.dtype)


def _matmul(a, b, *, tn, out_dtype, act=None, name):
    m, k = a.shape
    n = b.shape[1]
    tm = TM_MM
    return pl.pallas_call(
        functools.partial(_mm_kernel, act=act),
        grid=(m // tm, n // tn),
        in_specs=[pl.BlockSpec((tm, k), lambda i, j: (i, 0)),
                  pl.BlockSpec((k, tn), lambda i, j: (0, j))],
        out_specs=pl.BlockSpec((tm, tn), lambda i, j: (i, j)),
        out_shape=jax.ShapeDtypeStruct((m, n), out_dtype),
        compiler_params=_params("parallel", "parallel"),
        name=name,
    )(a, b)


def _pool_kernel(p_ref, halo_ref, w_ref, scale_ref, o_ref, *, seq, tm):
    t0 = (pl.program_id(0) * tm) % seq
    x = p_ref[...]
    halo = jnp.where(t0 == 0, 0.0, halo_ref[...])
    ext = jnp.concatenate([halo, x], axis=0)
    sums = [ext]
    for sh in (1, 2, 4, 8):
        sums.append(sums[-1] + pltpu.roll(sums[-1], sh, 0))
    t = t0 + _iota2((tm, POOL_GROUP), 0)
    ys = []
    for g, win in enumerate(POOL_WINDOWS):
        cols = slice(g * POOL_GROUP, (g + 1) * POOL_GROUP)
        cnt = jnp.minimum(t + 1, win).astype(F32)
        pooled = sums[g + 1][16:, cols] / cnt - x[:, cols]
        ys.append(_dot(pooled, w_ref[g]))
    o_ref[...] = (jnp.concatenate(ys, axis=1) * scale_ref[...]).astype(BF16)


def _pool(p, pool_w, pool_scale, seq):
    m = p.shape[0]
    tm = TM_LN
    return pl.pallas_call(
        functools.partial(_pool_kernel, seq=seq, tm=tm),
        grid=(m // tm,),
        in_specs=[pl.BlockSpec((tm, MIX), lambda i: (i, 0)),
                  pl.BlockSpec((16, MIX), lambda i: (jnp.maximum(i * (tm // 16) - 1, 0), 0)),
                  pl.BlockSpec(pool_w.shape, lambda i: (0, 0, 0)),
                  pl.BlockSpec((1, MIX), lambda i: (0, 0))],
        out_specs=pl.BlockSpec((tm, MIX), lambda i: (i, 0)),
        out_shape=jax.ShapeDtypeStruct((m, MIX), BF16),
        compiler_params=_params("parallel"),
        name="pool_mixer",
    )(p, p, pool_w.astype(BF16), pool_scale.reshape(1, MIX))


def _rwkv_prep_kernel(*refs, seq, tm, has_vres):
    (pr, pk, pv, hr, hk, hv, ps, hs, mur, muk, muv, mus, w0, a0, kkp, kap,
     w2p, a2p, g2p) = refs[:19]
    if has_vres:
        vf, v0, v1p, v2p = refs[19:23]
        outs = refs[23:]
    else:
        outs = refs[19:]
    r_o, k_o, v_o, kk_o, a_o, ld_o, g_o = outs

    first = (pl.program_id(0) * tm) % seq == 0

    def lerp(p_ref, h_ref, mu_ref):
        p = p_ref[...]
        prev = _shift_rows(p, jnp.where(first, 0.0, h_ref[...]), 1)
        return p + (prev - p) * mu_ref[...]

    xr = lerp(pr, hr, mur)
    xk = lerp(pk, hk, muk)
    xv = lerp(pv, hv, muv)
    xs = lerp(ps, hs, mus)

    w = -_softplus(-(w0[...] + _dot(jnp.tanh(xs), w2p[...]))) - 0.5
    a = _sigmoid(a0[...] + _dot(xs, a2p[...]))
    g = _dot(_sigmoid(xs), g2p[...])
    if has_vres:
        gate = _sigmoid(v0[...] + _dot(_dot(xv, v1p[...]), v2p[...]))
        xv = xv + (vf[...] - xv) * gate

    kk = xk * kkp[...]
    r2 = _iota2((LANES, LANES), 0)
    c2 = _iota2((LANES, LANES), 1)
    head_ones = ((r2 >> 6) == (c2 >> 6)).astype(BF16)
    kk2 = kk * kk
    ss = jnp.concatenate(
        [_dot_sel_r(kk2[:, i * LANES:(i + 1) * LANES], head_ones, 2) for i in range(MIX // LANES)],
        axis=1)
    kk = kk / jnp.maximum(jnp.sqrt(ss), 1e-12)

    r_o[...] = xr
    k_o[...] = xk * (1.0 + (a - 1.0) * kap[...])
    v_o[...] = xv
    kk_o[...] = kk
    a_o[...] = a
    ld_o[...] = _dot_sel_l(_chunk_cumsum_matrix(tm), -jnp.exp(w), 3)
    g_o[...] = g


def _rwkv_prep(p, psm, wts, v_first, seq):
    m = p.shape[0]
    tm = TM_PREP
    has_vres = v_first is not None
    tile = lambda cb: pl.BlockSpec((tm, MIX), lambda i, cb=cb: (i, cb))
    halo = lambda cb: pl.BlockSpec((8, MIX), lambda i, cb=cb: (jnp.maximum(i * (tm // 8) - 1, 0), cb))
    vec = pl.BlockSpec((1, MIX), lambda i: (0, 0))
    full = lambda arr: pl.BlockSpec(arr.shape, lambda i: (0,) * arr.ndim)
    in_specs = [tile(1), tile(2), tile(3), halo(1), halo(2), halo(3),
                pl.BlockSpec((tm, N_SMALL), lambda i: (i, 0)),
                pl.BlockSpec((8, N_SMALL), lambda i: (jnp.maximum(i * (tm // 8) - 1, 0), 0)),
                vec, vec, vec, pl.BlockSpec((1, N_SMALL), lambda i: (0, 0)),
                vec, vec, vec, vec,
                full(wts["w2p"]), full(wts["a2p"]), full(wts["g2p"])]
    args = [p, p, p, p, p, p, psm, psm, wts["mu_r"], wts["mu_k"], wts["mu_v"], wts["mu_s"],
            wts["w0"], wts["a0"], wts["k_k"], wts["k_a"], wts["w2p"], wts["a2p"], wts["g2p"]]
    if has_vres:
        in_specs += [pl.BlockSpec((tm, MIX), lambda i: (i, 0)), vec, full(wts["v1p"]), full(wts["v2p"])]
        args += [v_first, wts["v0"], wts["v1p"], wts["v2p"]]
    out = pl.BlockSpec((tm, MIX), lambda i: (i, 0))
    return pl.pallas_call(
        functools.partial(_rwkv_prep_kernel, seq=seq, tm=tm, has_vres=has_vres),
        grid=(m // tm,),
        in_specs=in_specs,
        out_specs=[out] * 7,
        out_shape=[jax.ShapeDtypeStruct((m, MIX), F32)] * 7,
        compiler_params=_params("parallel"),
        name="rwkv_prep",
    )(*args)


def _rwkv_core_kernel(r_ref, k_ref, v_ref, kk_ref, a_ref, gc_ref, g_ref, rk_ref, lnw_ref, lnb_ref,
                      o_ref, h_ref):
    @pl.when(pl.program_id(1) == 0)
    def _():
        h_ref[...] = jnp.zeros_like(h_ref)

    nb, rb, _ = r_ref.shape
    nch = rb // CHUNK
    units = [(b, c) for c in range(nch) for b in range(nb)]
    m = _block_masks(LANES)
    lane = _iota2((1, LANES), 1)
    m0 = (lane < RWKV_HEAD).astype(F32)
    m1 = 1.0 - m0
    r2 = _iota2((LANES, LANES), 0)
    c2 = _iota2((LANES, LANES), 1)
    head_ones = ((r2 >> 6) == (c2 >> 6)).astype(BF16)
    stack = lambda x: jnp.concatenate([x * m0, x * m1], axis=0)
    unstack = lambda x: x[:CHUNK] + x[CHUNK:]
    cat0 = lambda *xs: jnp.concatenate(xs, axis=0)
    cat1 = lambda *xs: jnp.concatenate(xs, axis=1)

    def load(ref, u):
        b, c = u
        return ref[b, c * CHUNK:(c + 1) * CHUNK, :]

    first_row = _iota2((CHUNK, LANES), 0) == 0
    q_t, y0, phi, psi = {}, {}, {}, {}

    def chunk_terms(group):
        a_st, r_st, rhs, v_st, bbar_t, kbar_t, dlast, vs = [], [], [], [], [], [], [], []
        for u in group:
            gc = load(gc_ref, u)
            g_prev = jnp.where(first_row, 0.0, pltpu.roll(gc, 1, 0))
            kc, vc, kkc = load(k_ref, u), load(v_ref, u), load(kk_ref, u)
            bc = kkc * load(a_ref, u)
            glast = gc[CHUNK - 1:CHUNK]
            e_neg = jnp.exp(-gc)
            e_rest = jnp.exp(glast - gc)
            a_st.append(stack(-kkc * jnp.exp(g_prev)))
            r_st.append(stack(load(r_ref, u) * jnp.exp(gc)))
            b_hat, k_hat = bc * e_neg, kc * e_neg
            rhs.append(cat0(b_hat, b_hat, k_hat, k_hat))
            v_st.append(stack(vc))
            bbar_t.append((bc * e_rest).T)
            kbar_t.append((kc * e_rest).T)
            dlast.append(jnp.exp(glast))
            vs.append(vc)
            yield
        big = [_dot_nt(cat0(a, r), x) for a, r, x in zip(a_st, r_st, rhs)]
        yield
        a_ab = [jnp.where(m["strict"], x[:LANES, :LANES], 0.0) for x in big]
        a_kr = [cat0(jnp.where(m["strict"], x[:LANES, LANES:], 0.0),
                     jnp.where(m["incl"], x[LANES:, LANES:], 0.0)) for x in big]
        a_rb = [jnp.where(m["incl"], x[LANES:, :LANES], 0.0) for x in big]
        x1 = [_dot(a, v) for a, v in zip(a_kr, v_st)]
        t_inv = yield from _tri_inv(a_ab, m)
        x2 = [_dot(t, cat1(a, x[:LANES])) for t, a, x in zip(t_inv, a_st, x1)]
        yield
        x3 = [_dot(a, x) for a, x in zip(a_rb, x2)]
        w_t = [unstack(x[:, :LANES]) for x in x2]
        u_t = [unstack(x[:, LANES:]) for x in x2]
        for i, u in enumerate(group):
            phi[u] = m["eye"] * dlast[i] + jnp.where(m["bd64"], _dot(bbar_t[i], w_t[i]), 0.0)
            psi[u] = jnp.where(m["bd64"], _dot(cat1(bbar_t[i], kbar_t[i]), cat0(u_t[i], vs[i])), 0.0)
        yield
        for i, u in enumerate(group):
            q_t[u] = unstack(r_st[i] + x3[i][:, :LANES])
            y0[u] = unstack(x3[i][:, LANES:] + x1[i][LANES:])

    _interleave([chunk_terms(units[i:i + CORE_GROUP]) for i in range(0, len(units), CORE_GROUP)],
                CORE_STAGGER)

    ys = {}
    hs = [h_ref[b] for b in range(nb)]
    for u in units:
        b = u[0]
        qh = _dot3(cat0(q_t[u], phi[u]), hs[b])
        ys[u] = y0[u] + qh[:CHUNK]
        hs[b] = qh[CHUNK:] + psi[u]
    for b in range(nb):
        h_ref[b] = hs[b]

    y = cat0(*[ys[(b, c)] for b in range(nb) for c in range(nch)])
    flat = lambda ref: ref[...].reshape(nb * rb, LANES)
    rr, kk_, vv = flat(r_ref), flat(k_ref), flat(v_ref)
    sums = _dot(cat0(y, rr * kk_ * rk_ref[...]), head_ones)
    yc = y - sums[:nb * rb] * (1.0 / RWKV_HEAD)
    var = _dot(yc * yc, head_ones) * (1.0 / RWKV_HEAD)
    yn = yc * lax.rsqrt(var + RWKV_LN_EPS) * lnw_ref[...] + lnb_ref[...]
    bonus = sums[nb * rb:] * vv
    o_ref[...] = ((yn + bonus) * flat(g_ref)).astype(BF16).reshape(nb, rb, LANES)


def _rwkv_core(prep, wts, batch, seq):
    rb = RB_RWKV
    tile = pl.BlockSpec((batch, rb, LANES), lambda p, j: (0, j, p))
    vec = pl.BlockSpec((1, LANES), lambda p, j: (0, p))
    out = pl.pallas_call(
        _rwkv_core_kernel,
        grid=(MIX // LANES, seq // rb),
        in_specs=[tile] * 7 + [vec] * 3,
        out_specs=tile,
        out_shape=jax.ShapeDtypeStruct((batch, seq, MIX), BF16),
        scratch_shapes=[pltpu.VMEM((batch, LANES, LANES), F32)],
        compiler_params=_params("parallel", "arbitrary"),
        name="rwkv_core",
    )(*[a.reshape(batch, seq, MIX) for a in prep], wts["r_k"], wts["ln_w"], wts["ln_b"])
    return out.reshape(batch * seq, MIX)


def _gdn_prep_kernel(pq, pk, pv, hq, hk, hv, ps, cwq, cwk, cwv, alog, dtb,
                     q_o, k_o, v_o, gb_o, *, seq, tm):
    first = (pl.program_id(0) * tm) % seq == 0
    ones = jnp.ones((LANES, LANES), BF16)

    def conv_silu(p_ref, h_ref, w_ref):
        x = p_ref[...]
        halo = jnp.where(first, 0.0, h_ref[...])
        w = w_ref[...]
        y = x * w[3:4]
        for s in (1, 2, 3):
            y = y + _shift_rows(x, halo, s) * w[3 - s:4 - s]
        return _silu(y)

    def l2norm(y, scale):
        blocks = []
        for i in range(GDN_HEADS):
            blk = y[:, i * LANES:(i + 1) * LANES]
            ss = _dot_sel_r(blk * blk, ones, 2)
            blocks.append(blk * (lax.rsqrt(ss + GDN_EPS) * scale))
        return jnp.concatenate(blocks, axis=1)

    q_o[...] = l2norm(conv_silu(pq, hq, cwq), GDN_HEAD ** -0.5)
    k_o[...] = l2norm(conv_silu(pk, hk, cwk), 1.0)
    v_o[...] = conv_silu(pv, hv, cwv)

    raw = ps[:, N_SMALL - LANES:]
    gval = -jnp.exp(alog[...]) * _softplus(raw + dtb[...])
    gcum = _dot_sel_l(_chunk_cumsum_matrix(tm), gval, 3)
    lane = _iota2(raw.shape, 1)
    is_g = (lane >= GDN_G_LANE) & (lane < GDN_G_LANE + GDN_HEADS)
    is_b = (lane >= GDN_B_LANE) & (lane < GDN_B_LANE + GDN_HEADS)
    gb_o[...] = jnp.where(is_g, gcum, jnp.where(is_b, _sigmoid(raw), 0.0))


def _gdn_prep(p, psm, wts, seq):
    m = p.shape[0]
    tm = TM_PREP
    tile = lambda cb: pl.BlockSpec((tm, MIX), lambda i, cb=cb: (i, cb))
    halo = lambda cb: pl.BlockSpec((8, MIX), lambda i, cb=cb: (jnp.maximum(i * (tm // 8) - 1, 0), cb))
    cw = pl.BlockSpec((4, MIX), lambda i: (0, 0))
    lv = pl.BlockSpec((1, LANES), lambda i: (0, 0))
    out = pl.BlockSpec((tm, MIX), lambda i: (i, 0))
    return pl.pallas_call(
        functools.partial(_gdn_prep_kernel, seq=seq, tm=tm),
        grid=(m // tm,),
        in_specs=[tile(4), tile(5), tile(6), halo(4), halo(5), halo(6),
                  pl.BlockSpec((tm, N_SMALL), lambda i: (i, 0)), cw, cw, cw, lv, lv],
        out_specs=[out, out, out, pl.BlockSpec((tm, LANES), lambda i: (i, 0))],
        out_shape=[jax.ShapeDtypeStruct((m, MIX), F32)] * 3 + [jax.ShapeDtypeStruct((m, LANES), F32)],
        compiler_params=_params("parallel"),
        name="gdn_prep",
    )(p, p, p, p, p, p, psm, wts["cw_q"], wts["cw_k"], wts["cw_v"], wts["alog"], wts["dtb"])


def _gdn_core_kernel(q_ref, k_ref, v_ref, gb_ref, z_ref, nw_ref, o_ref, s_ref):
    @pl.when(pl.program_id(1) == 0)
    def _():
        s_ref[...] = jnp.zeros_like(s_ref)

    nb, rb, width = q_ref.shape
    nh = width // LANES
    units = [(b, hh) for hh in range(nh) for b in range(nb)]
    m = _block_masks(LANES)
    ones = jnp.ones((LANES, LANES), BF16)
    cat0 = lambda *xs: jnp.concatenate(xs, axis=0)
    cat1 = lambda *xs: jnp.concatenate(xs, axis=1)

    def lane_bcast(x, lane):
        return jnp.broadcast_to(pltpu.roll(x, LANES - lane, 1)[:, 0:1], x.shape)

    nch = rb // CHUNK
    q_t, y0, phi, psi = {}, {}, {}, {}

    def tile_terms(group):
        gf, ks, lhs, rhs, e_g, qs, gamma = [], [], [], [], [], [], []
        for b, hh in group:
            hl = slice(hh * LANES, (hh + 1) * LANES)
            head = nh * pl.program_id(0) + hh
            gbv = gb_ref[b]
            g = lane_bcast(gbv, GDN_G_LANE + head)
            beta = lane_bcast(gbv, GDN_B_LANE + head)
            q, k, v = q_ref[b, :, hl], k_ref[b, :, hl], v_ref[b, :, hl]
            diff = g - g.T
            gamma.append(jnp.where(m["incl"], jnp.exp(jnp.where(m["incl"], diff, 0.0)), 0.0))
            eg = jnp.exp(g)
            kb = k * beta
            gf.append(g)
            ks.append(k)
            qs.append(q)
            e_g.append(eg)
            lhs.append(cat0(kb, q))
            rhs.append(cat1(v * beta, kb * eg))
            yield
        kq = [_dot_nt(x, k) for x, k in zip(lhs, ks)]
        yield
        a_neg = [-jnp.where(m["strict"], x[:LANES] * gm, 0.0) for x, gm in zip(kq, gamma)]
        a_qk = [x[LANES:] * gm for x, gm in zip(kq, gamma)]
        k_bar_t = [[(k[c * CHUNK:(c + 1) * CHUNK]
                     * jnp.exp(g[(c + 1) * CHUNK - 1:(c + 1) * CHUNK] - g[c * CHUNK:(c + 1) * CHUNK])).T
                    for c in range(nch)] for k, g in zip(ks, gf)]
        t_inv = yield from _tri_inv(a_neg, m)
        uw = [_dot(t, x) for t, x in zip(t_inv, rhs)]
        yield
        x3 = [_dot(a, x) for a, x in zip(a_qk, uw)]
        for i, u in enumerate(group):
            for c in range(nch):
                glast = gf[i][(c + 1) * CHUNK - 1:(c + 1) * CHUNK]
                pp = _dot(k_bar_t[i][c], uw[i][c * CHUNK:(c + 1) * CHUNK])
                phi[(u, c)] = m["eye"] * jnp.exp(glast) - pp[:, LANES:]
                psi[(u, c)] = pp[:, :LANES]
        yield
        for i, u in enumerate(group):
            y0[u] = x3[i][:, :LANES]
            q_t[u] = qs[i] * e_g[i] - x3[i][:, LANES:]

    _interleave([tile_terms(units[i:i + CORE_GROUP]) for i in range(0, len(units), CORE_GROUP)],
                CORE_STAGGER)

    ys = {u: [] for u in units}
    st = {(b, hh): s_ref[hh, b] for b, hh in units}
    for c in range(nch):
        cs = slice(c * CHUNK, (c + 1) * CHUNK)
        for u in units:
            qs_ = _dot3(cat0(q_t[u][cs], phi[(u, c)]), st[u])
            ys[u].append(y0[u][cs] + qs_[:CHUNK])
            st[u] = qs_[CHUNK:] + psi[(u, c)]
    for b, hh in units:
        s_ref[hh, b] = st[(b, hh)]

    for hh in range(nh):
        hl = slice(hh * LANES, (hh + 1) * LANES)
        y = cat0(*[yc for b in range(nb) for yc in ys[(b, hh)]])
        ms = _dot(y * y, ones) * (1.0 / GDN_HEAD)
        out = y * lax.rsqrt(ms + GDN_EPS) * nw_ref[...]
        z = z_ref[:, :, hl].reshape(nb * rb, LANES)
        o_ref[:, :, hl] = (out * _silu(z)).astype(BF16).reshape(nb, rb, LANES)


def _gdn_core(q, k, v, gb, p, norm_w, batch, seq):
    rb, nh = RB_GDN, GDN_HEADS_PER_STEP
    width = nh * LANES
    tile = pl.BlockSpec((batch, rb, width), lambda h, j: (0, j, h))
    z_col0 = 7 * (MIX // width)
    r3 = lambda a: a.reshape(batch, seq, a.shape[-1])
    out = pl.pallas_call(
        _gdn_core_kernel,
        grid=(GDN_HEADS // nh, seq // rb),
        in_specs=[tile, tile, tile,
                  pl.BlockSpec((batch, rb, LANES), lambda h, j: (0, j, 0)),
                  pl.BlockSpec((batch, rb, width), lambda h, j: (0, j, z_col0 + h)),
                  pl.BlockSpec((1, LANES), lambda h, j: (0, 0))],
        out_specs=tile,
        out_shape=jax.ShapeDtypeStruct((batch, seq, MIX), BF16),
        scratch_shapes=[pltpu.VMEM((nh, batch, LANES, LANES), F32)],
        compiler_params=_params("parallel", "arbitrary"),
        name="gdn_core",
    )(r3(q), r3(k), r3(v), r3(gb), r3(p), norm_w.reshape(1, LANES))
    return out.reshape(batch * seq, MIX)


def _merge_kernel(ya, yb, yc, wa, wb, wc, ga, gb, gc, o_ref):
    dot = lambda y, w: jnp.dot(y[...], w[...], preferred_element_type=F32)
    acc = ga[...].astype(F32) * dot(ya, wa)
    acc = acc + gb[...].astype(F32) * dot(yb, wb)
    acc = acc + gc[...].astype(F32) * dot(yc, wc)
    o_ref[...] = acc.astype(BF16)


def _merge(ya, yb, yc, wa, wb, wc, gates):
    m = ya.shape[0]
    d = wa.shape[1]
    tm, tn = TM_LN, 1024
    nj = d // tn
    y = pl.BlockSpec((tm, MIX), lambda i, j: (i, 0))
    w = pl.BlockSpec((MIX, tn), lambda i, j: (0, j))
    g = lambda br: pl.BlockSpec((tm, tn), lambda i, j, br=br: (i, br * nj + j))
    return pl.pallas_call(
        _merge_kernel,
        grid=(m // tm, nj),
        in_specs=[y, y, y, w, w, w, g(0), g(1), g(2)],
        out_specs=pl.BlockSpec((tm, tn), lambda i, j: (i, j)),
        out_shape=jax.ShapeDtypeStruct((m, d), BF16),
        compiler_params=_params("parallel", "parallel"),
        name="merge",
    )(ya, yb, yc, wa, wb, wc, gates, gates, gates)


def _proj_ln_kernel(*refs, emit_h, sub):
    a_ref, w_ref, x_ref, gt_ref, lnw_ref, lnb_ref = refs[:6]
    if emit_h:
        sc_ref, sh_ref, xo_ref, ho_ref = refs[6:]
    else:
        (xo_ref,) = refs[6:]
    nsub = a_ref.shape[0] // sub
    w = w_ref[...]
    dot = lambda r: jnp.dot(a_ref[r * sub:(r + 1) * sub, :], w, preferred_element_type=F32)
    y_next = dot(0)
    for r in range(nsub):
        y = y_next
        if r + 1 < nsub:
            y_next = dot(r + 1)
        rows = slice(r * sub, (r + 1) * sub)
        xn = _resid_ln(x_ref[rows, :], y, gt_ref[0], lnw_ref[...], lnb_ref[...])
        xo_ref[rows, :] = xn
        if emit_h:
            ho_ref[rows, :] = (xn * (1.0 + sc_ref[0]) + sh_ref[0]).astype(BF16)


def _proj_ln(a, w, x, gt, lnw, lnb, nxt, seq, *, tm, sub, name):
    m, k = a.shape
    d = w.shape[1]
    per = seq // tm
    emit_h = nxt is not None
    row = pl.BlockSpec((tm, d), lambda i: (i, 0))
    mod = pl.BlockSpec((1, 1, d), lambda i: (i // per, 0, 0))
    vec = pl.BlockSpec((1, d), lambda i: (0, 0))
    in_specs = [pl.BlockSpec((tm, k), lambda i: (i, 0)),
                pl.BlockSpec((k, d), lambda i: (0, 0), pipeline_mode=pl.Buffered(1)),
                row, mod, vec, vec]
    args = [a, w, x, gt, lnw.reshape(1, d), lnb.reshape(1, d)]
    out_specs, out_shape = [row], [jax.ShapeDtypeStruct((m, d), F32)]
    if emit_h:
        in_specs += [mod, mod]
        args += list(nxt)
        out_specs.append(row)
        out_shape.append(jax.ShapeDtypeStruct((m, d), BF16))
    res = pl.pallas_call(
        functools.partial(_proj_ln_kernel, emit_h=emit_h, sub=sub),
        grid=(m // tm,),
        in_specs=in_specs,
        out_specs=out_specs,
        out_shape=out_shape,
        compiler_params=_params("parallel"),
        name=name,
    )(*args)
    return (res[0], res[1]) if emit_h else (res[0], None)


def _swiglu_kernel(a_ref, wg_ref, wu_ref, o_ref):
    a = a_ref[...]
    g = jnp.dot(a, wg_ref[...], preferred_element_type=F32)
    u = jnp.dot(a, wu_ref[...], preferred_element_type=F32)
    o_ref[...] = (_silu(g) * u).astype(BF16)


def _swiglu(h, w_up):
    m, k = h.shape
    dff = w_up.shape[1] // 2
    tm, tn = TM_MM, 512
    nj = dff // tn
    return pl.pallas_call(
        _swiglu_kernel,
        grid=(m // tm, nj),
        in_specs=[pl.BlockSpec((tm, k), lambda i, j: (i, 0)),
                  pl.BlockSpec((k, tn), lambda i, j: (0, j)),
                  pl.BlockSpec((k, tn), lambda i, j: (0, nj + j))],
        out_specs=pl.BlockSpec((tm, tn), lambda i, j: (i, j)),
        out_shape=jax.ShapeDtypeStruct((m, dff), BF16),
        compiler_params=_params("parallel", "parallel"),
        name="ffn_up_swiglu",
    )(h, w_up, w_up)


def _layer_weights(l, w_in, rwkv_mu, rwkv_w0, rwkv_w2, rwkv_a0, rwkv_a2, rwkv_g2, rwkv_k_k, rwkv_k_a,
                   rwkv_r_k, rwkv_ln_w, rwkv_ln_b, rwkv_v0, rwkv_v1, rwkv_v2, gdn_conv_w, gdn_a_log,
                   gdn_dt_bias):
    c = MIX
    wl = w_in[l]
    o_rwkv = c
    o_lora = o_rwkv + 3 * c
    n_lora = rwkv_w2.shape[1] + rwkv_a2.shape[1] + rwkv_g2.shape[1]
    o_gdn = o_lora + n_lora
    o_ab = o_gdn + 4 * c
    o_gate = o_ab + 2 * GDN_HEADS
    pad_cols = N_SMALL - n_lora - 2 * GDN_HEADS
    w_main = jnp.concatenate([wl[:, :o_lora], wl[:, o_gdn:o_ab]], axis=1).astype(BF16)
    assert n_lora - (N_SMALL - LANES) == GDN_G_LANE and GDN_B_LANE == GDN_G_LANE + GDN_HEADS
    w_small = jnp.concatenate([wl[:, o_lora:o_gdn], wl[:, o_ab:o_gate],
                               jnp.zeros((wl.shape[0], pad_cols), wl.dtype)], axis=1).astype(BF16)
    w_gate = wl[:, o_gate:].astype(BF16)

    mu = rwkv_mu[l]
    row = lambda v: v.reshape(1, -1)
    n_w, n_a, n_g = rwkv_w2.shape[1], rwkv_a2.shape[1], rwkv_g2.shape[1]
    padrows = lambda w, lo: jnp.zeros((N_SMALL, c), F32).at[lo:lo + w.shape[0]].set(w).astype(BF16)
    lane_vec = lambda v, lo: jnp.zeros((1, LANES), F32).at[0, lo:lo + v.shape[0]].set(v)
    rw = {
        "mu_r": row(mu[:c]), "mu_k": row(mu[c:2 * c]), "mu_v": row(mu[2 * c:3 * c]),
        "mu_s": jnp.zeros((1, N_SMALL), F32).at[0, :n_lora].set(mu[3 * c:]),
        "w0": row(rwkv_w0[l]), "a0": row(rwkv_a0[l]), "k_k": row(rwkv_k_k[l]), "k_a": row(rwkv_k_a[l]),
        "w2p": padrows(rwkv_w2[l], 0), "a2p": padrows(rwkv_a2[l], n_w), "g2p": padrows(rwkv_g2[l], n_w + n_a),
        "r_k": row(rwkv_r_k[l]), "ln_w": row(rwkv_ln_w[l]), "ln_b": row(rwkv_ln_b[l]),
    }
    if l > 0:
        nv = rwkv_v1.shape[2]
        rw["v0"] = row(rwkv_v0[l - 1])
        rw["v1p"] = jnp.zeros((c, LANES), F32).at[:, :nv].set(rwkv_v1[l - 1]).astype(BF16)
        rw["v2p"] = jnp.zeros((LANES, c), F32).at[:nv].set(rwkv_v2[l - 1]).astype(BF16)
    cw = gdn_conv_w[l]
    gd = {"cw_q": cw[:, :c], "cw_k": cw[:, c:2 * c], "cw_v": cw[:, 2 * c:],
          "alog": lane_vec(gdn_a_log[l], GDN_G_LANE), "dtb": lane_vec(gdn_dt_bias[l], GDN_G_LANE)}
    return w_main, w_small, w_gate, rw, gd


def kernel(x, c, ada_w, ada_b, w_in, pool_w, pool_scale, rwkv_mu, rwkv_w0, rwkv_w2, rwkv_a0, rwkv_a2, rwkv_g2, rwkv_k_k, rwkv_k_a, rwkv_r_k, rwkv_ln_w, rwkv_ln_b, rwkv_v0, rwkv_v1, rwkv_v2, gdn_conv_w, gdn_a_log, gdn_dt_bias, gdn_norm_w, w_branch_a, w_branch_b, w_branch_c, w_out, ln1_w, ln1_b, ffn_w_up, ffn_w_down, ln2_w, ln2_b):
    batch, seq, d = x.shape
    assert seq % TM_MM == 0 and d == 2 * MIX
    m = batch * seq
    xf = x.reshape(m, d)

    mod = _ada(c, ada_w, ada_b)
    mods = [[mod[l, :, i * d:(i + 1) * d].reshape(batch, 1, d) for i in range(6)]
            for l in range(DEPTH)]

    h = _modulate(xf, mods[0][1], mods[0][0], seq)
    v_first = None
    for l in range(DEPTH):
        sh_m, sc_m, gt_m, sh_f, sc_f, gt_f = mods[l]
        w_main, w_small, w_gate, rw, gd = _layer_weights(
            l, w_in, rwkv_mu, rwkv_w0, rwkv_w2, rwkv_a0, rwkv_a2, rwkv_g2, rwkv_k_k, rwkv_k_a,
            rwkv_r_k, rwkv_ln_w, rwkv_ln_b, rwkv_v0, rwkv_v1, rwkv_v2, gdn_conv_w, gdn_a_log,
            gdn_dt_bias)

        p = _matmul(h, w_main, tn=1024, out_dtype=F32, name="in_proj_main")
        psm = _matmul(h, w_small, tn=N_SMALL, out_dtype=F32, name="in_proj_small")
        gates = _matmul(h, w_gate, tn=1024, out_dtype=BF16, act="sigmoid", name="in_proj_gates")

        y_a = _pool(p, pool_w[l], pool_scale[l], seq)
        prep = _rwkv_prep(p, psm, rw, v_first, seq)
        if l == 0:
            v_first = prep[2]
        y_b = _rwkv_core(prep, rw, batch, seq)
        gq, gk, gv, ggb = _gdn_prep(p, psm, gd, seq)
        y_c = _gdn_core(gq, gk, gv, ggb, p, gdn_norm_w[l], batch, seq)

        merged = _merge(y_a, y_b, y_c, w_branch_a[l].astype(BF16), w_branch_b[l].astype(BF16),
                        w_branch_c[l].astype(BF16), gates)
        xf, h = _proj_ln(merged, w_out[l].astype(BF16), xf, gt_m, ln1_w[l], ln1_b[l],
                         (sc_f, sh_f), seq, tm=512, sub=256, name="out_proj_ln")

        act = _swiglu(h, ffn_w_up[l].astype(BF16))
        nxt = (mods[l + 1][1], mods[l + 1][0]) if l + 1 < DEPTH else None
        xf, h = _proj_ln(act, ffn_w_down[l].astype(BF16), xf, gt_f, ln2_w[l], ln2_b[l],
                         nxt, seq, tm=256, sub=256, name="ffn_down_ln")
    return xf.reshape(batch, seq, d)
```

```python
import functools

import jax
import jax.numpy as jnp
from jax import lax
from jax.experimental import pallas as pl
from jax.experimental.pallas import tpu as pltpu

F32 = jnp.float32
BF16 = jnp.bfloat16

DEPTH = 2
DN_ALPHA = (2 * DEPTH) ** 0.25
LN_EPS = 1e-5
MIX = 1024
POOL_WINDOWS = (2, 4, 8, 16)
POOL_GROUP = 256
RWKV_HEAD = 64
RWKV_LN_EPS = RWKV_HEAD * 1e-5
GDN_HEAD = 128
GDN_HEADS = 8
GDN_EPS = 1e-6
CHUNK = 64
LANES = 128
N_SMALL = 384
GDN_G_LANE = 32
GDN_B_LANE = 40
VMEM_LIMIT = 56 * 1024 * 1024

TM_MM = 1024
TM_LN = 512
TM_PREP = 256
RB_RWKV = 256
RB_GDN = 128
GDN_HEADS_PER_STEP = 4
CORE_GROUP = 16
CORE_STAGGER = 1


def _dot(a, b):
    return jnp.dot(a.astype(BF16), b.astype(BF16), preferred_element_type=F32)


def _dot_nt(a, b):
    return lax.dot_general(a.astype(BF16), b.astype(BF16), (((1,), (1,)), ((), ())),
                           preferred_element_type=F32)


def _split(a, n):
    terms, rest = [], a
    for _ in range(n):
        t = rest.astype(BF16)
        terms.append(t)
        rest = rest - t.astype(F32)
    return terms


def _dot3(a, b, nt=False):
    d = _dot_nt if nt else _dot
    a1, a2 = _split(a, 2)
    b1, b2 = _split(b, 2)
    return d(a1, b1) + (d(a1, b2) + d(a2, b1))


def _dot_sel_l(sel, b, n):
    out = None
    for t in _split(b, n):
        p = jnp.dot(sel, t, preferred_element_type=F32)
        out = p if out is None else out + p
    return out


def _sigmoid(x):
    return 0.5 * jnp.tanh(0.5 * x) + 0.5


def _silu(x):
    return x * _sigmoid(x)


def _softplus(x):
    return jnp.maximum(x, 0.0) + jnp.log(1.0 + jnp.exp(-jnp.abs(x)))


def _iota2(shape, dim):
    return lax.broadcasted_iota(jnp.int32, shape, dim)


def _block_masks(n):
    r = _iota2((n, n), 0)
    c = _iota2((n, n), 1)
    same = lambda s: (r >> s) == (c >> s)
    m = {
        "eye": (r == c).astype(F32),
        "bd8": same(3),
        "off16": same(4) & jnp.logical_not(same(3)),
        "off32": same(5) & jnp.logical_not(same(4)),
        "off64": same(6) & jnp.logical_not(same(5)),
        "bd64": same(6),
    }
    m["strict"] = m["bd64"] & (r > c)
    m["incl"] = m["bd64"] & (r >= c)
    return m


def _chunk_cumsum_matrix(n):
    r = _iota2((n, n), 0)
    c = _iota2((n, n), 1)
    return (((r >> 6) == (c >> 6)) & (c <= r)).astype(BF16)


def _tri_inv(mats, m):
    a8 = [jnp.where(m["bd8"], a, 0.0) for a in mats]
    s = [m["eye"] + x for x in a8]
    p = [_dot(x, x) for x in a8]
    yield
    q = [_dot(pi, jnp.concatenate([si, pi], axis=1)) for pi, si in zip(p, s)]
    yield
    s = [si + qi[:, :LANES] for si, qi in zip(s, q)]
    s = [si + _dot(qi[:, LANES:], si) for si, qi in zip(s, q)]
    yield
    for key in ("off16", "off32", "off64"):
        t = [_dot(jnp.where(m[key], a, 0.0), si) for a, si in zip(mats, s)]
        yield
        s = [si + _dot(si, ti) for si, ti in zip(s, t)]
        yield
    return s


def _interleave(gens, stagger):
    active, pending, tick = [], list(gens), 0
    while active or pending:
        if pending and tick % stagger == 0:
            active.append(pending.pop(0))
        tick += 1
        for g in list(active):
            try:
                next(g)
            except StopIteration:
                active.remove(g)


def _shift_rows(x, halo, s):
    xr = pltpu.roll(x, s, 0)
    hr = pltpu.roll(halo, s, 0)
    rows = _iota2(halo.shape, 0)
    top = jnp.where(rows < s, hr, xr[:8])
    return jnp.concatenate([top, xr[8:]], axis=0)


def _resid_ln(x, y, gt, lnw, lnb):
    z = DN_ALPHA * x + (1.0 + gt) * y
    mu = jnp.mean(z, axis=-1, keepdims=True)
    zc = z - mu
    var = jnp.mean(zc * zc, axis=-1, keepdims=True)
    return zc * lax.rsqrt(var + LN_EPS) * lnw + lnb


def _params(*sem):
    return pltpu.CompilerParams(dimension_semantics=sem, vmem_limit_bytes=VMEM_LIMIT)


def _ada_kernel(c_ref, w_ref, b_ref, o_ref):
    c = c_ref[...]
    o_ref[0] = _dot3(_silu(c), w_ref[0]) + b_ref[0]


def _ada(c, ada_w, ada_b):
    nl, d, n = ada_w.shape
    b = c.shape[0]
    tn = 512
    cp = jnp.zeros((8, d), F32).at[:b].set(c)
    out = pl.pallas_call(
        _ada_kernel,
        grid=(nl, n // tn),
        in_specs=[pl.BlockSpec((8, d), lambda l, j: (0, 0)),
                  pl.BlockSpec((1, d, tn), lambda l, j: (l, 0, j)),
                  pl.BlockSpec((1, 1, tn), lambda l, j: (l, 0, j))],
        out_specs=pl.BlockSpec((1, 8, tn), lambda l, j: (l, 0, j)),
        out_shape=jax.ShapeDtypeStruct((nl, 8, n), F32),
        compiler_params=_params("parallel", "parallel"),
        name="ada_mod",
    )(cp, ada_w, ada_b.reshape(nl, 1, n))
    return out[:, :b]


def _modulate_kernel(x_ref, sc_ref, sh_ref, h_ref):
    h_ref[...] = (x_ref[...] * (1.0 + sc_ref[0]) + sh_ref[0]).astype(BF16)


def _modulate(x, sc, sh, seq):
    m, d = x.shape
    tm = TM_LN
    per = seq // tm
    mod = pl.BlockSpec((1, 1, d), lambda i: (i // per, 0, 0))
    return pl.pallas_call(
        _modulate_kernel,
        grid=(m // tm,),
        in_specs=[pl.BlockSpec((tm, d), lambda i: (i, 0)), mod, mod],
        out_specs=pl.BlockSpec((tm, d), lambda i: (i, 0)),
        out_shape=jax.ShapeDtypeStruct((m, d), BF16),
        compiler_params=_params("parallel"),
        name="modulate",
    )(x, sc, sh)


def _mm_kernel(a_ref, b_ref, o_ref, *, act):
    acc = jnp.dot(a_ref[...], b_ref[...], preferred_element_type=F32)
    if act == "sigmoid":
        acc = _sigmoid(acc)
    o_ref[...] = acc.astype(o_ref.dtype)


def _matmul(a, b, *, tn, out_dtype, act=None, name):
    m, k = a.shape
    n = b.shape[1]
    tm = TM_MM
    return pl.pallas_call(
        functools.partial(_mm_kernel, act=act),
        grid=(m // tm, n // tn),
        in_specs=[pl.BlockSpec((tm, k), lambda i, j: (i, 0)),
                  pl.BlockSpec((k, tn), lambda i, j: (0, j))],
        out_specs=pl.BlockSpec((tm, tn), lambda i, j: (i, j)),
        out_shape=jax.ShapeDtypeStruct((m, n), out_dtype),
        compiler_params=_params("parallel", "parallel"),
        name=name,
    )(a, b)


def _pool_kernel(p_ref, halo_ref, w_ref, scale_ref, o_ref, *, seq, tm):
    t0 = (pl.program_id(0) * tm) % seq
    x = p_ref[...]
    halo = jnp.where(t0 == 0, 0.0, halo_ref[...])
    ext = jnp.concatenate([halo, x], axis=0)
    sums = [ext]
    for sh in (1, 2, 4, 8):
        sums.append(sums[-1] + pltpu.roll(sums[-1], sh, 0))
    t = t0 + _iota2((tm, POOL_GROUP), 0)
    ys = []
    for g, win in enumerate(POOL_WINDOWS):
        cols = slice(g * POOL_GROUP, (g + 1) * POOL_GROUP)
        cnt = jnp.minimum(t + 1, win).astype(F32)
        pooled = sums[g + 1][16:, cols] / cnt - x[:, cols]
        ys.append(_dot(pooled, w_ref[g]))
    o_ref[...] = (jnp.concatenate(ys, axis=1) * scale_ref[...]).astype(BF16)


def _pool(p, pool_w, pool_scale, seq):
    m = p.shape[0]
    tm = TM_LN
    return pl.pallas_call(
        functools.partial(_pool_kernel, seq=seq, tm=tm),
        grid=(m // tm,),
        in_specs=[pl.BlockSpec((tm, MIX), lambda i: (i, 0)),
                  pl.BlockSpec((16, MIX), lambda i: (jnp.maximum(i * (tm // 16) - 1, 0), 0)),
                  pl.BlockSpec(pool_w.shape, lambda i: (0, 0, 0)),
                  pl.BlockSpec((1, MIX), lambda i: (0, 0))],
        out_specs=pl.BlockSpec((tm, MIX), lambda i: (i, 0)),
        out_shape=jax.ShapeDtypeStruct((m, MIX), BF16),
        compiler_params=_params("parallel"),
        name="pool_mixer",
    )(p, p, pool_w.astype(BF16), pool_scale.reshape(1, MIX))


def _rwkv_prep_kernel(*refs, seq, tm, has_vres):
    (pr, pk, pv, hr, hk, hv, ps, hs, mur, muk, muv, mus, w0, a0, kkp, kap,
     w2p, a2p, g2p) = refs[:19]
    if has_vres:
        vf, v0, v1p, v2p = refs[19:23]
        outs = refs[23:]
    else:
        outs = refs[19:]
    r_o, k_o, v_o, kk_o, a_o, ld_o, g_o = outs

    first = (pl.program_id(0) * tm) % seq == 0

    def lerp(p_ref, h_ref, mu_ref):
        p = p_ref[...]
        prev = _shift_rows(p, jnp.where(first, 0.0, h_ref[...]), 1)
        return p + (prev - p) * mu_ref[...]

    xr = lerp(pr, hr, mur)
    xk = lerp(pk, hk, muk)
    xv = lerp(pv, hv, muv)
    xs = lerp(ps, hs, mus)

    z = w0[...] + _dot(jnp.tanh(xs), w2p[...])
    w = jnp.minimum(z, 0.0) - jnp.log(1.0 + jnp.exp(-jnp.abs(z))) - 0.5
    a = _sigmoid(a0[...] + _dot(xs, a2p[...]))
    g = _dot(_sigmoid(xs), g2p[...])
    if has_vres:
        gate = _sigmoid(v0[...] + _dot(_dot(xv, v1p[...]), v2p[...]))
        xv = xv + (vf[...] - xv) * gate

    kk = xk * kkp[...]
    r2 = _iota2((LANES, LANES), 0)
    c2 = _iota2((LANES, LANES), 1)
    head_ones = ((r2 >> 6) == (c2 >> 6)).astype(BF16)
    kk2 = kk * kk
    ss = jnp.concatenate(
        [_dot(kk2[:, i * LANES:(i + 1) * LANES], head_ones) for i in range(MIX // LANES)],
        axis=1)
    kk = kk * lax.rsqrt(jnp.maximum(ss, 1e-24))

    r_o[...] = xr
    k_o[...] = xk * (1.0 + (a - 1.0) * kap[...])
    v_o[...] = xv
    kk_o[...] = kk
    a_o[...] = a
    ld_o[...] = _dot_sel_l(_chunk_cumsum_matrix(tm), -jnp.exp(w), 2)
    g_o[...] = g


def _rwkv_prep(p, psm, wts, v_first, seq):
    m = p.shape[0]
    tm = TM_PREP
    has_vres = v_first is not None
    tile = lambda cb: pl.BlockSpec((tm, MIX), lambda i, cb=cb: (i, cb))
    halo = lambda cb: pl.BlockSpec((8, MIX), lambda i, cb=cb: (jnp.maximum(i * (tm // 8) - 1, 0), cb))
    vec = pl.BlockSpec((1, MIX), lambda i: (0, 0))
    full = lambda arr: pl.BlockSpec(arr.shape, lambda i: (0,) * arr.ndim)
    in_specs = [tile(1), tile(2), tile(3), halo(1), halo(2), halo(3),
                pl.BlockSpec((tm, N_SMALL), lambda i: (i, 0)),
                pl.BlockSpec((8, N_SMALL), lambda i: (jnp.maximum(i * (tm // 8) - 1, 0), 0)),
                vec, vec, vec, pl.BlockSpec((1, N_SMALL), lambda i: (0, 0)),
                vec, vec, vec, vec,
                full(wts["w2p"]), full(wts["a2p"]), full(wts["g2p"])]
    args = [p, p, p, p, p, p, psm, psm, wts["mu_r"], wts["mu_k"], wts["mu_v"], wts["mu_s"],
            wts["w0"], wts["a0"], wts["k_k"], wts["k_a"], wts["w2p"], wts["a2p"], wts["g2p"]]
    if has_vres:
        in_specs += [pl.BlockSpec((tm, MIX), lambda i: (i, 0)), vec, full(wts["v1p"]), full(wts["v2p"])]
        args += [v_first, wts["v0"], wts["v1p"], wts["v2p"]]
    out = pl.BlockSpec((tm, MIX), lambda i: (i, 0))
    return pl.pallas_call(
        functools.partial(_rwkv_prep_kernel, seq=seq, tm=tm, has_vres=has_vres),
        grid=(m // tm,),
        in_specs=in_specs,
        out_specs=[out] * 7,
        out_shape=[jax.ShapeDtypeStruct((m, MIX), F32)] * 7,
        compiler_params=_params("parallel"),
        name="rwkv_prep",
    )(*args)


def _rwkv_core_kernel(r_ref, k_ref, v_ref, kk_ref, a_ref, gc_ref, g_ref, rk_ref, lnw_ref, lnb_ref,
                      o_ref, h_ref):
    @pl.when(pl.program_id(1) == 0)
    def _():
        h_ref[...] = jnp.zeros_like(h_ref)

    nb, rb, _ = r_ref.shape
    nch = rb // CHUNK
    units = [(b, c) for c in range(nch) for b in range(nb)]
    m = _block_masks(LANES)
    lane = _iota2((1, LANES), 1)
    m0 = (lane < RWKV_HEAD).astype(F32)
    m1 = 1.0 - m0
    r2 = _iota2((LANES, LANES), 0)
    c2 = _iota2((LANES, LANES), 1)
    head_ones = ((r2 >> 6) == (c2 >> 6)).astype(BF16)
    stack = lambda x: jnp.concatenate([x * m0, x * m1], axis=0)
    unstack = lambda x: x[:CHUNK] + x[CHUNK:]
    cat0 = lambda *xs: jnp.concatenate(xs, axis=0)
    cat1 = lambda *xs: jnp.concatenate(xs, axis=1)

    def load(ref, u):
        b, c = u
        return ref[b, c * CHUNK:(c + 1) * CHUNK, :]

    first_row = _iota2((CHUNK, LANES), 0) == 0
    q_t, y0, phi, psi = {}, {}, {}, {}

    def chunk_terms(group):
        a_st, r_st, rhs, v_st, bbar_t, kbar_t, dlast, vs = [], [], [], [], [], [], [], []
        for u in group:
            gc = load(gc_ref, u)
            g_prev = jnp.where(first_row, 0.0, pltpu.roll(gc, 1, 0))
            kc, vc, kkc = load(k_ref, u), load(v_ref, u), load(kk_ref, u)
            bc = kkc * load(a_ref, u)
            glast = gc[CHUNK - 1:CHUNK]
            e_neg = jnp.exp(-gc)
            e_rest = jnp.exp(glast - gc)
            a_st.append(stack(-kkc * jnp.exp(g_prev)))
            r_st.append(stack(load(r_ref, u) * jnp.exp(gc)))
            b_hat, k_hat = bc * e_neg, kc * e_neg
            rhs.append(cat0(b_hat, b_hat, k_hat, k_hat))
            v_st.append(stack(vc))
            bbar_t.append((bc * e_rest).T)
            kbar_t.append((kc * e_rest).T)
            dlast.append(jnp.exp(glast))
            vs.append(vc)
            yield
        big = [_dot_nt(cat0(a, r), x) for a, r, x in zip(a_st, r_st, rhs)]
        yield
        a_ab = [jnp.where(m["strict"], x[:LANES, :LANES], 0.0) for x in big]
        a_kr = [cat0(jnp.where(m["strict"], x[:LANES, LANES:], 0.0),
                     jnp.where(m["incl"], x[LANES:, LANES:], 0.0)) for x in big]
        a_rb = [jnp.where(m["incl"], x[LANES:, :LANES], 0.0) for x in big]
        x1 = [_dot(a, v) for a, v in zip(a_kr, v_st)]
        t_inv = yield from _tri_inv(a_ab, m)
        x2 = [_dot(t, cat1(a, x[:LANES])) for t, a, x in zip(t_inv, a_st, x1)]
        yield
        x3 = [_dot(a, x) for a, x in zip(a_rb, x2)]
        w_t = [unstack(x[:, :LANES]) for x in x2]
        u_t = [unstack(x[:, LANES:]) for x in x2]
        for i, u in enumerate(group):
            phi[u] = m["eye"] * dlast[i] + jnp.where(m["bd64"], _dot(bbar_t[i], w_t[i]), 0.0)
            psi[u] = jnp.where(m["bd64"], _dot(cat1(bbar_t[i], kbar_t[i]), cat0(u_t[i], vs[i])), 0.0)
        yield
        for i, u in enumerate(group):
            q_t[u] = unstack(r_st[i] + x3[i][:, :LANES])
            y0[u] = unstack(x3[i][:, LANES:] + x1[i][LANES:])

    _interleave([chunk_terms(units[i:i + CORE_GROUP]) for i in range(0, len(units), CORE_GROUP)],
                CORE_STAGGER)

    ys = {}
    hs = [h_ref[b] for b in range(nb)]
    for u in units:
        b = u[0]
        qh = _dot(cat0(q_t[u], phi[u]), hs[b])
        ys[u] = y0[u] + qh[:CHUNK]
        hs[b] = qh[CHUNK:] + psi[u]
    for b in range(nb):
        h_ref[b] = hs[b]

    y = cat0(*[ys[(b, c)] for b in range(nb) for c in range(nch)])
    flat = lambda ref: ref[...].reshape(nb * rb, LANES)
    rr, kk_, vv = flat(r_ref), flat(k_ref), flat(v_ref)
    sums = _dot(cat0(y, rr * kk_ * rk_ref[...]), head_ones)
    yc = y - sums[:nb * rb] * (1.0 / RWKV_HEAD)
    var = _dot(yc * yc, head_ones) * (1.0 / RWKV_HEAD)
    yn = yc * lax.rsqrt(var + RWKV_LN_EPS) * lnw_ref[...] + lnb_ref[...]
    bonus = sums[nb * rb:] * vv
    o_ref[...] = ((yn + bonus) * flat(g_ref)).astype(BF16).reshape(nb, rb, LANES)


def _rwkv_core(prep, wts, batch, seq):
    rb = RB_RWKV
    tile = pl.BlockSpec((batch, rb, LANES), lambda p, j: (0, j, p))
    vec = pl.BlockSpec((1, LANES), lambda p, j: (0, p))
    out = pl.pallas_call(
        _rwkv_core_kernel,
        grid=(MIX // LANES, seq // rb),
        in_specs=[tile] * 7 + [vec] * 3,
        out_specs=tile,
        out_shape=jax.ShapeDtypeStruct((batch, seq, MIX), BF16),
        scratch_shapes=[pltpu.VMEM((batch, LANES, LANES), F32)],
        compiler_params=_params("parallel", "arbitrary"),
        name="rwkv_core",
    )(*[a.reshape(batch, seq, MIX) for a in prep], wts["r_k"], wts["ln_w"], wts["ln_b"])
    return out.reshape(batch * seq, MIX)


def _gdn_prep_kernel(pq, pk, pv, hq, hk, hv, ps, cwq, cwk, cwv, alog, dtb,
                     q_o, k_o, v_o, gb_o, *, seq, tm):
    first = (pl.program_id(0) * tm) % seq == 0
    ones = jnp.ones((LANES, LANES), BF16)

    def conv_silu(p_ref, h_ref, w_ref):
        x = p_ref[...]
        halo = jnp.where(first, 0.0, h_ref[...])
        w = w_ref[...]
        y = x * w[3:4]
        for s in (1, 2, 3):
            y = y + _shift_rows(x, halo, s) * w[3 - s:4 - s]
        return _silu(y)

    def l2norm(y, scale):
        blocks = []
        for i in range(GDN_HEADS):
            blk = y[:, i * LANES:(i + 1) * LANES]
            ss = _dot(blk * blk, ones)
            blocks.append(blk * (lax.rsqrt(ss + GDN_EPS) * scale))
        return jnp.concatenate(blocks, axis=1)

    q_o[...] = l2norm(conv_silu(pq, hq, cwq), GDN_HEAD ** -0.5)
    k_o[...] = l2norm(conv_silu(pk, hk, cwk), 1.0)
    v_o[...] = conv_silu(pv, hv, cwv)

    raw = ps[:, N_SMALL - LANES:]
    gval = -jnp.exp(alog[...]) * _softplus(raw + dtb[...])
    gcum = _dot_sel_l(_chunk_cumsum_matrix(tm), gval, 3)
    lane = _iota2(raw.shape, 1)
    is_g = (lane >= GDN_G_LANE) & (lane < GDN_G_LANE + GDN_HEADS)
    is_b = (lane >= GDN_B_LANE) & (lane < GDN_B_LANE + GDN_HEADS)
    gb_o[...] = jnp.where(is_g, gcum, jnp.where(is_b, _sigmoid(raw), 0.0))


def _gdn_prep(p, psm, wts, seq):
    m = p.shape[0]
    tm = TM_PREP
    tile = lambda cb: pl.BlockSpec((tm, MIX), lambda i, cb=cb: (i, cb))
    halo = lambda cb: pl.BlockSpec((8, MIX), lambda i, cb=cb: (jnp.maximum(i * (tm // 8) - 1, 0), cb))
    cw = pl.BlockSpec((4, MIX), lambda i: (0, 0))
    lv = pl.BlockSpec((1, LANES), lambda i: (0, 0))
    out = pl.BlockSpec((tm, MIX), lambda i: (i, 0))
    return pl.pallas_call(
        functools.partial(_gdn_prep_kernel, seq=seq, tm=tm),
        grid=(m // tm,),
        in_specs=[tile(4), tile(5), tile(6), halo(4), halo(5), halo(6),
                  pl.BlockSpec((tm, N_SMALL), lambda i: (i, 0)), cw, cw, cw, lv, lv],
        out_specs=[out, out, out, pl.BlockSpec((tm, LANES), lambda i: (i, 0))],
        out_shape=[jax.ShapeDtypeStruct((m, MIX), F32)] * 3 + [jax.ShapeDtypeStruct((m, LANES), F32)],
        compiler_params=_params("parallel"),
        name="gdn_prep",
    )(p, p, p, p, p, p, psm, wts["cw_q"], wts["cw_k"], wts["cw_v"], wts["alog"], wts["dtb"])


def _gdn_core_kernel(q_ref, k_ref, v_ref, gb_ref, z_ref, nw_ref, o_ref, s_ref):
    @pl.when(pl.program_id(1) == 0)
    def _():
        s_ref[...] = jnp.zeros_like(s_ref)

    nb, rb, width = q_ref.shape
    nh = width // LANES
    units = [(b, hh) for hh in range(nh) for b in range(nb)]
    m = _block_masks(LANES)
    ones = jnp.ones((LANES, LANES), BF16)
    cat0 = lambda *xs: jnp.concatenate(xs, axis=0)
    cat1 = lambda *xs: jnp.concatenate(xs, axis=1)

    def lane_bcast(x, lane):
        return jnp.broadcast_to(pltpu.roll(x, LANES - lane, 1)[:, 0:1], x.shape)

    nch = rb // CHUNK
    q_t, y0, phi, psi = {}, {}, {}, {}

    def tile_terms(group):
        gf, ks, lhs, rhs, e_g, qs, gamma = [], [], [], [], [], [], []
        for b, hh in group:
            hl = slice(hh * LANES, (hh + 1) * LANES)
            head = nh * pl.program_id(0) + hh
            gbv = gb_ref[b]
            g = lane_bcast(gbv, GDN_G_LANE + head)
            beta = lane_bcast(gbv, GDN_B_LANE + head)
            q, k, v = q_ref[b, :, hl], k_ref[b, :, hl], v_ref[b, :, hl]
            diff = g - g.T
            gamma.append(jnp.where(m["incl"], jnp.exp(jnp.where(m["incl"], diff, 0.0)), 0.0))
            eg = jnp.exp(g)
            kb = k * beta
            gf.append(g)
            ks.append(k)
            qs.append(q)
            e_g.append(eg)
            lhs.append(cat0(kb, q))
            rhs.append(cat1(v * beta, kb * eg))
            yield
        kq = [_dot_nt(x, k) for x, k in zip(lhs, ks)]
        yield
        a_neg = [-jnp.where(m["strict"], x[:LANES] * gm, 0.0) for x, gm in zip(kq, gamma)]
        a_qk = [x[LANES:] * gm for x, gm in zip(kq, gamma)]
        k_bar_t = [[(k[c * CHUNK:(c + 1) * CHUNK]
                     * jnp.exp(g[(c + 1) * CHUNK - 1:(c + 1) * CHUNK] - g[c * CHUNK:(c + 1) * CHUNK])).T
                    for c in range(nch)] for k, g in zip(ks, gf)]
        t_inv = yield from _tri_inv(a_neg, m)
        uw = [_dot(t, x) for t, x in zip(t_inv, rhs)]
        yield
        x3 = [_dot(a, x) for a, x in zip(a_qk, uw)]
        for i, u in enumerate(group):
            for c in range(nch):
                glast = gf[i][(c + 1) * CHUNK - 1:(c + 1) * CHUNK]
                pp = _dot(k_bar_t[i][c], uw[i][c * CHUNK:(c + 1) * CHUNK])
                phi[(u, c)] = m["eye"] * jnp.exp(glast) - pp[:, LANES:]
                psi[(u, c)] = pp[:, :LANES]
        yield
        for i, u in enumerate(group):
            y0[u] = x3[i][:, :LANES]
            q_t[u] = qs[i] * e_g[i] - x3[i][:, LANES:]

    _interleave([tile_terms(units[i:i + CORE_GROUP]) for i in range(0, len(units), CORE_GROUP)],
                CORE_STAGGER)

    ys = {u: [] for u in units}
    st = {(b, hh): s_ref[hh, b] for b, hh in units}
    for c in range(nch):
        cs = slice(c * CHUNK, (c + 1) * CHUNK)
        for u in units:
            qs_ = _dot(cat0(q_t[u][cs], phi[(u, c)]), st[u])
            ys[u].append(y0[u][cs] + qs_[:CHUNK])
            st[u] = qs_[CHUNK:] + psi[(u, c)]
    for b, hh in units:
        s_ref[hh, b] = st[(b, hh)]

    for hh in range(nh):
        hl = slice(hh * LANES, (hh + 1) * LANES)
        y = cat0(*[yc for b in range(nb) for yc in ys[(b, hh)]])
        ms = _dot(y * y, ones) * (1.0 / GDN_HEAD)
        out = y * lax.rsqrt(ms + GDN_EPS) * nw_ref[...]
        z = z_ref[:, :, hl].reshape(nb * rb, LANES)
        o_ref[:, :, hl] = (out * _silu(z)).astype(BF16).reshape(nb, rb, LANES)


def _gdn_core(q, k, v, gb, p, norm_w, batch, seq):
    rb, nh = RB_GDN, GDN_HEADS_PER_STEP
    width = nh * LANES
    tile = pl.BlockSpec((batch, rb, width), lambda h, j: (0, j, h))
    z_col0 = 7 * (MIX // width)
    r3 = lambda a: a.reshape(batch, seq, a.shape[-1])
    out = pl.pallas_call(
        _gdn_core_kernel,
        grid=(GDN_HEADS // nh, seq // rb),
        in_specs=[tile, tile, tile,
                  pl.BlockSpec((batch, rb, LANES), lambda h, j: (0, j, 0)),
                  pl.BlockSpec((batch, rb, width), lambda h, j: (0, j, z_col0 + h)),
                  pl.BlockSpec((1, LANES), lambda h, j: (0, 0))],
        out_specs=tile,
        out_shape=jax.ShapeDtypeStruct((batch, seq, MIX), BF16),
        scratch_shapes=[pltpu.VMEM((nh, batch, LANES, LANES), F32)],
        compiler_params=_params("parallel", "arbitrary"),
        name="gdn_core",
    )(r3(q), r3(k), r3(v), r3(gb), r3(p), norm_w.reshape(1, LANES))
    return out.reshape(batch * seq, MIX)


def _merge_kernel(ya, yb, yc, wa, wb, wc, ga, gb, gc, o_ref):
    dot = lambda y, w: jnp.dot(y[...], w[...], preferred_element_type=F32)
    acc = ga[...].astype(F32) * dot(ya, wa)
    acc = acc + gb[...].astype(F32) * dot(yb, wb)
    acc = acc + gc[...].astype(F32) * dot(yc, wc)
    o_ref[...] = acc.astype(BF16)


def _merge(ya, yb, yc, wa, wb, wc, gates):
    m = ya.shape[0]
    d = wa.shape[1]
    tm = TM_LN
    y = pl.BlockSpec((tm, MIX), lambda i: (i, 0))
    w = pl.BlockSpec((MIX, d), lambda i: (0, 0), pipeline_mode=pl.Buffered(1))
    g = lambda br: pl.BlockSpec((tm, d), lambda i, br=br: (i, br))
    return pl.pallas_call(
        _merge_kernel,
        grid=(m // tm,),
        in_specs=[y, y, y, w, w, w, g(0), g(1), g(2)],
        out_specs=pl.BlockSpec((tm, d), lambda i: (i, 0)),
        out_shape=jax.ShapeDtypeStruct((m, d), BF16),
        compiler_params=_params("parallel"),
        name="merge",
    )(ya, yb, yc, wa, wb, wc, gates, gates, gates)


def _proj_ln_kernel(*refs, emit_h, sub):
    a_ref, w_ref, x_ref, gt_ref, lnw_ref, lnb_ref = refs[:6]
    if emit_h:
        sc_ref, sh_ref, xo_ref, ho_ref = refs[6:]
    else:
        (xo_ref,) = refs[6:]
    nsub = a_ref.shape[0] // sub
    w = w_ref[...]
    dot = lambda r: jnp.dot(a_ref[r * sub:(r + 1) * sub, :], w, preferred_element_type=F32)
    y_next = dot(0)
    for r in range(nsub):
        y = y_next
        if r + 1 < nsub:
            y_next = dot(r + 1)
        rows = slice(r * sub, (r + 1) * sub)
        xn = _resid_ln(x_ref[rows, :], y, gt_ref[0], lnw_ref[...], lnb_ref[...])
        xo_ref[rows, :] = xn
        if emit_h:
            ho_ref[rows, :] = (xn * (1.0 + sc_ref[0]) + sh_ref[0]).astype(BF16)


def _proj_ln(a, w, x, gt, lnw, lnb, nxt, seq, *, tm, sub, name):
    m, k = a.shape
    d = w.shape[1]
    per = seq // tm
    emit_h = nxt is not None
    row = pl.BlockSpec((tm, d), lambda i: (i, 0))
    mod = pl.BlockSpec((1, 1, d), lambda i: (i // per, 0, 0))
    vec = pl.BlockSpec((1, d), lambda i: (0, 0))
    in_specs = [pl.BlockSpec((tm, k), lambda i: (i, 0)),
                pl.BlockSpec((k, d), lambda i: (0, 0), pipeline_mode=pl.Buffered(1)),
                row, mod, vec, vec]
    args = [a, w, x, gt, lnw.reshape(1, d), lnb.reshape(1, d)]
    out_specs, out_shape = [row], [jax.ShapeDtypeStruct((m, d), F32)]
    if emit_h:
        in_specs += [mod, mod]
        args += list(nxt)
        out_specs.append(row)
        out_shape.append(jax.ShapeDtypeStruct((m, d), BF16))
    res = pl.pallas_call(
        functools.partial(_proj_ln_kernel, emit_h=emit_h, sub=sub),
        grid=(m // tm,),
        in_specs=in_specs,
        out_specs=out_specs,
        out_shape=out_shape,
        compiler_params=_params("parallel"),
        name=name,
    )(*args)
    return (res[0], res[1]) if emit_h else (res[0], None)


def _swiglu_kernel(a_ref, wg_ref, wu_ref, o_ref):
    a = a_ref[...]
    g = jnp.dot(a, wg_ref[...], preferred_element_type=F32)
    u = jnp.dot(a, wu_ref[...], preferred_element_type=F32)
    o_ref[...] = (_silu(g) * u).astype(BF16)


def _swiglu(h, w_up):
    m, k = h.shape
    dff = w_up.shape[1] // 2
    tm, tn = TM_MM, 512
    nj = dff // tn
    return pl.pallas_call(
        _swiglu_kernel,
        grid=(m // tm, nj),
        in_specs=[pl.BlockSpec((tm, k), lambda i, j: (i, 0)),
                  pl.BlockSpec((k, tn), lambda i, j: (0, j)),
                  pl.BlockSpec((k, tn), lambda i, j: (0, nj + j))],
        out_specs=pl.BlockSpec((tm, tn), lambda i, j: (i, j)),
        out_shape=jax.ShapeDtypeStruct((m, dff), BF16),
        compiler_params=_params("parallel", "parallel"),
        name="ffn_up_swiglu",
    )(h, w_up, w_up)


def _layer_weights(l, w_in, rwkv_mu, rwkv_w0, rwkv_w2, rwkv_a0, rwkv_a2, rwkv_g2, rwkv_k_k, rwkv_k_a,
                   rwkv_r_k, rwkv_ln_w, rwkv_ln_b, rwkv_v0, rwkv_v1, rwkv_v2, gdn_conv_w, gdn_a_log,
                   gdn_dt_bias):
    c = MIX
    wl = w_in[l]
    o_rwkv = c
    o_lora = o_rwkv + 3 * c
    n_lora = rwkv_w2.shape[1] + rwkv_a2.shape[1] + rwkv_g2.shape[1]
    o_gdn = o_lora + n_lora
    o_ab = o_gdn + 4 * c
    o_gate = o_ab + 2 * GDN_HEADS
    pad_cols = N_SMALL - n_lora - 2 * GDN_HEADS
    w_main = jnp.concatenate([wl[:, :o_lora], wl[:, o_gdn:o_ab]], axis=1).astype(BF16)
    assert n_lora - (N_SMALL - LANES) == GDN_G_LANE and GDN_B_LANE == GDN_G_LANE + GDN_HEADS
    w_small = jnp.concatenate([wl[:, o_lora:o_gdn], wl[:, o_ab:o_gate],
                               jnp.zeros((wl.shape[0], pad_cols), wl.dtype)], axis=1).astype(BF16)
    w_gate = wl[:, o_gate:].astype(BF16)

    mu = rwkv_mu[l]
    row = lambda v: v.reshape(1, -1)
    n_w, n_a, n_g = rwkv_w2.shape[1], rwkv_a2.shape[1], rwkv_g2.shape[1]
    padrows = lambda w, lo: jnp.zeros((N_SMALL, c), F32).at[lo:lo + w.shape[0]].set(w).astype(BF16)
    lane_vec = lambda v, lo: jnp.zeros((1, LANES), F32).at[0, lo:lo + v.shape[0]].set(v)
    rw = {
        "mu_r": row(mu[:c]), "mu_k": row(mu[c:2 * c]), "mu_v": row(mu[2 * c:3 * c]),
        "mu_s": jnp.zeros((1, N_SMALL), F32).at[0, :n_lora].set(mu[3 * c:]),
        "w0": row(rwkv_w0[l]), "a0": row(rwkv_a0[l]), "k_k": row(rwkv_k_k[l]), "k_a": row(rwkv_k_a[l]),
        "w2p": padrows(rwkv_w2[l], 0), "a2p": padrows(rwkv_a2[l], n_w), "g2p": padrows(rwkv_g2[l], n_w + n_a),
        "r_k": row(rwkv_r_k[l]), "ln_w": row(rwkv_ln_w[l]), "ln_b": row(rwkv_ln_b[l]),
    }
    if l > 0:
        nv = rwkv_v1.shape[2]
        rw["v0"] = row(rwkv_v0[l - 1])
        rw["v1p"] = jnp.zeros((c, LANES), F32).at[:, :nv].set(rwkv_v1[l - 1]).astype(BF16)
        rw["v2p"] = jnp.zeros((LANES, c), F32).at[:nv].set(rwkv_v2[l - 1]).astype(BF16)
    cw = gdn_conv_w[l]
    gd = {"cw_q": cw[:, :c], "cw_k": cw[:, c:2 * c], "cw_v": cw[:, 2 * c:],
          "alog": lane_vec(gdn_a_log[l], GDN_G_LANE), "dtb": lane_vec(gdn_dt_bias[l], GDN_G_LANE)}
    return w_main, w_small, w_gate, rw, gd


def kernel(x, c, ada_w, ada_b, w_in, pool_w, pool_scale, rwkv_mu, rwkv_w0, rwkv_w2, rwkv_a0, rwkv_a2, rwkv_g2, rwkv_k_k, rwkv_k_a, rwkv_r_k, rwkv_ln_w, rwkv_ln_b, rwkv_v0, rwkv_v1, rwkv_v2, gdn_conv_w, gdn_a_log, gdn_dt_bias, gdn_norm_w, w_branch_a, w_branch_b, w_branch_c, w_out, ln1_w, ln1_b, ffn_w_up, ffn_w_down, ln2_w, ln2_b):
    batch, seq, d = x.shape
    assert seq % TM_MM == 0 and d == 2 * MIX
    m = batch * seq
    xf = x.reshape(m, d)

    mod = _ada(c, ada_w, ada_b)
    mods = [[mod[l, :, i * d:(i + 1) * d].reshape(batch, 1, d) for i in range(6)]
            for l in range(DEPTH)]

    h = _modulate(xf, mods[0][1], mods[0][0], seq)
    v_first = None
    for l in range(DEPTH):
        sh_m, sc_m, gt_m, sh_f, sc_f, gt_f = mods[l]
        w_main, w_small, w_gate, rw, gd = _layer_weights(
            l, w_in, rwkv_mu, rwkv_w0, rwkv_w2, rwkv_a0, rwkv_a2, rwkv_g2, rwkv_k_k, rwkv_k_a,
            rwkv_r_k, rwkv_ln_w, rwkv_ln_b, rwkv_v0, rwkv_v1, rwkv_v2, gdn_conv_w, gdn_a_log,
            gdn_dt_bias)

        p = _matmul(h, w_main, tn=1024, out_dtype=F32, name="in_proj_main")
        psm = _matmul(h, w_small, tn=N_SMALL, out_dtype=F32, name="in_proj_small")
        gates = _matmul(h, w_gate, tn=1024, out_dtype=BF16, act="sigmoid", name="in_proj_gates")

        y_a = _pool(p, pool_w[l], pool_scale[l], seq)
        prep = _rwkv_prep(p, psm, rw, v_first, seq)
        if l == 0:
            v_first = prep[2]
        y_b = _rwkv_core(prep, rw, batch, seq)
        gq, gk, gv, ggb = _gdn_prep(p, psm, gd, seq)
        y_c = _gdn_core(gq, gk, gv, ggb, p, gdn_norm_w[l], batch, seq)

        merged = _merge(y_a, y_b, y_c, w_branch_a[l].astype(BF16), w_branch_b[l].astype(BF16),
                        w_branch_c[l].astype(BF16), gates)
        xf, h = _proj_ln(merged, w_out[l].astype(BF16), xf, gt_m, ln1_w[l], ln1_b[l],
                         (sc_f, sh_f), seq, tm=512, sub=256, name="out_proj_ln")

        act = _swiglu(h, ffn_w_up[l].astype(BF16))
        nxt = (mods[l + 1][1], mods[l + 1][0]) if l + 1 < DEPTH else None
        xf, h = _proj_ln(act, ffn_w_down[l].astype(BF16), xf, gt_f, ln2_w[l], ln2_b[l],
                         nxt, seq, tm=256, sub=256, name="ffn_down_ln")
    return xf.reshape(batch, seq, d)
```

```python
import functools

import jax
import jax.numpy as jnp
from jax import lax
from jax.experimental import pallas as pl
from jax.experimental.pallas import tpu as pltpu

F32 = jnp.float32
BF16 = jnp.bfloat16

DEPTH = 2
DN_ALPHA = (2 * DEPTH) ** 0.25
LN_EPS = 1e-5
MIX = 1024
POOL_WINDOWS = (2, 4, 8, 16)
POOL_GROUP = 256
RWKV_HEAD = 64
RWKV_LN_EPS = RWKV_HEAD * 1e-5
GDN_HEAD = 128
GDN_HEADS = 8
GDN_EPS = 1e-6
CHUNK = 64
LANES = 128
N_SMALL = 384
GDN_G_LANE = 32
GDN_B_LANE = 40
VMEM_LIMIT = 56 * 1024 * 1024

TM_MM = 1024
TM_LN = 512
TM_PREP = 256
RB_RWKV = 256
RB_GDN = 128
GDN_HEADS_PER_STEP = 4
CORE_GROUP = 16
CORE_STAGGER = 1


def _dot(a, b):
    return jnp.dot(a.astype(BF16), b.astype(BF16), preferred_element_type=F32)


def _dot_nt(a, b):
    return lax.dot_general(a.astype(BF16), b.astype(BF16), (((1,), (1,)), ((), ())),
                           preferred_element_type=F32)


def _split(a, n):
    terms, rest = [], a
    for _ in range(n):
        t = rest.astype(BF16)
        terms.append(t)
        rest = rest - t.astype(F32)
    return terms


def _dot3(a, b, nt=False):
    d = _dot_nt if nt else _dot
    a1, a2 = _split(a, 2)
    b1, b2 = _split(b, 2)
    return d(a1, b1) + (d(a1, b2) + d(a2, b1))


def _dot_sel_l(sel, b, n):
    out = None
    for t in _split(b, n):
        p = jnp.dot(sel, t, preferred_element_type=F32)
        out = p if out is None else out + p
    return out


def _sigmoid(x):
    return 0.5 * jnp.tanh(0.5 * x) + 0.5


def _silu(x):
    return x * _sigmoid(x)


def _softplus(x):
    return jnp.maximum(x, 0.0) + jnp.log(1.0 + jnp.exp(-jnp.abs(x)))


def _iota2(shape, dim):
    return lax.broadcasted_iota(jnp.int32, shape, dim)


def _block_masks(n):
    r = _iota2((n, n), 0)
    c = _iota2((n, n), 1)
    same = lambda s: (r >> s) == (c >> s)
    m = {
        "eye": (r == c).astype(F32),
        "bd8": same(3),
        "off16": same(4) & jnp.logical_not(same(3)),
        "off32": same(5) & jnp.logical_not(same(4)),
        "off64": same(6) & jnp.logical_not(same(5)),
        "bd64": same(6),
    }
    m["strict"] = m["bd64"] & (r > c)
    m["incl"] = m["bd64"] & (r >= c)
    return m


def _chunk_cumsum_matrix(n):
    r = _iota2((n, n), 0)
    c = _iota2((n, n), 1)
    return (((r >> 6) == (c >> 6)) & (c <= r)).astype(BF16)


def _tri_inv(mats, m):
    a8 = [jnp.where(m["bd8"], a, 0.0) for a in mats]
    s = [m["eye"] + x for x in a8]
    p = [_dot(x, x) for x in a8]
    yield
    q = [_dot(pi, jnp.concatenate([si, pi], axis=1)) for pi, si in zip(p, s)]
    yield
    s = [si + qi[:, :LANES] for si, qi in zip(s, q)]
    s = [si + _dot(qi[:, LANES:], si) for si, qi in zip(s, q)]
    yield
    for key in ("off16", "off32", "off64"):
        t = [_dot(jnp.where(m[key], a, 0.0), si) for a, si in zip(mats, s)]
        yield
        s = [si + _dot(si, ti) for si, ti in zip(s, t)]
        yield
    return s


def _interleave(gens, stagger):
    active, pending, tick = [], list(gens), 0
    while active or pending:
        if pending and tick % stagger == 0:
            active.append(pending.pop(0))
        tick += 1
        for g in list(active):
            try:
                next(g)
            except StopIteration:
                active.remove(g)


def _shift_rows(x, halo, s):
    halo = halo[halo.shape[0] - 8:]
    xr = pltpu.roll(x, s, 0)
    hr = pltpu.roll(halo, s, 0)
    rows = _iota2(halo.shape, 0)
    top = jnp.where(rows < s, hr, xr[:8])
    return jnp.concatenate([top, xr[8:]], axis=0)


def _resid_ln(x, y, gt, lnw, lnb):
    z = DN_ALPHA * x + (1.0 + gt) * y
    mu = jnp.mean(z, axis=-1, keepdims=True)
    zc = z - mu
    var = jnp.mean(zc * zc, axis=-1, keepdims=True)
    return zc * lax.rsqrt(var + LN_EPS) * lnw + lnb


def _params(*sem):
    return pltpu.CompilerParams(dimension_semantics=sem, vmem_limit_bytes=VMEM_LIMIT)


def _ada_kernel(c_ref, w_ref, b_ref, o_ref):
    c = c_ref[...]
    o_ref[0] = _dot3(_silu(c), w_ref[0]) + b_ref[0]


def _ada(c, ada_w, ada_b):
    nl, d, n = ada_w.shape
    b = c.shape[0]
    tn = 512
    cp = jnp.zeros((8, d), F32).at[:b].set(c)
    out = pl.pallas_call(
        _ada_kernel,
        grid=(nl, n // tn),
        in_specs=[pl.BlockSpec((8, d), lambda l, j: (0, 0)),
                  pl.BlockSpec((1, d, tn), lambda l, j: (l, 0, j)),
                  pl.BlockSpec((1, 1, tn), lambda l, j: (l, 0, j))],
        out_specs=pl.BlockSpec((1, 8, tn), lambda l, j: (l, 0, j)),
        out_shape=jax.ShapeDtypeStruct((nl, 8, n), F32),
        compiler_params=_params("parallel", "parallel"),
        name="ada_mod",
    )(cp, ada_w, ada_b.reshape(nl, 1, n))
    return out[:, :b]


def _modulate_kernel(x_ref, sc_ref, sh_ref, h_ref):
    h_ref[...] = (x_ref[...] * (1.0 + sc_ref[0]) + sh_ref[0]).astype(BF16)


def _modulate(x, sc, sh, seq):
    m, d = x.shape
    tm = TM_LN
    per = seq // tm
    mod = pl.BlockSpec((1, 1, d), lambda i: (i // per, 0, 0))
    return pl.pallas_call(
        _modulate_kernel,
        grid=(m // tm,),
        in_specs=[pl.BlockSpec((tm, d), lambda i: (i, 0)), mod, mod],
        out_specs=pl.BlockSpec((tm, d), lambda i: (i, 0)),
        out_shape=jax.ShapeDtypeStruct((m, d), BF16),
        compiler_params=_params("parallel"),
        name="modulate",
    )(x, sc, sh)


def _mm_kernel(a_ref, bt_ref, o_ref, *, act):
    acc = lax.dot_general(a_ref[...], bt_ref[...], (((1,), (1,)), ((), ())),
                          preferred_element_type=F32)
    if act == "sigmoid":
        acc = _sigmoid(acc)
    o_ref[...] = acc.astype(o_ref.dtype)


def _matmul_nt(a, bt, *, tn, out_dtype, act=None, name):
    m, k = a.shape
    n = bt.shape[0]
    tm = TM_MM
    return pl.pallas_call(
        functools.partial(_mm_kernel, act=act),
        grid=(m // tm, n // tn),
        in_specs=[pl.BlockSpec((tm, k), lambda i, j: (i, 0)),
                  pl.BlockSpec((tn, k), lambda i, j: (j, 0))],
        out_specs=pl.BlockSpec((tm, tn), lambda i, j: (i, j)),
        out_shape=jax.ShapeDtypeStruct((m, n), out_dtype),
        compiler_params=_params("parallel", "parallel"),
        name=name,
    )(a, bt)


def _pool_kernel(p_ref, halo_ref, w_ref, scale_ref, o_ref, *, seq, tm):
    t0 = (pl.program_id(0) * tm) % seq
    x = p_ref[...].astype(F32)
    halo = jnp.where(t0 == 0, 0.0, halo_ref[...].astype(F32))
    ext = jnp.concatenate([halo, x], axis=0)
    sums = [ext]
    for sh in (1, 2, 4, 8):
        sums.append(sums[-1] + pltpu.roll(sums[-1], sh, 0))
    t = t0 + _iota2((tm, POOL_GROUP), 0)
    ys = []
    for g, win in enumerate(POOL_WINDOWS):
        cols = slice(g * POOL_GROUP, (g + 1) * POOL_GROUP)
        cnt = jnp.minimum(t + 1, win).astype(F32)
        pooled = sums[g + 1][16:, cols] / cnt - x[:, cols]
        ys.append(_dot(pooled, w_ref[g]))
    o_ref[...] = (jnp.concatenate(ys, axis=1) * scale_ref[...]).astype(BF16)


def _pool(p, pool_w, pool_scale, seq):
    m = p.shape[0]
    tm = TM_LN
    return pl.pallas_call(
        functools.partial(_pool_kernel, seq=seq, tm=tm),
        grid=(m // tm,),
        in_specs=[pl.BlockSpec((tm, MIX), lambda i: (i, 0)),
                  pl.BlockSpec((16, MIX), lambda i: (jnp.maximum(i * (tm // 16) - 1, 0), 0)),
                  pl.BlockSpec(pool_w.shape, lambda i: (0, 0, 0)),
                  pl.BlockSpec((1, MIX), lambda i: (0, 0))],
        out_specs=pl.BlockSpec((tm, MIX), lambda i: (i, 0)),
        out_shape=jax.ShapeDtypeStruct((m, MIX), BF16),
        compiler_params=_params("parallel"),
        name="pool_mixer",
    )(p, p, pool_w.astype(BF16), pool_scale.reshape(1, MIX))


def _rwkv_prep_kernel(*refs, seq, tm, has_vres):
    (pr, pk, pv, hr, hk, hv, ps, hs, mur, muk, muv, mus, w0, a0, kkp, kap,
     w2p, a2p, g2p) = refs[:19]
    if has_vres:
        vf, v0, v1p, v2p = refs[19:23]
        outs = refs[23:]
    else:
        outs = refs[19:]
    r_o, k_o, v_o, kk_o, a_o, ld_o, g_o = outs

    first = (pl.program_id(0) * tm) % seq == 0

    def lerp(p_ref, h_ref, mu_ref):
        p = p_ref[...].astype(F32)
        prev = _shift_rows(p, jnp.where(first, 0.0, h_ref[...].astype(F32)), 1)
        return p + (prev - p) * mu_ref[...]

    xr = lerp(pr, hr, mur)
    xk = lerp(pk, hk, muk)
    xv = lerp(pv, hv, muv)
    xs = lerp(ps, hs, mus)

    z = w0[...] + _dot(jnp.tanh(xs), w2p[...])
    w = jnp.minimum(z, 0.0) - jnp.log(1.0 + jnp.exp(-jnp.abs(z))) - 0.5
    a = _sigmoid(a0[...] + _dot(xs, a2p[...]))
    g = _dot(_sigmoid(xs), g2p[...])
    if has_vres:
        gate = _sigmoid(v0[...] + _dot(_dot(xv, v1p[...]), v2p[...]))
        xv = xv + (vf[...].astype(F32) - xv) * gate

    kk = xk * kkp[...]
    r2 = _iota2((LANES, LANES), 0)
    c2 = _iota2((LANES, LANES), 1)
    head_ones = ((r2 >> 6) == (c2 >> 6)).astype(BF16)
    kk2 = kk * kk
    ss = jnp.concatenate(
        [_dot(kk2[:, i * LANES:(i + 1) * LANES], head_ones) for i in range(MIX // LANES)],
        axis=1)
    kk = kk * lax.rsqrt(jnp.maximum(ss, 1e-24))

    r_o[...] = xr.astype(r_o.dtype)
    k_o[...] = (xk * (1.0 + (a - 1.0) * kap[...])).astype(k_o.dtype)
    v_o[...] = xv.astype(v_o.dtype)
    kk_o[...] = kk.astype(kk_o.dtype)
    a_o[...] = a.astype(a_o.dtype)
    ld_o[...] = _dot_sel_l(_chunk_cumsum_matrix(tm), -jnp.exp(w), 2)
    g_o[...] = g.astype(g_o.dtype)


def _rwkv_prep(p, psm, wts, v_first, seq):
    m = p.shape[0]
    tm = TM_PREP
    has_vres = v_first is not None
    tile = lambda cb: pl.BlockSpec((tm, MIX), lambda i, cb=cb: (i, cb))
    halo = lambda cb: pl.BlockSpec((16, MIX), lambda i, cb=cb: (jnp.maximum(i * (tm // 16) - 1, 0), cb))
    vec = pl.BlockSpec((1, MIX), lambda i: (0, 0))
    full = lambda arr: pl.BlockSpec(arr.shape, lambda i: (0,) * arr.ndim)
    in_specs = [tile(1), tile(2), tile(3), halo(1), halo(2), halo(3),
                pl.BlockSpec((tm, N_SMALL), lambda i: (i, 0)),
                pl.BlockSpec((8, N_SMALL), lambda i: (jnp.maximum(i * (tm // 8) - 1, 0), 0)),
                vec, vec, vec, pl.BlockSpec((1, N_SMALL), lambda i: (0, 0)),
                vec, vec, vec, vec,
                full(wts["w2p"]), full(wts["a2p"]), full(wts["g2p"])]
    args = [p, p, p, p, p, p, psm, psm, wts["mu_r"], wts["mu_k"], wts["mu_v"], wts["mu_s"],
            wts["w0"], wts["a0"], wts["k_k"], wts["k_a"], wts["w2p"], wts["a2p"], wts["g2p"]]
    if has_vres:
        in_specs += [pl.BlockSpec((tm, MIX), lambda i: (i, 0)), vec, full(wts["v1p"]), full(wts["v2p"])]
        args += [v_first, wts["v0"], wts["v1p"], wts["v2p"]]
    out = pl.BlockSpec((tm, MIX), lambda i: (i, 0))
    return pl.pallas_call(
        functools.partial(_rwkv_prep_kernel, seq=seq, tm=tm, has_vres=has_vres),
        grid=(m // tm,),
        in_specs=in_specs,
        out_specs=[out] * 7,
        out_shape=[jax.ShapeDtypeStruct((m, MIX), dt) for dt in (BF16,) * 5 + (F32, BF16)],
        compiler_params=_params("parallel"),
        name="rwkv_prep",
    )(*args)


def _rwkv_core_kernel(r_ref, k_ref, v_ref, kk_ref, a_ref, gc_ref, g_ref, rk_ref, lnw_ref, lnb_ref,
                      o_ref, h_ref):
    @pl.when(pl.program_id(1) == 0)
    def _():
        h_ref[...] = jnp.zeros_like(h_ref)

    nb, rb, _ = r_ref.shape
    nch = rb // CHUNK
    units = [(b, c) for c in range(nch) for b in range(nb)]
    m = _block_masks(LANES)
    lane = _iota2((1, LANES), 1)
    m0 = (lane < RWKV_HEAD).astype(F32)
    m1 = 1.0 - m0
    r2 = _iota2((LANES, LANES), 0)
    c2 = _iota2((LANES, LANES), 1)
    head_ones = ((r2 >> 6) == (c2 >> 6)).astype(BF16)
    stack = lambda x: jnp.concatenate([x * m0, x * m1], axis=0)
    unstack = lambda x: x[:CHUNK] + x[CHUNK:]
    cat0 = lambda *xs: jnp.concatenate(xs, axis=0)
    cat1 = lambda *xs: jnp.concatenate(xs, axis=1)

    def load(ref, u):
        b, c = u
        return ref[b, c * CHUNK:(c + 1) * CHUNK, :].astype(F32)

    first_row = _iota2((CHUNK, LANES), 0) == 0
    q_t, y0, phi, psi = {}, {}, {}, {}

    def chunk_terms(group):
        a_st, r_st, rhs, v_st, bbar_t, kbar_t, dlast, vs = [], [], [], [], [], [], [], []
        for u in group:
            gc = load(gc_ref, u)
            g_prev = jnp.where(first_row, 0.0, pltpu.roll(gc, 1, 0))
            kc, vc, kkc = load(k_ref, u), load(v_ref, u), load(kk_ref, u)
            bc = kkc * load(a_ref, u)
            glast = gc[CHUNK - 1:CHUNK]
            e_neg = jnp.exp(-gc)
            e_rest = jnp.exp(glast - gc)
            a_st.append(stack(-kkc * jnp.exp(g_prev)))
            r_st.append(stack(load(r_ref, u) * jnp.exp(gc)))
            b_hat, k_hat = bc * e_neg, kc * e_neg
            rhs.append(cat0(b_hat, b_hat, k_hat, k_hat))
            v_st.append(stack(vc))
            bbar_t.append((bc * e_rest).T)
            kbar_t.append((kc * e_rest).T)
            dlast.append(jnp.exp(glast))
            vs.append(vc)
            yield
        big = [_dot_nt(cat0(a, r), x) for a, r, x in zip(a_st, r_st, rhs)]
        yield
        a_ab = [jnp.where(m["strict"], x[:LANES, :LANES], 0.0) for x in big]
        a_kr = [cat0(jnp.where(m["strict"], x[:LANES, LANES:], 0.0),
                     jnp.where(m["incl"], x[LANES:, LANES:], 0.0)) for x in big]
        a_rb = [jnp.where(m["incl"], x[LANES:, :LANES], 0.0) for x in big]
        x1 = [_dot(a, v) for a, v in zip(a_kr, v_st)]
        t_inv = yield from _tri_inv(a_ab, m)
        x2 = [_dot(t, cat1(a, x[:LANES])) for t, a, x in zip(t_inv, a_st, x1)]
        yield
        x3 = [_dot(a, x) for a, x in zip(a_rb, x2)]
        w_t = [unstack(x[:, :LANES]) for x in x2]
        u_t = [unstack(x[:, LANES:]) for x in x2]
        for i, u in enumerate(group):
            phi[u] = m["eye"] * dlast[i] + jnp.where(m["bd64"], _dot(bbar_t[i], w_t[i]), 0.0)
            psi[u] = jnp.where(m["bd64"], _dot(cat1(bbar_t[i], kbar_t[i]), cat0(u_t[i], vs[i])), 0.0)
        yield
        for i, u in enumerate(group):
            q_t[u] = unstack(r_st[i] + x3[i][:, :LANES])
            y0[u] = unstack(x3[i][:, LANES:] + x1[i][LANES:])

    _interleave([chunk_terms(units[i:i + CORE_GROUP]) for i in range(0, len(units), CORE_GROUP)],
                CORE_STAGGER)

    ys = {}
    hs = [h_ref[b] for b in range(nb)]
    for u in units:
        b = u[0]
        qh = _dot(cat0(q_t[u], phi[u]), hs[b])
        ys[u] = y0[u] + qh[:CHUNK]
        hs[b] = qh[CHUNK:] + psi[u]
    for b in range(nb):
        h_ref[b] = hs[b]

    y = cat0(*[ys[(b, c)] for b in range(nb) for c in range(nch)])
    flat = lambda ref: ref[...].astype(F32).reshape(nb * rb, LANES)
    rr, kk_, vv = flat(r_ref), flat(k_ref), flat(v_ref)
    sums = _dot(cat0(y, rr * kk_ * rk_ref[...]), head_ones)
    yc = y - sums[:nb * rb] * (1.0 / RWKV_HEAD)
    var = _dot(yc * yc, head_ones) * (1.0 / RWKV_HEAD)
    yn = yc * lax.rsqrt(var + RWKV_LN_EPS) * lnw_ref[...] + lnb_ref[...]
    bonus = sums[nb * rb:] * vv
    o_ref[...] = ((yn + bonus) * flat(g_ref)).astype(BF16).reshape(nb, rb, LANES)


def _rwkv_core(prep, wts, batch, seq):
    rb = RB_RWKV
    tile = pl.BlockSpec((batch, rb, LANES), lambda p, j: (0, j, p))
    vec = pl.BlockSpec((1, LANES), lambda p, j: (0, p))
    out = pl.pallas_call(
        _rwkv_core_kernel,
        grid=(MIX // LANES, seq // rb),
        in_specs=[tile] * 7 + [vec] * 3,
        out_specs=tile,
        out_shape=jax.ShapeDtypeStruct((batch, seq, MIX), BF16),
        scratch_shapes=[pltpu.VMEM((batch, LANES, LANES), F32)],
        compiler_params=_params("parallel", "arbitrary"),
        name="rwkv_core",
    )(*[a.reshape(batch, seq, MIX) for a in prep], wts["r_k"], wts["ln_w"], wts["ln_b"])
    return out.reshape(batch * seq, MIX)


def _gdn_prep_kernel(pq, pk, pv, hq, hk, hv, ps, cwq, cwk, cwv, alog, dtb,
                     q_o, k_o, v_o, gb_o, *, seq, tm):
    first = (pl.program_id(0) * tm) % seq == 0
    ones = jnp.ones((LANES, LANES), BF16)

    def conv_silu(p_ref, h_ref, w_ref):
        x = p_ref[...].astype(F32)
        halo = jnp.where(first, 0.0, h_ref[...].astype(F32))
        w = w_ref[...]
        y = x * w[3:4]
        for s in (1, 2, 3):
            y = y + _shift_rows(x, halo, s) * w[3 - s:4 - s]
        return _silu(y)

    def l2norm(y, scale):
        blocks = []
        for i in range(GDN_HEADS):
            blk = y[:, i * LANES:(i + 1) * LANES]
            ss = _dot(blk * blk, ones)
            blocks.append(blk * (lax.rsqrt(ss + GDN_EPS) * scale))
        return jnp.concatenate(blocks, axis=1)

    q_o[...] = l2norm(conv_silu(pq, hq, cwq), GDN_HEAD ** -0.5).astype(q_o.dtype)
    k_o[...] = l2norm(conv_silu(pk, hk, cwk), 1.0).astype(k_o.dtype)
    v_o[...] = conv_silu(pv, hv, cwv).astype(v_o.dtype)

    raw = ps[:, N_SMALL - LANES:]
    gval = -jnp.exp(alog[...]) * _softplus(raw + dtb[...])
    gcum = _dot_sel_l(_chunk_cumsum_matrix(tm), gval, 3)
    lane = _iota2(raw.shape, 1)
    is_g = (lane >= GDN_G_LANE) & (lane < GDN_G_LANE + GDN_HEADS)
    is_b = (lane >= GDN_B_LANE) & (lane < GDN_B_LANE + GDN_HEADS)
    gb_o[...] = jnp.where(is_g, gcum, jnp.where(is_b, _sigmoid(raw), 0.0))


def _gdn_prep(p, psm, wts, seq):
    m = p.shape[0]
    tm = TM_PREP
    tile = lambda cb: pl.BlockSpec((tm, MIX), lambda i, cb=cb: (i, cb))
    halo = lambda cb: pl.BlockSpec((16, MIX), lambda i, cb=cb: (jnp.maximum(i * (tm // 16) - 1, 0), cb))
    cw = pl.BlockSpec((4, MIX), lambda i: (0, 0))
    lv = pl.BlockSpec((1, LANES), lambda i: (0, 0))
    out = pl.BlockSpec((tm, MIX), lambda i: (i, 0))
    return pl.pallas_call(
        functools.partial(_gdn_prep_kernel, seq=seq, tm=tm),
        grid=(m // tm,),
        in_specs=[tile(4), tile(5), tile(6), halo(4), halo(5), halo(6),
                  pl.BlockSpec((tm, N_SMALL), lambda i: (i, 0)), cw, cw, cw, lv, lv],
        out_specs=[out, out, out, pl.BlockSpec((tm, LANES), lambda i: (i, 0))],
        out_shape=[jax.ShapeDtypeStruct((m, MIX), BF16)] * 3 + [jax.ShapeDtypeStruct((m, LANES), F32)],
        compiler_params=_params("parallel"),
        name="gdn_prep",
    )(p, p, p, p, p, p, psm, wts["cw_q"], wts["cw_k"], wts["cw_v"], wts["alog"], wts["dtb"])


def _gdn_core_kernel(q_ref, k_ref, v_ref, gb_ref, z_ref, nw_ref, o_ref, s_ref):
    @pl.when(pl.program_id(1) == 0)
    def _():
        s_ref[...] = jnp.zeros_like(s_ref)

    nb, rb, width = q_ref.shape
    nh = width // LANES
    units = [(b, hh) for hh in range(nh) for b in range(nb)]
    m = _block_masks(LANES)
    ones = jnp.ones((LANES, LANES), BF16)
    cat0 = lambda *xs: jnp.concatenate(xs, axis=0)
    cat1 = lambda *xs: jnp.concatenate(xs, axis=1)

    def lane_bcast(x, lane):
        return jnp.broadcast_to(pltpu.roll(x, LANES - lane, 1)[:, 0:1], x.shape)

    nch = rb // CHUNK
    q_t, y0, phi, psi = {}, {}, {}, {}

    def tile_terms(group):
        gf, ks, lhs, rhs, e_g, qs, gamma = [], [], [], [], [], [], []
        for b, hh in group:
            hl = slice(hh * LANES, (hh + 1) * LANES)
            head = nh * pl.program_id(0) + hh
            gbv = gb_ref[b]
            g = lane_bcast(gbv, GDN_G_LANE + head)
            beta = lane_bcast(gbv, GDN_B_LANE + head)
            q, k, v = (r[b, :, hl].astype(F32) for r in (q_ref, k_ref, v_ref))
            diff = g - g.T
            gamma.append(jnp.where(m["incl"], jnp.exp(jnp.where(m["incl"], diff, 0.0)), 0.0))
            eg = jnp.exp(g)
            kb = k * beta
            gf.append(g)
            ks.append(k)
            qs.append(q)
            e_g.append(eg)
            lhs.append(cat0(kb, q))
            rhs.append(cat1(v * beta, kb * eg))
            yield
        kq = [_dot_nt(x, k) for x, k in zip(lhs, ks)]
        yield
        a_neg = [-jnp.where(m["strict"], x[:LANES] * gm, 0.0) for x, gm in zip(kq, gamma)]
        a_qk = [x[LANES:] * gm for x, gm in zip(kq, gamma)]
        k_bar_t = [[(k[c * CHUNK:(c + 1) * CHUNK]
                     * jnp.exp(g[(c + 1) * CHUNK - 1:(c + 1) * CHUNK] - g[c * CHUNK:(c + 1) * CHUNK])).T
                    for c in range(nch)] for k, g in zip(ks, gf)]
        t_inv = yield from _tri_inv(a_neg, m)
        uw = [_dot(t, x) for t, x in zip(t_inv, rhs)]
        yield
        x3 = [_dot(a, x) for a, x in zip(a_qk, uw)]
        for i, u in enumerate(group):
            for c in range(nch):
                glast = gf[i][(c + 1) * CHUNK - 1:(c + 1) * CHUNK]
                pp = _dot(k_bar_t[i][c], uw[i][c * CHUNK:(c + 1) * CHUNK])
                phi[(u, c)] = m["eye"] * jnp.exp(glast) - pp[:, LANES:]
                psi[(u, c)] = pp[:, :LANES]
        yield
        for i, u in enumerate(group):
            y0[u] = x3[i][:, :LANES]
            q_t[u] = qs[i] * e_g[i] - x3[i][:, LANES:]

    _interleave([tile_terms(units[i:i + CORE_GROUP]) for i in range(0, len(units), CORE_GROUP)],
                CORE_STAGGER)

    ys = {u: [] for u in units}
    st = {(b, hh): s_ref[hh, b] for b, hh in units}
    for c in range(nch):
        cs = slice(c * CHUNK, (c + 1) * CHUNK)
        for u in units:
            qs_ = _dot(cat0(q_t[u][cs], phi[(u, c)]), st[u])
            ys[u].append(y0[u][cs] + qs_[:CHUNK])
            st[u] = qs_[CHUNK:] + psi[(u, c)]
    for b, hh in units:
        s_ref[hh, b] = st[(b, hh)]

    for hh in range(nh):
        hl = slice(hh * LANES, (hh + 1) * LANES)
        y = cat0(*[yc for b in range(nb) for yc in ys[(b, hh)]])
        ms = _dot(y * y, ones) * (1.0 / GDN_HEAD)
        out = y * lax.rsqrt(ms + GDN_EPS) * nw_ref[...]
        z = z_ref[:, :, hl].astype(F32).reshape(nb * rb, LANES)
        o_ref[:, :, hl] = (out * _silu(z)).astype(BF16).reshape(nb, rb, LANES)


def _gdn_core(q, k, v, gb, p, norm_w, batch, seq):
    rb, nh = RB_GDN, GDN_HEADS_PER_STEP
    width = nh * LANES
    tile = pl.BlockSpec((batch, rb, width), lambda h, j: (0, j, h))
    z_col0 = 7 * (MIX // width)
    r3 = lambda a: a.reshape(batch, seq, a.shape[-1])
    out = pl.pallas_call(
        _gdn_core_kernel,
        grid=(GDN_HEADS // nh, seq // rb),
        in_specs=[tile, tile, tile,
                  pl.BlockSpec((batch, rb, LANES), lambda h, j: (0, j, 0)),
                  pl.BlockSpec((batch, rb, width), lambda h, j: (0, j, z_col0 + h)),
                  pl.BlockSpec((1, LANES), lambda h, j: (0, 0))],
        out_specs=tile,
        out_shape=jax.ShapeDtypeStruct((batch, seq, MIX), BF16),
        scratch_shapes=[pltpu.VMEM((nh, batch, LANES, LANES), F32)],
        compiler_params=_params("parallel", "arbitrary"),
        name="gdn_core",
    )(r3(q), r3(k), r3(v), r3(gb), r3(p), norm_w.reshape(1, LANES))
    return out.reshape(batch * seq, MIX)


def _merge_kernel(ya, yb, yc, wa, wb, wc, ga, gb, gc, o_ref):
    dot = lambda y, w: jnp.dot(y[...], w[0], preferred_element_type=F32)
    acc = ga[...].astype(F32) * dot(ya, wa)
    acc = acc + gb[...].astype(F32) * dot(yb, wb)
    acc = acc + gc[...].astype(F32) * dot(yc, wc)
    o_ref[...] = acc.astype(BF16)


def _merge(ya, yb, yc, wa, wb, wc, l, gates):
    m = ya.shape[0]
    d = wa.shape[2]
    tm = TM_LN
    y = pl.BlockSpec((tm, MIX), lambda i: (i, 0))
    w = pl.BlockSpec((1, MIX, d), lambda i: (l, 0, 0), pipeline_mode=pl.Buffered(1))
    g = lambda br: pl.BlockSpec((tm, d), lambda i, br=br: (i, br))
    return pl.pallas_call(
        _merge_kernel,
        grid=(m // tm,),
        in_specs=[y, y, y, w, w, w, g(0), g(1), g(2)],
        out_specs=pl.BlockSpec((tm, d), lambda i: (i, 0)),
        out_shape=jax.ShapeDtypeStruct((m, d), BF16),
        compiler_params=_params("parallel"),
        name="merge",
    )(ya, yb, yc, wa, wb, wc, gates, gates, gates)


def _proj_ln_kernel(*refs, emit_h, sub):
    a_ref, w_ref, x_ref, gt_ref, lnw_ref, lnb_ref = refs[:6]
    if emit_h:
        sc_ref, sh_ref, xo_ref, ho_ref = refs[6:]
    else:
        (xo_ref,) = refs[6:]
    nsub = a_ref.shape[0] // sub
    w = w_ref[0]
    dot = lambda r: jnp.dot(a_ref[r * sub:(r + 1) * sub, :], w, preferred_element_type=F32)
    y_next = dot(0)
    for r in range(nsub):
        y = y_next
        if r + 1 < nsub:
            y_next = dot(r + 1)
        rows = slice(r * sub, (r + 1) * sub)
        xn = _resid_ln(x_ref[rows, :], y, gt_ref[0], lnw_ref[...], lnb_ref[...])
        xo_ref[rows, :] = xn
        if emit_h:
            ho_ref[rows, :] = (xn * (1.0 + sc_ref[0]) + sh_ref[0]).astype(BF16)


def _proj_ln(a, w, l, x, gt, lnw, lnb, nxt, seq, *, tm, sub, name):
    m, k = a.shape
    d = w.shape[2]
    per = seq // tm
    emit_h = nxt is not None
    row = pl.BlockSpec((tm, d), lambda i: (i, 0))
    mod = pl.BlockSpec((1, 1, d), lambda i: (i // per, 0, 0))
    vec = pl.BlockSpec((1, d), lambda i: (0, 0))
    in_specs = [pl.BlockSpec((tm, k), lambda i: (i, 0)),
                pl.BlockSpec((1, k, d), lambda i: (l, 0, 0), pipeline_mode=pl.Buffered(1)),
                row, mod, vec, vec]
    args = [a, w, x, gt, lnw.reshape(1, d), lnb.reshape(1, d)]
    out_specs, out_shape = [row], [jax.ShapeDtypeStruct((m, d), F32)]
    if emit_h:
        in_specs += [mod, mod]
        args += list(nxt)
        out_specs.append(row)
        out_shape.append(jax.ShapeDtypeStruct((m, d), BF16))
    res = pl.pallas_call(
        functools.partial(_proj_ln_kernel, emit_h=emit_h, sub=sub),
        grid=(m // tm,),
        in_specs=in_specs,
        out_specs=out_specs,
        out_shape=out_shape,
        compiler_params=_params("parallel"),
        name=name,
    )(*args)
    return (res[0], res[1]) if emit_h else (res[0], None)


def _swiglu_kernel(a_ref, wg_ref, wu_ref, o_ref):
    a = a_ref[...]
    g = jnp.dot(a, wg_ref[0], preferred_element_type=F32)
    u = jnp.dot(a, wu_ref[0], preferred_element_type=F32)
    o_ref[...] = (_silu(g) * u).astype(BF16)


def _swiglu(h, w_up, l):
    m, k = h.shape
    dff = w_up.shape[2] // 2
    tm, tn = TM_MM, 1408
    nj = dff // tn
    return pl.pallas_call(
        _swiglu_kernel,
        grid=(m // tm, nj),
        in_specs=[pl.BlockSpec((tm, k), lambda i, j: (i, 0)),
                  pl.BlockSpec((1, k, tn), lambda i, j: (l, 0, j)),
                  pl.BlockSpec((1, k, tn), lambda i, j: (l, 0, nj + j))],
        out_specs=pl.BlockSpec((tm, tn), lambda i, j: (i, j)),
        out_shape=jax.ShapeDtypeStruct((m, dff), BF16),
        compiler_params=_params("parallel", "parallel"),
        name="ffn_up_swiglu",
    )(h, w_up, w_up)


def _layer_weights(l, w_in_t, rwkv_mu, rwkv_w0, rwkv_w2, rwkv_a0, rwkv_a2, rwkv_g2, rwkv_k_k, rwkv_k_a,
                   rwkv_r_k, rwkv_ln_w, rwkv_ln_b, rwkv_v0, rwkv_v1, rwkv_v2, gdn_conv_w, gdn_a_log,
                   gdn_dt_bias):
    c = MIX
    wl = w_in_t[l]
    o_rwkv = c
    o_lora = o_rwkv + 3 * c
    n_lora = rwkv_w2.shape[1] + rwkv_a2.shape[1] + rwkv_g2.shape[1]
    o_gdn = o_lora + n_lora
    o_ab = o_gdn + 4 * c
    o_gate = o_ab + 2 * GDN_HEADS
    pad_rows = N_SMALL - n_lora - 2 * GDN_HEADS
    w_main = jnp.concatenate([wl[:o_lora], wl[o_gdn:o_ab]], axis=0)
    assert n_lora - (N_SMALL - LANES) == GDN_G_LANE and GDN_B_LANE == GDN_G_LANE + GDN_HEADS
    w_small = jnp.concatenate([wl[o_lora:o_gdn], wl[o_ab:o_gate],
                               jnp.zeros((pad_rows, wl.shape[1]), wl.dtype)], axis=0)
    w_gate = wl[o_gate:]

    mu = rwkv_mu[l]
    row = lambda v: v.reshape(1, -1)
    n_w, n_a, n_g = rwkv_w2.shape[1], rwkv_a2.shape[1], rwkv_g2.shape[1]
    padrows = lambda w, lo: jnp.zeros((N_SMALL, c), F32).at[lo:lo + w.shape[0]].set(w).astype(BF16)
    lane_vec = lambda v, lo: jnp.zeros((1, LANES), F32).at[0, lo:lo + v.shape[0]].set(v)
    rw = {
        "mu_r": row(mu[:c]), "mu_k": row(mu[c:2 * c]), "mu_v": row(mu[2 * c:3 * c]),
        "mu_s": jnp.zeros((1, N_SMALL), F32).at[0, :n_lora].set(mu[3 * c:]),
        "w0": row(rwkv_w0[l]), "a0": row(rwkv_a0[l]), "k_k": row(rwkv_k_k[l]), "k_a": row(rwkv_k_a[l]),
        "w2p": padrows(rwkv_w2[l], 0), "a2p": padrows(rwkv_a2[l], n_w), "g2p": padrows(rwkv_g2[l], n_w + n_a),
        "r_k": row(rwkv_r_k[l]), "ln_w": row(rwkv_ln_w[l]), "ln_b": row(rwkv_ln_b[l]),
    }
    if l > 0:
        nv = rwkv_v1.shape[2]
        rw["v0"] = row(rwkv_v0[l - 1])
        rw["v1p"] = jnp.zeros((c, LANES), F32).at[:, :nv].set(rwkv_v1[l - 1]).astype(BF16)
        rw["v2p"] = jnp.zeros((LANES, c), F32).at[:nv].set(rwkv_v2[l - 1]).astype(BF16)
    cw = gdn_conv_w[l]
    gd = {"cw_q": cw[:, :c], "cw_k": cw[:, c:2 * c], "cw_v": cw[:, 2 * c:],
          "alog": lane_vec(gdn_a_log[l], GDN_G_LANE), "dtb": lane_vec(gdn_dt_bias[l], GDN_G_LANE)}
    return w_main, w_small, w_gate, rw, gd


def kernel(x, c, ada_w, ada_b, w_in, pool_w, pool_scale, rwkv_mu, rwkv_w0, rwkv_w2, rwkv_a0, rwkv_a2, rwkv_g2, rwkv_k_k, rwkv_k_a, rwkv_r_k, rwkv_ln_w, rwkv_ln_b, rwkv_v0, rwkv_v1, rwkv_v2, gdn_conv_w, gdn_a_log, gdn_dt_bias, gdn_norm_w, w_branch_a, w_branch_b, w_branch_c, w_out, ln1_w, ln1_b, ffn_w_up, ffn_w_down, ln2_w, ln2_b):
    batch, seq, d = x.shape
    assert seq % TM_MM == 0 and d == 2 * MIX
    m = batch * seq
    xf = x.reshape(m, d)

    mod = _ada(c, ada_w, ada_b)
    mods = [[mod[l, :, i * d:(i + 1) * d].reshape(batch, 1, d) for i in range(6)]
            for l in range(DEPTH)]

    h = _modulate(xf, mods[0][1], mods[0][0], seq)
    w_in_t = jnp.swapaxes(w_in, 1, 2).astype(BF16)
    wa, wb, wc = (w.astype(BF16) for w in (w_branch_a, w_branch_b, w_branch_c))
    w_out_b, w_up_b, w_down_b = (w.astype(BF16) for w in (w_out, ffn_w_up, ffn_w_down))
    v_first = None
    for l in range(DEPTH):
        sh_m, sc_m, gt_m, sh_f, sc_f, gt_f = mods[l]
        w_main, w_small, w_gate, rw, gd = _layer_weights(
            l, w_in_t, rwkv_mu, rwkv_w0, rwkv_w2, rwkv_a0, rwkv_a2, rwkv_g2, rwkv_k_k, rwkv_k_a,
            rwkv_r_k, rwkv_ln_w, rwkv_ln_b, rwkv_v0, rwkv_v1, rwkv_v2, gdn_conv_w, gdn_a_log,
            gdn_dt_bias)

        p = _matmul_nt(h, w_main, tn=2048, out_dtype=BF16, name="in_proj_main")
        psm = _matmul_nt(h, w_small, tn=N_SMALL, out_dtype=F32, name="in_proj_small")
        gates = _matmul_nt(h, w_gate, tn=2048, out_dtype=BF16, act="sigmoid", name="in_proj_gates")

        y_a = _pool(p, pool_w[l], pool_scale[l], seq)
        prep = _rwkv_prep(p, psm, rw, v_first, seq)
        if l == 0:
            v_first = prep[2]
        y_b = _rwkv_core(prep, rw, batch, seq)
        gq, gk, gv, ggb = _gdn_prep(p, psm, gd, seq)
        y_c = _gdn_core(gq, gk, gv, ggb, p, gdn_norm_w[l], batch, seq)

        merged = _merge(y_a, y_b, y_c, wa, wb, wc, l, gates)
        xf, h = _proj_ln(merged, w_out_b, l, xf, gt_m, ln1_w[l], ln1_b[l],
                         (sc_f, sh_f), seq, tm=512, sub=256, name="out_proj_ln")

        act = _swiglu(h, w_up_b, l)
        nxt = (mods[l + 1][1], mods[l + 1][0]) if l + 1 < DEPTH else None
        xf, h = _proj_ln(act, w_down_b, l, xf, gt_f, ln2_w[l], ln2_b[l],
                         nxt, seq, tm=256, sub=256, name="ffn_down_ln")
    return xf.reshape(batch, seq, d)
```

```python
import functools

import jax
import jax.numpy as jnp
from jax import lax
from jax.experimental import pallas as pl
from jax.experimental.pallas import tpu as pltpu

F32 = jnp.float32
BF16 = jnp.bfloat16

DEPTH = 2
DN_ALPHA = (2 * DEPTH) ** 0.25
LN_EPS = 1e-5
MIX = 1024
POOL_WINDOWS = (2, 4, 8, 16)
POOL_GROUP = 256
RWKV_HEAD = 64
RWKV_LN_EPS = RWKV_HEAD * 1e-5
GDN_HEAD = 128
GDN_HEADS = 8
GDN_EPS = 1e-6
CHUNK = 64
LANES = 128
N_SMALL = 384
GDN_G_LANE = 32
GDN_B_LANE = 40
VMEM_LIMIT = 56 * 1024 * 1024

TM_MM = 1024
TM_LN = 512
TM_PREP = 256
RB_RWKV = 256
RB_GDN = 128
GDN_HEADS_PER_STEP = 4
CORE_GROUP = 16
CORE_STAGGER = 1


def _dot(a, b):
    return jnp.dot(a.astype(BF16), b.astype(BF16), preferred_element_type=F32)


def _dot_nt(a, b):
    return lax.dot_general(a.astype(BF16), b.astype(BF16), (((1,), (1,)), ((), ())),
                           preferred_element_type=F32)


def _split(a, n):
    terms, rest = [], a
    for _ in range(n):
        t = rest.astype(BF16)
        terms.append(t)
        rest = rest - t.astype(F32)
    return terms


def _dot3(a, b, nt=False):
    d = _dot_nt if nt else _dot
    a1, a2 = _split(a, 2)
    b1, b2 = _split(b, 2)
    return d(a1, b1) + (d(a1, b2) + d(a2, b1))


def _dot_sel_l(sel, b, n):
    out = None
    for t in _split(b, n):
        p = jnp.dot(sel, t, preferred_element_type=F32)
        out = p if out is None else out + p
    return out


def _sigmoid(x):
    return 0.5 * jnp.tanh(0.5 * x) + 0.5


def _silu(x):
    return x * _sigmoid(x)


def _softplus(x):
    return jnp.maximum(x, 0.0) + jnp.log(1.0 + jnp.exp(-jnp.abs(x)))


def _iota2(shape, dim):
    return lax.broadcasted_iota(jnp.int32, shape, dim)


def _block_masks(n):
    r = _iota2((n, n), 0)
    c = _iota2((n, n), 1)
    same = lambda s: (r >> s) == (c >> s)
    m = {
        "eye": (r == c).astype(F32),
        "bd8": same(3),
        "off16": same(4) & jnp.logical_not(same(3)),
        "off32": same(5) & jnp.logical_not(same(4)),
        "off64": same(6) & jnp.logical_not(same(5)),
        "bd64": same(6),
    }
    m["strict"] = m["bd64"] & (r > c)
    m["incl"] = m["bd64"] & (r >= c)
    return m


def _chunk_cumsum_matrix(n):
    r = _iota2((n, n), 0)
    c = _iota2((n, n), 1)
    return (((r >> 6) == (c >> 6)) & (c <= r)).astype(BF16)


def _tri_inv(mats, m):
    a8 = [jnp.where(m["bd8"], a, 0.0) for a in mats]
    s = [m["eye"] + x for x in a8]
    p = [_dot(x, x) for x in a8]
    yield
    q = [_dot(pi, jnp.concatenate([si, pi], axis=1)) for pi, si in zip(p, s)]
    yield
    s = [si + qi[:, :LANES] for si, qi in zip(s, q)]
    s = [si + _dot(qi[:, LANES:], si) for si, qi in zip(s, q)]
    yield
    for key in ("off16", "off32", "off64"):
        t = [_dot(jnp.where(m[key], a, 0.0), si) for a, si in zip(mats, s)]
        yield
        s = [si + _dot(si, ti) for si, ti in zip(s, t)]
        yield
    return s


def _interleave(gens, stagger):
    active, pending, tick = [], list(gens), 0
    while active or pending:
        if pending and tick % stagger == 0:
            active.append(pending.pop(0))
        tick += 1
        for g in list(active):
            try:
                next(g)
            except StopIteration:
                active.remove(g)


def _shift_rows(x, halo, s):
    halo = halo[halo.shape[0] - 8:]
    xr = pltpu.roll(x, s, 0)
    hr = pltpu.roll(halo, s, 0)
    rows = _iota2(halo.shape, 0)
    top = jnp.where(rows < s, hr, xr[:8])
    return top if x.shape[0] == 8 else jnp.concatenate([top, xr[8:]], axis=0)


def _shift_rows_bf16(x, halo, shifts):
    tm = x.shape[0]
    r = _iota2((tm, tm), 0)
    c = _iota2((tm, tm), 1)
    sel = jnp.concatenate([(r - c == s).astype(BF16) for s in shifts], axis=0)
    moved = jnp.dot(sel, x, preferred_element_type=F32)
    x_top = x[:16].astype(F32)[:8]
    outs = []
    for j, s in enumerate(shifts):
        top = _shift_rows(x_top, halo, s)
        outs.append(jnp.concatenate([top, moved[j * tm + 8:(j + 1) * tm]], axis=0))
    return outs


def _resid_ln(x, y, gt, lnw, lnb):
    z = DN_ALPHA * x + (1.0 + gt) * y
    mu = jnp.mean(z, axis=-1, keepdims=True)
    zc = z - mu
    var = jnp.mean(zc * zc, axis=-1, keepdims=True)
    return zc * lax.rsqrt(var + LN_EPS) * lnw + lnb


def _params(*sem):
    return pltpu.CompilerParams(dimension_semantics=sem, vmem_limit_bytes=VMEM_LIMIT)


def _ada_kernel(c_ref, w_ref, b_ref, o_ref):
    c = c_ref[...]
    o_ref[0] = _dot3(_silu(c), w_ref[0]) + b_ref[0]


def _ada(c, ada_w, ada_b):
    nl, d, n = ada_w.shape
    b = c.shape[0]
    tn = 512
    cp = jnp.zeros((8, d), F32).at[:b].set(c)
    out = pl.pallas_call(
        _ada_kernel,
        grid=(nl, n // tn),
        in_specs=[pl.BlockSpec((8, d), lambda l, j: (0, 0)),
                  pl.BlockSpec((1, d, tn), lambda l, j: (l, 0, j)),
                  pl.BlockSpec((1, 1, tn), lambda l, j: (l, 0, j))],
        out_specs=pl.BlockSpec((1, 8, tn), lambda l, j: (l, 0, j)),
        out_shape=jax.ShapeDtypeStruct((nl, 8, n), F32),
        compiler_params=_params("parallel", "parallel"),
        name="ada_mod",
    )(cp, ada_w, ada_b.reshape(nl, 1, n))
    return out[:, :b]


def _modulate_kernel(x_ref, sc_ref, sh_ref, h_ref):
    h_ref[...] = (x_ref[...] * (1.0 + sc_ref[0]) + sh_ref[0]).astype(BF16)


def _modulate(x, sc, sh, seq):
    m, d = x.shape
    tm = TM_LN
    per = seq // tm
    mod = pl.BlockSpec((1, 1, d), lambda i: (i // per, 0, 0))
    return pl.pallas_call(
        _modulate_kernel,
        grid=(m // tm,),
        in_specs=[pl.BlockSpec((tm, d), lambda i: (i, 0)), mod, mod],
        out_specs=pl.BlockSpec((tm, d), lambda i: (i, 0)),
        out_shape=jax.ShapeDtypeStruct((m, d), BF16),
        compiler_params=_params("parallel"),
        name="modulate",
    )(x, sc, sh)


def _mm_kernel(a_ref, bt_ref, o_ref, *, act):
    acc = lax.dot_general(a_ref[...], bt_ref[...], (((1,), (1,)), ((), ())),
                          preferred_element_type=F32)
    if act == "sigmoid":
        acc = _sigmoid(acc)
    o_ref[...] = acc.astype(o_ref.dtype)


def _matmul_nt(a, bt, *, tn, out_dtype, act=None, name):
    m, k = a.shape
    n = bt.shape[0]
    tm = TM_MM
    return pl.pallas_call(
        functools.partial(_mm_kernel, act=act),
        grid=(m // tm, n // tn),
        in_specs=[pl.BlockSpec((tm, k), lambda i, j: (i, 0)),
                  pl.BlockSpec((tn, k), lambda i, j: (j, 0))],
        out_specs=pl.BlockSpec((tm, tn), lambda i, j: (i, j)),
        out_shape=jax.ShapeDtypeStruct((m, n), out_dtype),
        compiler_params=_params("parallel", "parallel"),
        name=name,
    )(a, bt)


def _pool_kernel(p_ref, halo_ref, w_ref, scale_ref, o_ref, *, seq, tm):
    t0 = (pl.program_id(0) * tm) % seq
    x = p_ref[...].astype(F32)
    halo = jnp.where(t0 == 0, 0.0, halo_ref[...].astype(F32))
    ext = jnp.concatenate([halo, x], axis=0)
    sums = [ext]
    for sh in (1, 2, 4, 8):
        sums.append(sums[-1] + pltpu.roll(sums[-1], sh, 0))
    t = t0 + _iota2((tm, POOL_GROUP), 0)
    ys = []
    for g, win in enumerate(POOL_WINDOWS):
        cols = slice(g * POOL_GROUP, (g + 1) * POOL_GROUP)
        cnt = jnp.minimum(t + 1, win).astype(F32)
        pooled = sums[g + 1][16:, cols] / cnt - x[:, cols]
        ys.append(_dot(pooled, w_ref[g]))
    o_ref[...] = (jnp.concatenate(ys, axis=1) * scale_ref[...]).astype(BF16)


def _pool(p, pool_w, pool_scale, seq):
    m = p.shape[0]
    tm = TM_LN
    return pl.pallas_call(
        functools.partial(_pool_kernel, seq=seq, tm=tm),
        grid=(m // tm,),
        in_specs=[pl.BlockSpec((tm, MIX), lambda i: (i, 0)),
                  pl.BlockSpec((16, MIX), lambda i: (jnp.maximum(i * (tm // 16) - 1, 0), 0)),
                  pl.BlockSpec(pool_w.shape, lambda i: (0, 0, 0)),
                  pl.BlockSpec((1, MIX), lambda i: (0, 0))],
        out_specs=pl.BlockSpec((tm, MIX), lambda i: (i, 0)),
        out_shape=jax.ShapeDtypeStruct((m, MIX), BF16),
        compiler_params=_params("parallel"),
        name="pool_mixer",
    )(p, p, pool_w.astype(BF16), pool_scale.reshape(1, MIX))


def _rwkv_prep_kernel(*refs, seq, tm, has_vres):
    (pr, pk, pv, hr, hk, hv, ps, hs, mur, muk, muv, mus, w0, a0, kkp, kap,
     w2p, a2p, g2p) = refs[:19]
    if has_vres:
        vf, v0, v1p, v2p = refs[19:23]
        outs = refs[23:]
    else:
        outs = refs[19:]
    r_o, k_o, v_o, kk_o, a_o, ld_o, g_o = outs

    first = (pl.program_id(0) * tm) % seq == 0

    def lerp(p_ref, h_ref, mu_ref):
        halo = jnp.where(first, 0.0, h_ref[...].astype(F32))
        if p_ref.dtype == BF16:
            (prev,) = _shift_rows_bf16(p_ref[...], halo, (1,))
        else:
            prev = _shift_rows(p_ref[...], halo, 1)
        p = p_ref[...].astype(F32)
        return p + (prev - p) * mu_ref[...]

    xr = lerp(pr, hr, mur)
    xk = lerp(pk, hk, muk)
    xv = lerp(pv, hv, muv)
    xs = lerp(ps, hs, mus)

    z = w0[...] + _dot(jnp.tanh(xs), w2p[...])
    w = jnp.minimum(z, 0.0) - jnp.log(1.0 + jnp.exp(-jnp.abs(z))) - 0.5
    a = _sigmoid(a0[...] + _dot(xs, a2p[...]))
    g = _dot(_sigmoid(xs), g2p[...])
    if has_vres:
        gate = _sigmoid(v0[...] + _dot(_dot(xv, v1p[...]), v2p[...]))
        xv = xv + (vf[...].astype(F32) - xv) * gate

    kk = xk * kkp[...]
    r2 = _iota2((LANES, LANES), 0)
    c2 = _iota2((LANES, LANES), 1)
    head_ones = ((r2 >> 6) == (c2 >> 6)).astype(BF16)
    kk2 = kk * kk
    ss = jnp.concatenate(
        [_dot(kk2[:, i * LANES:(i + 1) * LANES], head_ones) for i in range(MIX // LANES)],
        axis=1)
    kk = kk * lax.rsqrt(jnp.maximum(ss, 1e-24))

    r_o[...] = xr.astype(r_o.dtype)
    k_o[...] = (xk * (1.0 + (a - 1.0) * kap[...])).astype(k_o.dtype)
    v_o[...] = xv.astype(v_o.dtype)
    kk_o[...] = kk.astype(kk_o.dtype)
    a_o[...] = a.astype(a_o.dtype)
    ld_o[...] = _dot_sel_l(_chunk_cumsum_matrix(tm), -jnp.exp(w), 2)
    g_o[...] = g.astype(g_o.dtype)


def _rwkv_prep(p, psm, wts, v_first, seq):
    m = p.shape[0]
    tm = TM_PREP
    has_vres = v_first is not None
    tile = lambda cb: pl.BlockSpec((tm, MIX), lambda i, cb=cb: (i, cb))
    halo = lambda cb: pl.BlockSpec((16, MIX), lambda i, cb=cb: (jnp.maximum(i * (tm // 16) - 1, 0), cb))
    vec = pl.BlockSpec((1, MIX), lambda i: (0, 0))
    full = lambda arr: pl.BlockSpec(arr.shape, lambda i: (0,) * arr.ndim)
    in_specs = [tile(1), tile(2), tile(3), halo(1), halo(2), halo(3),
                pl.BlockSpec((tm, N_SMALL), lambda i: (i, 0)),
                pl.BlockSpec((8, N_SMALL), lambda i: (jnp.maximum(i * (tm // 8) - 1, 0), 0)),
                vec, vec, vec, pl.BlockSpec((1, N_SMALL), lambda i: (0, 0)),
                vec, vec, vec, vec,
                full(wts["w2p"]), full(wts["a2p"]), full(wts["g2p"])]
    args = [p, p, p, p, p, p, psm, psm, wts["mu_r"], wts["mu_k"], wts["mu_v"], wts["mu_s"],
            wts["w0"], wts["a0"], wts["k_k"], wts["k_a"], wts["w2p"], wts["a2p"], wts["g2p"]]
    if has_vres:
        in_specs += [pl.BlockSpec((tm, MIX), lambda i: (i, 0)), vec, full(wts["v1p"]), full(wts["v2p"])]
        args += [v_first, wts["v0"], wts["v1p"], wts["v2p"]]
    out = pl.BlockSpec((tm, MIX), lambda i: (i, 0))
    return pl.pallas_call(
        functools.partial(_rwkv_prep_kernel, seq=seq, tm=tm, has_vres=has_vres),
        grid=(m // tm,),
        in_specs=in_specs,
        out_specs=[out] * 7,
        out_shape=[jax.ShapeDtypeStruct((m, MIX), dt) for dt in (BF16,) * 5 + (F32, BF16)],
        compiler_params=_params("parallel"),
        name="rwkv_prep",
    )(*args)


def _rwkv_core_kernel(r_ref, k_ref, v_ref, kk_ref, a_ref, gc_ref, g_ref, rk_ref, lnw_ref, lnb_ref,
                      o_ref, h_ref):
    @pl.when(pl.program_id(1) == 0)
    def _():
        h_ref[...] = jnp.zeros_like(h_ref)

    nb, rb, _ = r_ref.shape
    nch = rb // CHUNK
    units = [(b, c) for c in range(nch) for b in range(nb)]
    m = _block_masks(LANES)
    lane = _iota2((1, LANES), 1)
    m0 = (lane < RWKV_HEAD).astype(F32)
    m1 = 1.0 - m0
    r2 = _iota2((LANES, LANES), 0)
    c2 = _iota2((LANES, LANES), 1)
    head_ones = ((r2 >> 6) == (c2 >> 6)).astype(BF16)
    stack = lambda x: jnp.concatenate([x * m0, x * m1], axis=0)
    unstack = lambda x: x[:CHUNK] + x[CHUNK:]
    cat0 = lambda *xs: jnp.concatenate(xs, axis=0)
    cat1 = lambda *xs: jnp.concatenate(xs, axis=1)

    def load(ref, u):
        b, c = u
        return ref[b, c * CHUNK:(c + 1) * CHUNK, :].astype(F32)

    first_row = _iota2((CHUNK, LANES), 0) == 0
    q_t, y0, phi, psi = {}, {}, {}, {}

    def chunk_terms(group):
        a_st, r_st, rhs, v_st, bbar_t, kbar_t, dlast, vs = [], [], [], [], [], [], [], []
        for u in group:
            gc = load(gc_ref, u)
            g_prev = jnp.where(first_row, 0.0, pltpu.roll(gc, 1, 0))
            kc, vc, kkc = load(k_ref, u), load(v_ref, u), load(kk_ref, u)
            bc = kkc * load(a_ref, u)
            glast = gc[CHUNK - 1:CHUNK]
            e_neg = jnp.exp(-gc)
            e_rest = jnp.exp(glast - gc)
            a_st.append(stack(-kkc * jnp.exp(g_prev)))
            r_st.append(stack(load(r_ref, u) * jnp.exp(gc)))
            b_hat, k_hat = bc * e_neg, kc * e_neg
            rhs.append(cat0(b_hat, b_hat, k_hat, k_hat))
            v_st.append(stack(vc))
            bbar_t.append((bc * e_rest).T)
            kbar_t.append((kc * e_rest).T)
            dlast.append(jnp.exp(glast))
            vs.append(vc)
            yield
        big = [_dot_nt(cat0(a, r), x) for a, r, x in zip(a_st, r_st, rhs)]
        yield
        a_ab = [jnp.where(m["strict"], x[:LANES, :LANES], 0.0) for x in big]
        a_kr = [cat0(jnp.where(m["strict"], x[:LANES, LANES:], 0.0),
                     jnp.where(m["incl"], x[LANES:, LANES:], 0.0)) for x in big]
        a_rb = [jnp.where(m["incl"], x[LANES:, :LANES], 0.0) for x in big]
        x1 = [_dot(a, v) for a, v in zip(a_kr, v_st)]
        t_inv = yield from _tri_inv(a_ab, m)
        x2 = [_dot(t, cat1(a, x[:LANES])) for t, a, x in zip(t_inv, a_st, x1)]
        yield
        x3 = [_dot(a, x) for a, x in zip(a_rb, x2)]
        w_t = [unstack(x[:, :LANES]) for x in x2]
        u_t = [unstack(x[:, LANES:]) for x in x2]
        for i, u in enumerate(group):
            phi[u] = m["eye"] * dlast[i] + jnp.where(m["bd64"], _dot(bbar_t[i], w_t[i]), 0.0)
            psi[u] = jnp.where(m["bd64"], _dot(cat1(bbar_t[i], kbar_t[i]), cat0(u_t[i], vs[i])), 0.0)
        yield
        for i, u in enumerate(group):
            q_t[u] = unstack(r_st[i] + x3[i][:, :LANES])
            y0[u] = unstack(x3[i][:, LANES:] + x1[i][LANES:])

    _interleave([chunk_terms(units[i:i + CORE_GROUP]) for i in range(0, len(units), CORE_GROUP)],
                CORE_STAGGER)

    ys = {}
    hs = [h_ref[b] for b in range(nb)]
    for u in units:
        b = u[0]
        qh = _dot(cat0(q_t[u], phi[u]), hs[b])
        ys[u] = y0[u] + qh[:CHUNK]
        hs[b] = qh[CHUNK:] + psi[u]
    for b in range(nb):
        h_ref[b] = hs[b]

    y = cat0(*[ys[(b, c)] for b in range(nb) for c in range(nch)])
    flat = lambda ref: ref[...].astype(F32).reshape(nb * rb, LANES)
    rr, kk_, vv = flat(r_ref), flat(k_ref), flat(v_ref)
    sums = _dot(cat0(y, rr * kk_ * rk_ref[...]), head_ones)
    yc = y - sums[:nb * rb] * (1.0 / RWKV_HEAD)
    var = _dot(yc * yc, head_ones) * (1.0 / RWKV_HEAD)
    yn = yc * lax.rsqrt(var + RWKV_LN_EPS) * lnw_ref[...] + lnb_ref[...]
    bonus = sums[nb * rb:] * vv
    o_ref[...] = ((yn + bonus) * flat(g_ref)).astype(BF16).reshape(nb, rb, LANES)


def _rwkv_core(prep, wts, batch, seq):
    rb = RB_RWKV
    tile = pl.BlockSpec((batch, rb, LANES), lambda p, j: (0, j, p))
    vec = pl.BlockSpec((1, LANES), lambda p, j: (0, p))
    out = pl.pallas_call(
        _rwkv_core_kernel,
        grid=(MIX // LANES, seq // rb),
        in_specs=[tile] * 7 + [vec] * 3,
        out_specs=tile,
        out_shape=jax.ShapeDtypeStruct((batch, seq, MIX), BF16),
        scratch_shapes=[pltpu.VMEM((batch, LANES, LANES), F32)],
        compiler_params=_params("parallel", "arbitrary"),
        name="rwkv_core",
    )(*[a.reshape(batch, seq, MIX) for a in prep], wts["r_k"], wts["ln_w"], wts["ln_b"])
    return out.reshape(batch * seq, MIX)


def _gdn_prep_kernel(pq, pk, pv, hq, hk, hv, ps, cwq, cwk, cwv, alog, dtb,
                     q_o, k_o, v_o, gb_o, *, seq, tm):
    first = (pl.program_id(0) * tm) % seq == 0
    ones = jnp.ones((LANES, LANES), BF16)

    def conv_silu(p_ref, h_ref, w_ref):
        halo = jnp.where(first, 0.0, h_ref[...].astype(F32))
        w = w_ref[...]
        y = p_ref[...].astype(F32) * w[3:4]
        for s, xs in zip((1, 2, 3), _shift_rows_bf16(p_ref[...], halo, (1, 2, 3))):
            y = y + xs * w[3 - s:4 - s]
        return _silu(y)

    def l2norm(y, scale):
        blocks = []
        for i in range(GDN_HEADS):
            blk = y[:, i * LANES:(i + 1) * LANES]
            ss = _dot(blk * blk, ones)
            blocks.append(blk * (lax.rsqrt(ss + GDN_EPS) * scale))
        return jnp.concatenate(blocks, axis=1)

    q_o[...] = l2norm(conv_silu(pq, hq, cwq), GDN_HEAD ** -0.5).astype(q_o.dtype)
    k_o[...] = l2norm(conv_silu(pk, hk, cwk), 1.0).astype(k_o.dtype)
    v_o[...] = conv_silu(pv, hv, cwv).astype(v_o.dtype)

    raw = ps[:, N_SMALL - LANES:]
    gval = -jnp.exp(alog[...]) * _softplus(raw + dtb[...])
    gcum = _dot_sel_l(_chunk_cumsum_matrix(tm), gval, 3)
    lane = _iota2(raw.shape, 1)
    is_g = (lane >= GDN_G_LANE) & (lane < GDN_G_LANE + GDN_HEADS)
    is_b = (lane >= GDN_B_LANE) & (lane < GDN_B_LANE + GDN_HEADS)
    gb_o[...] = jnp.where(is_g, gcum, jnp.where(is_b, _sigmoid(raw), 0.0))


def _gdn_prep(p, psm, wts, seq):
    m = p.shape[0]
    tm = TM_PREP
    tile = lambda cb: pl.BlockSpec((tm, MIX), lambda i, cb=cb: (i, cb))
    halo = lambda cb: pl.BlockSpec((16, MIX), lambda i, cb=cb: (jnp.maximum(i * (tm // 16) - 1, 0), cb))
    cw = pl.BlockSpec((4, MIX), lambda i: (0, 0))
    lv = pl.BlockSpec((1, LANES), lambda i: (0, 0))
    out = pl.BlockSpec((tm, MIX), lambda i: (i, 0))
    return pl.pallas_call(
        functools.partial(_gdn_prep_kernel, seq=seq, tm=tm),
        grid=(m // tm,),
        in_specs=[tile(4), tile(5), tile(6), halo(4), halo(5), halo(6),
                  pl.BlockSpec((tm, N_SMALL), lambda i: (i, 0)), cw, cw, cw, lv, lv],
        out_specs=[out, out, out, pl.BlockSpec((tm, LANES), lambda i: (i, 0))],
        out_shape=[jax.ShapeDtypeStruct((m, MIX), BF16)] * 3 + [jax.ShapeDtypeStruct((m, LANES), F32)],
        compiler_params=_params("parallel"),
        name="gdn_prep",
    )(p, p, p, p, p, p, psm, wts["cw_q"], wts["cw_k"], wts["cw_v"], wts["alog"], wts["dtb"])


def _gdn_core_kernel(q_ref, k_ref, v_ref, gb_ref, z_ref, nw_ref, o_ref, s_ref):
    @pl.when(pl.program_id(1) == 0)
    def _():
        s_ref[...] = jnp.zeros_like(s_ref)

    nb, rb, width = q_ref.shape
    nh = width // LANES
    units = [(b, hh) for hh in range(nh) for b in range(nb)]
    m = _block_masks(LANES)
    ones = jnp.ones((LANES, LANES), BF16)
    cat0 = lambda *xs: jnp.concatenate(xs, axis=0)
    cat1 = lambda *xs: jnp.concatenate(xs, axis=1)

    def lane_bcast(x, lane):
        return jnp.broadcast_to(pltpu.roll(x, LANES - lane, 1)[:, 0:1], x.shape)

    nch = rb // CHUNK
    q_t, y0, phi, psi = {}, {}, {}, {}

    def tile_terms(group):
        gf, ks, lhs, rhs, e_g, qs, gamma = [], [], [], [], [], [], []
        for b, hh in group:
            hl = slice(hh * LANES, (hh + 1) * LANES)
            head = nh * pl.program_id(0) + hh
            gbv = gb_ref[b]
            g = lane_bcast(gbv, GDN_G_LANE + head)
            beta = lane_bcast(gbv, GDN_B_LANE + head)
            q, k, v = (r[b, :, hl].astype(F32) for r in (q_ref, k_ref, v_ref))
            diff = g - g.T
            gamma.append(jnp.where(m["incl"], jnp.exp(jnp.where(m["incl"], diff, 0.0)), 0.0))
            eg = jnp.exp(g)
            kb = k * beta
            gf.append(g)
            ks.append(k)
            qs.append(q)
            e_g.append(eg)
            lhs.append(cat0(kb, q))
            rhs.append(cat1(v * beta, kb * eg))
            yield
        kq = [_dot_nt(x, k) for x, k in zip(lhs, ks)]
        yield
        a_neg = [-jnp.where(m["strict"], x[:LANES] * gm, 0.0) for x, gm in zip(kq, gamma)]
        a_qk = [x[LANES:] * gm for x, gm in zip(kq, gamma)]
        k_bar_t = [[(k[c * CHUNK:(c + 1) * CHUNK]
                     * jnp.exp(g[(c + 1) * CHUNK - 1:(c + 1) * CHUNK] - g[c * CHUNK:(c + 1) * CHUNK])).T
                    for c in range(nch)] for k, g in zip(ks, gf)]
        t_inv = yield from _tri_inv(a_neg, m)
        uw = [_dot(t, x) for t, x in zip(t_inv, rhs)]
        yield
        x3 = [_dot(a, x) for a, x in zip(a_qk, uw)]
        for i, u in enumerate(group):
            for c in range(nch):
                glast = gf[i][(c + 1) * CHUNK - 1:(c + 1) * CHUNK]
                pp = _dot(k_bar_t[i][c], uw[i][c * CHUNK:(c + 1) * CHUNK])
                phi[(u, c)] = m["eye"] * jnp.exp(glast) - pp[:, LANES:]
                psi[(u, c)] = pp[:, :LANES]
        yield
        for i, u in enumerate(group):
            y0[u] = x3[i][:, :LANES]
            q_t[u] = qs[i] * e_g[i] - x3[i][:, LANES:]

    _interleave([tile_terms(units[i:i + CORE_GROUP]) for i in range(0, len(units), CORE_GROUP)],
                CORE_STAGGER)

    ys = {u: [] for u in units}
    st = {(b, hh): s_ref[hh, b] for b, hh in units}
    for c in range(nch):
        cs = slice(c * CHUNK, (c + 1) * CHUNK)
        for u in units:
            qs_ = _dot(cat0(q_t[u][cs], phi[(u, c)]), st[u])
            ys[u].append(y0[u][cs] + qs_[:CHUNK])
            st[u] = qs_[CHUNK:] + psi[(u, c)]
    for b, hh in units:
        s_ref[hh, b] = st[(b, hh)]

    for hh in range(nh):
        hl = slice(hh * LANES, (hh + 1) * LANES)
        y = cat0(*[yc for b in range(nb) for yc in ys[(b, hh)]])
        ms = _dot(y * y, ones) * (1.0 / GDN_HEAD)
        out = y * lax.rsqrt(ms + GDN_EPS) * nw_ref[...]
        z = z_ref[:, :, hl].astype(F32).reshape(nb * rb, LANES)
        o_ref[:, :, hl] = (out * _silu(z)).astype(BF16).reshape(nb, rb, LANES)


def _gdn_core(q, k, v, gb, p, norm_w, batch, seq):
    rb, nh = RB_GDN, GDN_HEADS_PER_STEP
    width = nh * LANES
    tile = pl.BlockSpec((batch, rb, width), lambda h, j: (0, j, h))
    z_col0 = 7 * (MIX // width)
    r3 = lambda a: a.reshape(batch, seq, a.shape[-1])
    out = pl.pallas_call(
        _gdn_core_kernel,
        grid=(GDN_HEADS // nh, seq // rb),
        in_specs=[tile, tile, tile,
                  pl.BlockSpec((batch, rb, LANES), lambda h, j: (0, j, 0)),
                  pl.BlockSpec((batch, rb, width), lambda h, j: (0, j, z_col0 + h)),
                  pl.BlockSpec((1, LANES), lambda h, j: (0, 0))],
        out_specs=tile,
        out_shape=jax.ShapeDtypeStruct((batch, seq, MIX), BF16),
        scratch_shapes=[pltpu.VMEM((nh, batch, LANES, LANES), F32)],
        compiler_params=_params("parallel", "arbitrary"),
        name="gdn_core",
    )(r3(q), r3(k), r3(v), r3(gb), r3(p), norm_w.reshape(1, LANES))
    return out.reshape(batch * seq, MIX)


def _merge_kernel(ya, yb, yc, wa, wb, wc, ga, gb, gc, o_ref):
    dot = lambda y, w: jnp.dot(y[...], w[0], preferred_element_type=F32)
    acc = ga[...].astype(F32) * dot(ya, wa)
    acc = acc + gb[...].astype(F32) * dot(yb, wb)
    acc = acc + gc[...].astype(F32) * dot(yc, wc)
    o_ref[...] = acc.astype(BF16)


def _merge(ya, yb, yc, wa, wb, wc, l, gates):
    m = ya.shape[0]
    d = wa.shape[2]
    tm = TM_LN
    y = pl.BlockSpec((tm, MIX), lambda i: (i, 0))
    w = pl.BlockSpec((1, MIX, d), lambda i: (l, 0, 0), pipeline_mode=pl.Buffered(1))
    g = lambda br: pl.BlockSpec((tm, d), lambda i, br=br: (i, br))
    return pl.pallas_call(
        _merge_kernel,
        grid=(m // tm,),
        in_specs=[y, y, y, w, w, w, g(0), g(1), g(2)],
        out_specs=pl.BlockSpec((tm, d), lambda i: (i, 0)),
        out_shape=jax.ShapeDtypeStruct((m, d), BF16),
        compiler_params=_params("parallel"),
        name="merge",
    )(ya, yb, yc, wa, wb, wc, gates, gates, gates)


def _proj_ln_kernel(*refs, emit_h, sub):
    a_ref, w_ref, x_ref, gt_ref, lnw_ref, lnb_ref = refs[:6]
    if emit_h:
        sc_ref, sh_ref, xo_ref, ho_ref = refs[6:]
    else:
        (xo_ref,) = refs[6:]
    nsub = a_ref.shape[0] // sub
    w = w_ref[0]
    dot = lambda r: jnp.dot(a_ref[r * sub:(r + 1) * sub, :], w, preferred_element_type=F32)
    y_next = dot(0)
    for r in range(nsub):
        y = y_next
        if r + 1 < nsub:
            y_next = dot(r + 1)
        rows = slice(r * sub, (r + 1) * sub)
        xn = _resid_ln(x_ref[rows, :], y, gt_ref[0], lnw_ref[...], lnb_ref[...])
        xo_ref[rows, :] = xn
        if emit_h:
            ho_ref[rows, :] = (xn * (1.0 + sc_ref[0]) + sh_ref[0]).astype(BF16)


def _proj_ln(a, w, l, x, gt, lnw, lnb, nxt, seq, *, tm, sub, name):
    m, k = a.shape
    d = w.shape[2]
    per = seq // tm
    emit_h = nxt is not None
    row = pl.BlockSpec((tm, d), lambda i: (i, 0))
    mod = pl.BlockSpec((1, 1, d), lambda i: (i // per, 0, 0))
    vec = pl.BlockSpec((1, d), lambda i: (0, 0))
    in_specs = [pl.BlockSpec((tm, k), lambda i: (i, 0)),
                pl.BlockSpec((1, k, d), lambda i: (l, 0, 0), pipeline_mode=pl.Buffered(1)),
                row, mod, vec, vec]
    args = [a, w, x, gt, lnw.reshape(1, d), lnb.reshape(1, d)]
    out_specs, out_shape = [row], [jax.ShapeDtypeStruct((m, d), F32)]
    if emit_h:
        in_specs += [mod, mod]
        args += list(nxt)
        out_specs.append(row)
        out_shape.append(jax.ShapeDtypeStruct((m, d), BF16))
    res = pl.pallas_call(
        functools.partial(_proj_ln_kernel, emit_h=emit_h, sub=sub),
        grid=(m // tm,),
        in_specs=in_specs,
        out_specs=out_specs,
        out_shape=out_shape,
        compiler_params=_params("parallel"),
        name=name,
    )(*args)
    return (res[0], res[1]) if emit_h else (res[0], None)


def _swiglu_kernel(a_ref, wg_ref, wu_ref, o_ref):
    a = a_ref[...]
    g = jnp.dot(a, wg_ref[0], preferred_element_type=F32)
    u = jnp.dot(a, wu_ref[0], preferred_element_type=F32)
    o_ref[...] = (_silu(g) * u).astype(BF16)


def _swiglu(h, w_up, l):
    m, k = h.shape
    dff = w_up.shape[2] // 2
    tm, tn = TM_MM, 512
    nj = dff // tn
    return pl.pallas_call(
        _swiglu_kernel,
        grid=(m // tm, nj),
        in_specs=[pl.BlockSpec((tm, k), lambda i, j: (i, 0)),
                  pl.BlockSpec((1, k, tn), lambda i, j: (l, 0, j)),
                  pl.BlockSpec((1, k, tn), lambda i, j: (l, 0, nj + j))],
        out_specs=pl.BlockSpec((tm, tn), lambda i, j: (i, j)),
        out_shape=jax.ShapeDtypeStruct((m, dff), BF16),
        compiler_params=_params("parallel", "parallel"),
        name="ffn_up_swiglu",
    )(h, w_up, w_up)


def _layer_weights(l, w_in_t, rwkv_mu, rwkv_w0, rwkv_w2, rwkv_a0, rwkv_a2, rwkv_g2, rwkv_k_k, rwkv_k_a,
                   rwkv_r_k, rwkv_ln_w, rwkv_ln_b, rwkv_v0, rwkv_v1, rwkv_v2, gdn_conv_w, gdn_a_log,
                   gdn_dt_bias):
    c = MIX
    rows = lambda lo, hi: w_in_t[l, lo:hi].astype(BF16)
    o_rwkv = c
    o_lora = o_rwkv + 3 * c
    n_lora = rwkv_w2.shape[1] + rwkv_a2.shape[1] + rwkv_g2.shape[1]
    o_gdn = o_lora + n_lora
    o_ab = o_gdn + 4 * c
    o_gate = o_ab + 2 * GDN_HEADS
    pad_rows = N_SMALL - n_lora - 2 * GDN_HEADS
    w_main = jnp.concatenate([rows(0, o_lora), rows(o_gdn, o_ab)], axis=0)
    assert n_lora - (N_SMALL - LANES) == GDN_G_LANE and GDN_B_LANE == GDN_G_LANE + GDN_HEADS
    w_small = jnp.concatenate([rows(o_lora, o_gdn), rows(o_ab, o_gate),
                               jnp.zeros((pad_rows, w_in_t.shape[2]), BF16)], axis=0)
    w_gate = rows(o_gate, w_in_t.shape[1])

    mu = rwkv_mu[l]
    row = lambda v: v.reshape(1, -1)
    n_w, n_a, n_g = rwkv_w2.shape[1], rwkv_a2.shape[1], rwkv_g2.shape[1]
    padrows = lambda w, lo: jnp.zeros((N_SMALL, c), F32).at[lo:lo + w.shape[0]].set(w).astype(BF16)
    lane_vec = lambda v, lo: jnp.zeros((1, LANES), F32).at[0, lo:lo + v.shape[0]].set(v)
    rw = {
        "mu_r": row(mu[:c]), "mu_k": row(mu[c:2 * c]), "mu_v": row(mu[2 * c:3 * c]),
        "mu_s": jnp.zeros((1, N_SMALL), F32).at[0, :n_lora].set(mu[3 * c:]),
        "w0": row(rwkv_w0[l]), "a0": row(rwkv_a0[l]), "k_k": row(rwkv_k_k[l]), "k_a": row(rwkv_k_a[l]),
        "w2p": padrows(rwkv_w2[l], 0), "a2p": padrows(rwkv_a2[l], n_w), "g2p": padrows(rwkv_g2[l], n_w + n_a),
        "r_k": row(rwkv_r_k[l]), "ln_w": row(rwkv_ln_w[l]), "ln_b": row(rwkv_ln_b[l]),
    }
    if l > 0:
        nv = rwkv_v1.shape[2]
        rw["v0"] = row(rwkv_v0[l - 1])
        rw["v1p"] = jnp.zeros((c, LANES), F32).at[:, :nv].set(rwkv_v1[l - 1]).astype(BF16)
        rw["v2p"] = jnp.zeros((LANES, c), F32).at[:nv].set(rwkv_v2[l - 1]).astype(BF16)
    cw = gdn_conv_w[l]
    gd = {"cw_q": cw[:, :c], "cw_k": cw[:, c:2 * c], "cw_v": cw[:, 2 * c:],
          "alog": lane_vec(gdn_a_log[l], GDN_G_LANE), "dtb": lane_vec(gdn_dt_bias[l], GDN_G_LANE)}
    return w_main, w_small, w_gate, rw, gd


def kernel(x, c, ada_w, ada_b, w_in, pool_w, pool_scale, rwkv_mu, rwkv_w0, rwkv_w2, rwkv_a0, rwkv_a2, rwkv_g2, rwkv_k_k, rwkv_k_a, rwkv_r_k, rwkv_ln_w, rwkv_ln_b, rwkv_v0, rwkv_v1, rwkv_v2, gdn_conv_w, gdn_a_log, gdn_dt_bias, gdn_norm_w, w_branch_a, w_branch_b, w_branch_c, w_out, ln1_w, ln1_b, ffn_w_up, ffn_w_down, ln2_w, ln2_b):
    batch, seq, d = x.shape
    assert seq % TM_MM == 0 and d == 2 * MIX
    m = batch * seq
    xf = x.reshape(m, d)

    mod = _ada(c, ada_w, ada_b)
    mods = [[mod[l, :, i * d:(i + 1) * d].reshape(batch, 1, d) for i in range(6)]
            for l in range(DEPTH)]

    h = _modulate(xf, mods[0][1], mods[0][0], seq)
    w_in_t = jnp.swapaxes(w_in, 1, 2)
    wa, wb, wc = (w.astype(BF16) for w in (w_branch_a, w_branch_b, w_branch_c))
    w_out_b, w_up_b, w_down_b = (w.astype(BF16) for w in (w_out, ffn_w_up, ffn_w_down))
    v_first = None
    for l in range(DEPTH):
        sh_m, sc_m, gt_m, sh_f, sc_f, gt_f = mods[l]
        w_main, w_small, w_gate, rw, gd = _layer_weights(
            l, w_in_t, rwkv_mu, rwkv_w0, rwkv_w2, rwkv_a0, rwkv_a2, rwkv_g2, rwkv_k_k, rwkv_k_a,
            rwkv_r_k, rwkv_ln_w, rwkv_ln_b, rwkv_v0, rwkv_v1, rwkv_v2, gdn_conv_w, gdn_a_log,
            gdn_dt_bias)

        p = _matmul_nt(h, w_main, tn=2048, out_dtype=BF16, name="in_proj_main")
        psm = _matmul_nt(h, w_small, tn=N_SMALL, out_dtype=F32, name="in_proj_small")
        gates = _matmul_nt(h, w_gate, tn=2048, out_dtype=BF16, act="sigmoid", name="in_proj_gates")

        y_a = _pool(p, pool_w[l], pool_scale[l], seq)
        prep = _rwkv_prep(p, psm, rw, v_first, seq)
        if l == 0:
            v_first = prep[2]
        y_b = _rwkv_core(prep, rw, batch, seq)
        gq, gk, gv, ggb = _gdn_prep(p, psm, gd, seq)
        y_c = _gdn_core(gq, gk, gv, ggb, p, gdn_norm_w[l], batch, seq)

        merged = _merge(y_a, y_b, y_c, wa, wb, wc, l, gates)
        xf, h = _proj_ln(merged, w_out_b, l, xf, gt_m, ln1_w[l], ln1_b[l],
                         (sc_f, sh_f), seq, tm=512, sub=256, name="out_proj_ln")

        act = _swiglu(h, w_up_b, l)
        nxt = (mods[l + 1][1], mods[l + 1][0]) if l + 1 < DEPTH else None
        xf, h = _proj_ln(act, w_down_b, l, xf, gt_f, ln2_w[l], ln2_b[l],
                         nxt, seq, tm=256, sub=256, name="ffn_down_ln")
    return xf.reshape(batch, seq, d)
```

```python
import functools
import math

import jax
import jax.numpy as jnp
from jax import lax
from jax.experimental import pallas as pl
from jax.experimental.pallas import tpu as pltpu

F32 = jnp.float32
BF16 = jnp.bfloat16

DEPTH = 2
DN_ALPHA = (2 * DEPTH) ** 0.25
LN_EPS = 1e-5
MIX = 1024
POOL_WINDOWS = (2, 4, 8, 16)
POOL_GROUP = 256
RWKV_HEAD = 64
RWKV_LN_EPS = RWKV_HEAD * 1e-5
GDN_HEAD = 128
GDN_HEADS = 8
GDN_EPS = 1e-6
CHUNK = 64
LANES = 128
N_SMALL = 384
GDN_G_LANE = 32
GDN_B_LANE = 40
VMEM_LIMIT = 56 * 1024 * 1024

TM_MM = 1024
TM_LN = 512
TM_PREP = 256
RB_RWKV = 256
RB_GDN = 128
GDN_HEADS_PER_STEP = 4
CORE_GROUP = 16
CORE_STAGGER = 1


def _dot(a, b):
    return jnp.dot(a.astype(BF16), b.astype(BF16), preferred_element_type=F32)


def _dot_nt(a, b):
    return lax.dot_general(a.astype(BF16), b.astype(BF16), (((1,), (1,)), ((), ())),
                           preferred_element_type=F32)


def _split(a, n):
    terms, rest = [], a
    for _ in range(n):
        t = rest.astype(BF16)
        terms.append(t)
        rest = rest - t.astype(F32)
    return terms


def _dot3(a, b, nt=False):
    d = _dot_nt if nt else _dot
    a1, a2 = _split(a, 2)
    b1, b2 = _split(b, 2)
    return d(a1, b1) + (d(a1, b2) + d(a2, b1))


def _dot_sel_l(sel, b, n):
    out = None
    for t in _split(b, n):
        p = jnp.dot(sel, t, preferred_element_type=F32)
        out = p if out is None else out + p
    return out


def _sigmoid(x):
    return 0.5 * jnp.tanh(0.5 * x) + 0.5


def _silu(x):
    return x * _sigmoid(x)


def _softplus(x):
    return jnp.maximum(x, 0.0) + jnp.log(1.0 + jnp.exp(-jnp.abs(x)))


def _iota2(shape, dim):
    return lax.broadcasted_iota(jnp.int32, shape, dim)


def _pair_masks():
    r = _iota2((CHUNK, LANES), 0)
    c = _iota2((CHUNK, LANES), 1) & (CHUNK - 1)
    same = lambda s: (r >> s) == (c >> s)
    r2 = _iota2((LANES, LANES), 0)
    c2 = _iota2((LANES, LANES), 1)
    lane = _iota2((1, LANES), 1)
    return {
        "eye": (r == c).astype(F32),
        "bd8": same(3),
        "off16": same(4) & jnp.logical_not(same(3)),
        "off32": same(5) & jnp.logical_not(same(4)),
        "off64": jnp.logical_not(same(5)),
        "strict": c < r,
        "incl": c <= r,
        "eye128": (r2 == c2).astype(F32),
        "bd64": (r2 >> 6) == (c2 >> 6),
        "low": _iota2((CHUNK, LANES), 1) < CHUNK,
        "half0": (lane < CHUNK).astype(F32),
        "half1": (lane >= CHUNK).astype(F32),
    }


def _stack(x, m):
    xb = x.astype(BF16)
    zero = jnp.zeros_like(xb)
    return jnp.concatenate([jnp.where(m["low"], xb, zero), jnp.where(m["low"], zero, xb)], axis=0)


def _chunk_cumsum_matrix(n):
    r = _iota2((n, n), 0)
    c = _iota2((n, n), 1)
    return (((r >> 6) == (c >> 6)) & (c <= r)).astype(BF16)


def _tri_inv(mats, m):
    bd = lambda x: _stack(x, m)
    a8 = [jnp.where(m["bd8"], a, 0.0) for a in mats]
    s = [m["eye"] + x for x in a8]
    p = [_dot(x, bd(x)) for x in a8]
    yield
    q = [_dot(pi, jnp.concatenate([bd(si), bd(pi)], axis=1)) for pi, si in zip(p, s)]
    yield
    s = [si + qi[:, :LANES] for si, qi in zip(s, q)]
    s = [si + _dot(qi[:, LANES:], bd(si)) for si, qi in zip(s, q)]
    yield
    for key in ("off16", "off32", "off64"):
        t = [_dot(jnp.where(m[key], a, 0.0), bd(si)) for a, si in zip(mats, s)]
        yield
        s = [si + _dot(si, bd(ti)) for si, ti in zip(s, t)]
        yield
    return s


def _interleave(gens, stagger):
    active, pending, tick = [], list(gens), 0
    while active or pending:
        if pending and tick % stagger == 0:
            active.append(pending.pop(0))
        tick += 1
        for g in list(active):
            try:
                next(g)
            except StopIteration:
                active.remove(g)


def _shift_rows(x, halo, s):
    halo = halo[halo.shape[0] - 8:]
    xr = pltpu.roll(x, s, 0)
    hr = pltpu.roll(halo, s, 0)
    rows = _iota2(halo.shape, 0)
    top = jnp.where(rows < s, hr, xr[:8])
    return top if x.shape[0] == 8 else jnp.concatenate([top, xr[8:]], axis=0)


def _shift_rows_bf16(x, halo, shifts):
    tm = x.shape[0]
    r = _iota2((tm, tm), 0)
    c = _iota2((tm, tm), 1)
    sel = jnp.concatenate([(r - c == s).astype(BF16) for s in shifts], axis=0)
    moved = jnp.dot(sel, x, preferred_element_type=F32)
    x_top = x[:16].astype(F32)[:8]
    outs = []
    for j, s in enumerate(shifts):
        top = _shift_rows(x_top, halo, s)
        outs.append(jnp.concatenate([top, moved[j * tm + 8:(j + 1) * tm]], axis=0))
    return outs


def _resid_ln(x, y, gt, lnw, lnb):
    z = DN_ALPHA * x + (1.0 + gt) * y
    mu = jnp.mean(z, axis=-1, keepdims=True)
    zc = z - mu
    var = jnp.mean(zc * zc, axis=-1, keepdims=True)
    return zc * lax.rsqrt(var + LN_EPS) * lnw + lnb


def _params(*sem):
    return pltpu.CompilerParams(dimension_semantics=sem, vmem_limit_bytes=VMEM_LIMIT)


def _ada_kernel(c_ref, w_ref, b_ref, o_ref):
    c = c_ref[...]
    o_ref[0] = _dot3(_silu(c), w_ref[0]) + b_ref[0]


def _ada(c, ada_w, ada_b):
    nl, d, n = ada_w.shape
    b = c.shape[0]
    tn = 512
    cp = jnp.zeros((8, d), F32).at[:b].set(c)
    out = pl.pallas_call(
        _ada_kernel,
        grid=(nl, n // tn),
        in_specs=[pl.BlockSpec((8, d), lambda l, j: (0, 0)),
                  pl.BlockSpec((1, d, tn), lambda l, j: (l, 0, j)),
                  pl.BlockSpec((1, 1, tn), lambda l, j: (l, 0, j))],
        out_specs=pl.BlockSpec((1, 8, tn), lambda l, j: (l, 0, j)),
        out_shape=jax.ShapeDtypeStruct((nl, 8, n), F32),
        compiler_params=_params("parallel", "parallel"),
        name="ada_mod",
    )(cp, ada_w, ada_b.reshape(nl, 1, n))
    return out[:, :b]


def _modulate_kernel(x_ref, sc_ref, sh_ref, h_ref):
    h_ref[...] = (x_ref[...] * (1.0 + sc_ref[0]) + sh_ref[0]).astype(BF16)


def _modulate(x, sc, sh, seq):
    m, d = x.shape
    tm = TM_LN
    per = seq // tm
    mod = pl.BlockSpec((1, 1, d), lambda i: (i // per, 0, 0))
    return pl.pallas_call(
        _modulate_kernel,
        grid=(m // tm,),
        in_specs=[pl.BlockSpec((tm, d), lambda i: (i, 0)), mod, mod],
        out_specs=pl.BlockSpec((tm, d), lambda i: (i, 0)),
        out_shape=jax.ShapeDtypeStruct((m, d), BF16),
        compiler_params=_params("parallel"),
        name="modulate",
    )(x, sc, sh)


def _mm_kernel(a_ref, bt_ref, o_ref, *, act):
    acc = lax.dot_general(a_ref[...], bt_ref[...], (((1,), (1,)), ((), ())),
                          preferred_element_type=F32)
    if act == "sigmoid":
        acc = _sigmoid(acc)
    o_ref[...] = acc.astype(o_ref.dtype)


def _matmul_nt(a, bt, *, tn, out_dtype, act=None, name):
    m, k = a.shape
    n = bt.shape[0]
    tm = TM_MM
    return pl.pallas_call(
        functools.partial(_mm_kernel, act=act),
        grid=(m // tm, n // tn),
        in_specs=[pl.BlockSpec((tm, k), lambda i, j: (i, 0)),
                  pl.BlockSpec((tn, k), lambda i, j: (j, 0))],
        out_specs=pl.BlockSpec((tm, tn), lambda i, j: (i, j)),
        out_shape=jax.ShapeDtypeStruct((m, n), out_dtype),
        compiler_params=_params("parallel", "parallel"),
        name=name,
    )(a, bt)


def _pool_kernel(p_ref, halo_ref, w_ref, scale_ref, o_ref, *, seq, tm):
    t0 = (pl.program_id(0) * tm) % seq
    x = p_ref[...].astype(F32)
    halo = jnp.where(t0 == 0, 0.0, halo_ref[...].astype(F32))
    ext = jnp.concatenate([halo, x], axis=0)
    sums = [ext]
    for sh in (1, 2, 4, 8):
        sums.append(sums[-1] + pltpu.roll(sums[-1], sh, 0))
    t = t0 + _iota2((tm, POOL_GROUP), 0)
    ys = []
    for g, win in enumerate(POOL_WINDOWS):
        cols = slice(g * POOL_GROUP, (g + 1) * POOL_GROUP)
        cnt = jnp.minimum(t + 1, win).astype(F32)
        pooled = sums[g + 1][16:, cols] / cnt - x[:, cols]
        ys.append(_dot(pooled, w_ref[g]))
    o_ref[...] = (jnp.concatenate(ys, axis=1) * scale_ref[...]).astype(BF16)


def _pool(p, pool_w, pool_scale, seq):
    m = p.shape[0]
    tm = TM_LN
    return pl.pallas_call(
        functools.partial(_pool_kernel, seq=seq, tm=tm),
        grid=(m // tm,),
        in_specs=[pl.BlockSpec((tm, MIX), lambda i: (i, 0)),
                  pl.BlockSpec((16, MIX), lambda i: (jnp.maximum(i * (tm // 16) - 1, 0), 0)),
                  pl.BlockSpec(pool_w.shape, lambda i: (0, 0, 0)),
                  pl.BlockSpec((1, MIX), lambda i: (0, 0))],
        out_specs=pl.BlockSpec((tm, MIX), lambda i: (i, 0)),
        out_shape=jax.ShapeDtypeStruct((m, MIX), BF16),
        compiler_params=_params("parallel"),
        name="pool_mixer",
    )(p, p, pool_w.astype(BF16), pool_scale.reshape(1, MIX))


def _rwkv_prep_kernel(*refs, seq, tm, has_vres):
    (pr, pk, pv, hr, hk, hv, ps, hs, mur, muk, muv, mus, w0, a0, kkp, kap,
     w2p, a2p, g2p) = refs[:19]
    if has_vres:
        vf, v0, v1p, v2p = refs[19:23]
        outs = refs[23:]
    else:
        outs = refs[19:]
    r_o, k_o, v_o, kk_o, a_o, ld_o, g_o = outs

    first = (pl.program_id(0) * tm) % seq == 0

    def lerp(p_ref, h_ref, mu_ref):
        halo = jnp.where(first, 0.0, h_ref[...].astype(F32))
        if p_ref.dtype == BF16:
            (prev,) = _shift_rows_bf16(p_ref[...], halo, (1,))
        else:
            prev = _shift_rows(p_ref[...], halo, 1)
        p = p_ref[...].astype(F32)
        return p + (prev - p) * mu_ref[...]

    xr = lerp(pr, hr, mur)
    xk = lerp(pk, hk, muk)
    xv = lerp(pv, hv, muv)
    xs = lerp(ps, hs, mus)

    z = w0[...] + _dot(jnp.tanh(xs), w2p[...])
    t = jnp.exp(-jnp.abs(z))
    log_decay = jnp.where(z < 0.0, t, 1.0) * (-math.exp(-0.5) / (1.0 + t))
    a = _sigmoid(a0[...] + _dot(xs, a2p[...]))
    g = _dot(_sigmoid(xs), g2p[...])
    if has_vres:
        gate = _sigmoid(v0[...] + _dot(_dot(xv, v1p[...]), v2p[...]))
        xv = xv + (vf[...].astype(F32) - xv) * gate

    kk = xk * kkp[...]
    r2 = _iota2((LANES, LANES), 0)
    c2 = _iota2((LANES, LANES), 1)
    head_ones = ((r2 >> 6) == (c2 >> 6)).astype(BF16)
    kk2 = kk * kk
    ss = jnp.concatenate(
        [_dot(kk2[:, i * LANES:(i + 1) * LANES], head_ones) for i in range(MIX // LANES)],
        axis=1)
    kk = kk * lax.rsqrt(jnp.maximum(ss, 1e-24))

    r_o[...] = xr.astype(r_o.dtype)
    k_o[...] = (xk * (1.0 + (a - 1.0) * kap[...])).astype(k_o.dtype)
    v_o[...] = xv.astype(v_o.dtype)
    kk_o[...] = kk.astype(kk_o.dtype)
    a_o[...] = a.astype(a_o.dtype)
    ld_o[...] = _dot_sel_l(_chunk_cumsum_matrix(tm), log_decay, 2)
    g_o[...] = g.astype(g_o.dtype)


def _rwkv_prep(p, psm, wts, v_first, seq):
    m = p.shape[0]
    tm = TM_PREP
    has_vres = v_first is not None
    tile = lambda cb: pl.BlockSpec((tm, MIX), lambda i, cb=cb: (i, cb))
    halo = lambda cb: pl.BlockSpec((16, MIX), lambda i, cb=cb: (jnp.maximum(i * (tm // 16) - 1, 0), cb))
    vec = pl.BlockSpec((1, MIX), lambda i: (0, 0))
    full = lambda arr: pl.BlockSpec(arr.shape, lambda i: (0,) * arr.ndim)
    in_specs = [tile(1), tile(2), tile(3), halo(1), halo(2), halo(3),
                pl.BlockSpec((tm, N_SMALL), lambda i: (i, 0)),
                pl.BlockSpec((8, N_SMALL), lambda i: (jnp.maximum(i * (tm // 8) - 1, 0), 0)),
                vec, vec, vec, pl.BlockSpec((1, N_SMALL), lambda i: (0, 0)),
                vec, vec, vec, vec,
                full(wts["w2p"]), full(wts["a2p"]), full(wts["g2p"])]
    args = [p, p, p, p, p, p, psm, psm, wts["mu_r"], wts["mu_k"], wts["mu_v"], wts["mu_s"],
            wts["w0"], wts["a0"], wts["k_k"], wts["k_a"], wts["w2p"], wts["a2p"], wts["g2p"]]
    if has_vres:
        in_specs += [pl.BlockSpec((tm, MIX), lambda i: (i, 0)), vec, full(wts["v1p"]), full(wts["v2p"])]
        args += [v_first, wts["v0"], wts["v1p"], wts["v2p"]]
    out = pl.BlockSpec((tm, MIX), lambda i: (i, 0))
    return pl.pallas_call(
        functools.partial(_rwkv_prep_kernel, seq=seq, tm=tm, has_vres=has_vres),
        grid=(m // tm,),
        in_specs=in_specs,
        out_specs=[out] * 7,
        out_shape=[jax.ShapeDtypeStruct((m, MIX), dt) for dt in (BF16,) * 5 + (F32, BF16)],
        compiler_params=_params("parallel"),
        name="rwkv_prep",
    )(*args)


def _rwkv_core_kernel(r_ref, k_ref, v_ref, kk_ref, a_ref, gc_ref, g_ref, rk_ref, lnw_ref, lnb_ref,
                      o_ref, h_ref):
    @pl.when(pl.program_id(1) == 0)
    def _():
        h_ref[...] = jnp.zeros_like(h_ref)

    nb, rb, _ = r_ref.shape
    nch = rb // CHUNK
    units = [(b, c) for c in range(nch) for b in range(nb)]
    m = _pair_masks()
    head_ones = m["bd64"].astype(BF16)
    stack = lambda x: _stack(x, m)
    cat0 = lambda *xs: jnp.concatenate(xs, axis=0)
    cat1 = lambda *xs: jnp.concatenate(xs, axis=1)

    def load(ref, u):
        b, c = u
        return ref[b, c * CHUNK:(c + 1) * CHUNK, :].astype(F32)

    first_row = _iota2((CHUNK, LANES), 0) == 0
    q_t, y0, phi, psi = {}, {}, {}, {}

    def chunk_terms(group):
        a_hat, r_hat, lhs, rhs, bbar_t, kbar_t, dlast, vs = [], [], [], [], [], [], [], []
        for u in group:
            gc = load(gc_ref, u)
            g_prev = jnp.where(first_row, 0.0, pltpu.roll(gc, 1, 0))
            kc, vc, kkc = load(k_ref, u), load(v_ref, u), load(kk_ref, u)
            bc = kkc * load(a_ref, u)
            glast = gc[CHUNK - 1:CHUNK]
            e_neg = jnp.exp(-gc)
            e_rest = jnp.exp(glast - gc)
            a_hat.append(-kkc * jnp.exp(g_prev))
            r_hat.append(load(r_ref, u) * jnp.exp(gc))
            lhs.append(cat0(a_hat[-1], r_hat[-1]))
            rhs.append(cat0(stack(bc * e_neg), stack(kc * e_neg)))
            bbar_t.append((bc * e_rest).T)
            kbar_t.append((kc * e_rest).T)
            dlast.append(jnp.exp(glast))
            vs.append(vc)
            yield
        big = [_dot_nt(x, y) for x, y in zip(lhs, rhs)]
        yield
        a_ab = [jnp.where(m["strict"], x[:CHUNK, :LANES], 0.0) for x in big]
        a_kr = [cat0(jnp.where(m["strict"], x[:CHUNK, LANES:], 0.0),
                     jnp.where(m["incl"], x[CHUNK:, LANES:], 0.0)) for x in big]
        a_rb = [jnp.where(m["incl"], x[CHUNK:, :LANES], 0.0) for x in big]
        x1 = [_dot(a, stack(v)) for a, v in zip(a_kr, vs)]
        t_inv = yield from _tri_inv(a_ab, m)
        x2 = [_dot(t, cat1(stack(a), stack(x[:CHUNK]))) for t, a, x in zip(t_inv, a_hat, x1)]
        yield
        x3 = [_dot(a, cat1(stack(x[:, :LANES]), stack(x[:, LANES:]))) for a, x in zip(a_rb, x2)]
        for i, u in enumerate(group):
            w_t, u_t = x2[i][:, :LANES], x2[i][:, LANES:]
            phi[u] = m["eye128"] * dlast[i] + jnp.where(m["bd64"], _dot(bbar_t[i], w_t), 0.0)
            psi[u] = jnp.where(m["bd64"], _dot(cat1(bbar_t[i], kbar_t[i]), cat0(u_t, vs[i])), 0.0)
        yield
        for i, u in enumerate(group):
            q_t[u] = r_hat[i] + x3[i][:, :LANES]
            y0[u] = x3[i][:, LANES:] + x1[i][CHUNK:]

    _interleave([chunk_terms(units[i:i + CORE_GROUP]) for i in range(0, len(units), CORE_GROUP)],
                CORE_STAGGER)

    ys = {}
    hs = [h_ref[b] for b in range(nb)]
    for u in units:
        b = u[0]
        qh = _dot(cat0(q_t[u], phi[u]), hs[b])
        ys[u] = y0[u] + qh[:CHUNK]
        hs[b] = qh[CHUNK:] + psi[u]
    for b in range(nb):
        h_ref[b] = hs[b]

    y = cat0(*[ys[(b, c)] for b in range(nb) for c in range(nch)])
    flat = lambda ref: ref[...].astype(F32).reshape(nb * rb, LANES)
    rr, kk_, vv = flat(r_ref), flat(k_ref), flat(v_ref)
    sums = _dot(cat0(y, rr * kk_ * rk_ref[...]), head_ones)
    yc = y - sums[:nb * rb] * (1.0 / RWKV_HEAD)
    var = _dot(yc * yc, head_ones) * (1.0 / RWKV_HEAD)
    yn = yc * lax.rsqrt(var + RWKV_LN_EPS) * lnw_ref[...] + lnb_ref[...]
    bonus = sums[nb * rb:] * vv
    o_ref[...] = ((yn + bonus) * flat(g_ref)).astype(BF16).reshape(nb, rb, LANES)


def _rwkv_core(prep, wts, batch, seq):
    rb = RB_RWKV
    tile = pl.BlockSpec((batch, rb, LANES), lambda p, j: (0, j, p))
    vec = pl.BlockSpec((1, LANES), lambda p, j: (0, p))
    out = pl.pallas_call(
        _rwkv_core_kernel,
        grid=(MIX // LANES, seq // rb),
        in_specs=[tile] * 7 + [vec] * 3,
        out_specs=tile,
        out_shape=jax.ShapeDtypeStruct((batch, seq, MIX), BF16),
        scratch_shapes=[pltpu.VMEM((batch, LANES, LANES), F32)],
        compiler_params=_params("parallel", "arbitrary"),
        name="rwkv_core",
    )(*[a.reshape(batch, seq, MIX) for a in prep], wts["r_k"], wts["ln_w"], wts["ln_b"])
    return out.reshape(batch * seq, MIX)


def _gdn_prep_kernel(pq, pk, pv, hq, hk, hv, ps, cwq, cwk, cwv, alog, dtb,
                     q_o, k_o, v_o, gb_o, *, seq, tm):
    first = (pl.program_id(0) * tm) % seq == 0
    ones = jnp.ones((LANES, LANES), BF16)

    def conv_silu(p_ref, h_ref, w_ref):
        halo = jnp.where(first, 0.0, h_ref[...].astype(F32))
        w = w_ref[...]
        y = p_ref[...].astype(F32) * w[3:4]
        for s, xs in zip((1, 2, 3), _shift_rows_bf16(p_ref[...], halo, (1, 2, 3))):
            y = y + xs * w[3 - s:4 - s]
        return _silu(y)

    def l2norm(y, scale):
        blocks = []
        for i in range(GDN_HEADS):
            blk = y[:, i * LANES:(i + 1) * LANES]
            ss = _dot(blk * blk, ones)
            blocks.append(blk * (lax.rsqrt(ss + GDN_EPS) * scale))
        return jnp.concatenate(blocks, axis=1)

    q_o[...] = l2norm(conv_silu(pq, hq, cwq), GDN_HEAD ** -0.5).astype(q_o.dtype)
    k_o[...] = l2norm(conv_silu(pk, hk, cwk), 1.0).astype(k_o.dtype)
    v_o[...] = conv_silu(pv, hv, cwv).astype(v_o.dtype)

    raw = ps[:, N_SMALL - LANES:]
    gval = -jnp.exp(alog[...]) * _softplus(raw + dtb[...])
    gcum = _dot_sel_l(_chunk_cumsum_matrix(tm), gval, 3)
    lane = _iota2(raw.shape, 1)
    is_g = (lane >= GDN_G_LANE) & (lane < GDN_G_LANE + GDN_HEADS)
    is_b = (lane >= GDN_B_LANE) & (lane < GDN_B_LANE + GDN_HEADS)
    gb_o[...] = jnp.where(is_g, gcum, jnp.where(is_b, _sigmoid(raw), 0.0))


def _gdn_prep(p, psm, wts, seq):
    m = p.shape[0]
    tm = TM_PREP
    tile = lambda cb: pl.BlockSpec((tm, MIX), lambda i, cb=cb: (i, cb))
    halo = lambda cb: pl.BlockSpec((16, MIX), lambda i, cb=cb: (jnp.maximum(i * (tm // 16) - 1, 0), cb))
    cw = pl.BlockSpec((4, MIX), lambda i: (0, 0))
    lv = pl.BlockSpec((1, LANES), lambda i: (0, 0))
    out = pl.BlockSpec((tm, MIX), lambda i: (i, 0))
    return pl.pallas_call(
        functools.partial(_gdn_prep_kernel, seq=seq, tm=tm),
        grid=(m // tm,),
        in_specs=[tile(4), tile(5), tile(6), halo(4), halo(5), halo(6),
                  pl.BlockSpec((tm, N_SMALL), lambda i: (i, 0)), cw, cw, cw, lv, lv],
        out_specs=[out, out, out, pl.BlockSpec((tm, LANES), lambda i: (i, 0))],
        out_shape=[jax.ShapeDtypeStruct((m, MIX), BF16)] * 3 + [jax.ShapeDtypeStruct((m, LANES), F32)],
        compiler_params=_params("parallel"),
        name="gdn_prep",
    )(p, p, p, p, p, p, psm, wts["cw_q"], wts["cw_k"], wts["cw_v"], wts["alog"], wts["dtb"])


def _gdn_core_kernel(q_ref, k_ref, v_ref, gb_ref, z_ref, nw_ref, o_ref, s_ref):
    @pl.when(pl.program_id(1) == 0)
    def _():
        s_ref[...] = jnp.zeros_like(s_ref)

    nb, rb, width = q_ref.shape
    nh = width // LANES
    units = [(b, hh) for hh in range(nh) for b in range(nb)]
    assert rb == 2 * CHUNK
    m = _pair_masks()
    ones = jnp.ones((LANES, LANES), BF16)
    cat0 = lambda *xs: jnp.concatenate(xs, axis=0)
    cat1 = lambda *xs: jnp.concatenate(xs, axis=1)

    def lane_bcast(x, lane):
        return jnp.broadcast_to(pltpu.roll(x, LANES - lane, 1)[:, 0:1], x.shape)

    nch = rb // CHUNK
    q_t, y0, phi, psi = {}, {}, {}, {}

    def tile_terms(group):
        gf, ks, lhs, kbd, rhs, e_g, qs, gamma = [], [], [], [], [], [], [], []
        zero = jnp.zeros((CHUNK, LANES), F32)
        for b, hh in group:
            hl = slice(hh * LANES, (hh + 1) * LANES)
            head = nh * pl.program_id(0) + hh
            gbv = gb_ref[b]
            g = lane_bcast(gbv, GDN_G_LANE + head)
            beta = lane_bcast(gbv, GDN_B_LANE + head)
            q, k, v = (r[b, :, hl].astype(F32) for r in (q_ref, k_ref, v_ref))
            g_col = g[:CHUNK] * m["half0"] + g[CHUNK:] * m["half1"]
            diff = g_col - g.T[:CHUNK]
            gamma.append(jnp.where(m["incl"], jnp.exp(jnp.where(m["incl"], diff, 0.0)), 0.0))
            eg = jnp.exp(g)
            kb = k * beta
            gf.append(g)
            ks.append(k)
            qs.append(q)
            e_g.append(eg)
            lhs.append(cat0(cat1(kb[:CHUNK], kb[CHUNK:]), cat1(q[:CHUNK], q[CHUNK:])))
            kbd.append(cat0(cat1(k[:CHUNK], zero), cat1(zero, k[CHUNK:])))
            rhs.append(cat1(v * beta, kb * eg))
            yield
        kq = [_dot_nt(x, y) for x, y in zip(lhs, kbd)]
        yield
        a_neg = [-jnp.where(m["strict"], x[:CHUNK] * gm, 0.0) for x, gm in zip(kq, gamma)]
        a_qk = [x[CHUNK:] * gm for x, gm in zip(kq, gamma)]
        k_bar_t = [[(k[c * CHUNK:(c + 1) * CHUNK]
                     * jnp.exp(g[(c + 1) * CHUNK - 1:(c + 1) * CHUNK] - g[c * CHUNK:(c + 1) * CHUNK])).T
                    for c in range(nch)] for k, g in zip(ks, gf)]
        t_inv = yield from _tri_inv(a_neg, m)
        uw = [_dot(_stack(t, m), x) for t, x in zip(t_inv, rhs)]
        yield
        x3 = [_dot(_stack(a, m), x) for a, x in zip(a_qk, uw)]
        for i, u in enumerate(group):
            for c in range(nch):
                glast = gf[i][(c + 1) * CHUNK - 1:(c + 1) * CHUNK]
                pp = _dot(k_bar_t[i][c], uw[i][c * CHUNK:(c + 1) * CHUNK])
                phi[(u, c)] = m["eye128"] * jnp.exp(glast) - pp[:, LANES:]
                psi[(u, c)] = pp[:, :LANES]
        yield
        for i, u in enumerate(group):
            y0[u] = x3[i][:, :LANES]
            q_t[u] = qs[i] * e_g[i] - x3[i][:, LANES:]

    _interleave([tile_terms(units[i:i + CORE_GROUP]) for i in range(0, len(units), CORE_GROUP)],
                CORE_STAGGER)

    ys = {u: [] for u in units}
    st = {(b, hh): s_ref[hh, b] for b, hh in units}
    for c in range(nch):
        cs = slice(c * CHUNK, (c + 1) * CHUNK)
        for u in units:
            qs_ = _dot(cat0(q_t[u][cs], phi[(u, c)]), st[u])
            ys[u].append(y0[u][cs] + qs_[:CHUNK])
            st[u] = qs_[CHUNK:] + psi[(u, c)]
    for b, hh in units:
        s_ref[hh, b] = st[(b, hh)]

    for hh in range(nh):
        hl = slice(hh * LANES, (hh + 1) * LANES)
        y = cat0(*[yc for b in range(nb) for yc in ys[(b, hh)]])
        ms = _dot(y * y, ones) * (1.0 / GDN_HEAD)
        out = y * lax.rsqrt(ms + GDN_EPS) * nw_ref[...]
        z = z_ref[:, :, hl].astype(F32).reshape(nb * rb, LANES)
        o_ref[:, :, hl] = (out * _silu(z)).astype(BF16).reshape(nb, rb, LANES)


def _gdn_core(q, k, v, gb, p, norm_w, batch, seq):
    rb, nh = RB_GDN, GDN_HEADS_PER_STEP
    width = nh * LANES
    tile = pl.BlockSpec((batch, rb, width), lambda h, j: (0, j, h))
    z_col0 = 7 * (MIX // width)
    r3 = lambda a: a.reshape(batch, seq, a.shape[-1])
    out = pl.pallas_call(
        _gdn_core_kernel,
        grid=(GDN_HEADS // nh, seq // rb),
        in_specs=[tile, tile, tile,
                  pl.BlockSpec((batch, rb, LANES), lambda h, j: (0, j, 0)),
                  pl.BlockSpec((batch, rb, width), lambda h, j: (0, j, z_col0 + h)),
                  pl.BlockSpec((1, LANES), lambda h, j: (0, 0))],
        out_specs=tile,
        out_shape=jax.ShapeDtypeStruct((batch, seq, MIX), BF16),
        scratch_shapes=[pltpu.VMEM((nh, batch, LANES, LANES), F32)],
        compiler_params=_params("parallel", "arbitrary"),
        name="gdn_core",
    )(r3(q), r3(k), r3(v), r3(gb), r3(p), norm_w.reshape(1, LANES))
    return out.reshape(batch * seq, MIX)


def _merge_kernel(ya, yb, yc, wa, wb, wc, ga, gb, gc, o_ref):
    dot = lambda y, w: jnp.dot(y[...], w[0], preferred_element_type=F32)
    acc = ga[...].astype(F32) * dot(ya, wa)
    acc = acc + gb[...].astype(F32) * dot(yb, wb)
    acc = acc + gc[...].astype(F32) * dot(yc, wc)
    o_ref[...] = acc.astype(BF16)


def _merge(ya, yb, yc, wa, wb, wc, l, gates):
    m = ya.shape[0]
    d = wa.shape[2]
    tm = TM_LN
    y = pl.BlockSpec((tm, MIX), lambda i: (i, 0))
    w = pl.BlockSpec((1, MIX, d), lambda i: (l, 0, 0), pipeline_mode=pl.Buffered(1))
    g = lambda br: pl.BlockSpec((tm, d), lambda i, br=br: (i, br))
    return pl.pallas_call(
        _merge_kernel,
        grid=(m // tm,),
        in_specs=[y, y, y, w, w, w, g(0), g(1), g(2)],
        out_specs=pl.BlockSpec((tm, d), lambda i: (i, 0)),
        out_shape=jax.ShapeDtypeStruct((m, d), BF16),
        compiler_params=_params("parallel"),
        name="merge",
    )(ya, yb, yc, wa, wb, wc, gates, gates, gates)


def _proj_ln_kernel(*refs, emit_h):
    a_ref, w_ref, x_ref, gt_ref, lnw_ref, lnb_ref = refs[:6]
    y = jnp.dot(a_ref[...], w_ref[0], preferred_element_type=F32)
    xn = _resid_ln(x_ref[...], y, gt_ref[0], lnw_ref[...], lnb_ref[...])
    if emit_h:
        sc_ref, sh_ref, xo_ref, ho_ref = refs[6:]
        ho_ref[...] = (xn * (1.0 + sc_ref[0]) + sh_ref[0]).astype(BF16)
    else:
        (xo_ref,) = refs[6:]
    xo_ref[...] = xn


def _proj_ln(a, w, l, x, gt, lnw, lnb, nxt, seq, *, tm, name):
    m, k = a.shape
    d = w.shape[2]
    per = seq // tm
    emit_h = nxt is not None
    row = pl.BlockSpec((tm, d), lambda i: (i, 0))
    mod = pl.BlockSpec((1, 1, d), lambda i: (i // per, 0, 0))
    vec = pl.BlockSpec((1, d), lambda i: (0, 0))
    in_specs = [pl.BlockSpec((tm, k), lambda i: (i, 0)),
                pl.BlockSpec((1, k, d), lambda i: (l, 0, 0), pipeline_mode=pl.Buffered(1)),
                row, mod, vec, vec]
    args = [a, w, x, gt, lnw.reshape(1, d), lnb.reshape(1, d)]
    out_specs, out_shape = [row], [jax.ShapeDtypeStruct((m, d), F32)]
    if emit_h:
        in_specs += [mod, mod]
        args += list(nxt)
        out_specs.append(row)
        out_shape.append(jax.ShapeDtypeStruct((m, d), BF16))
    res = pl.pallas_call(
        functools.partial(_proj_ln_kernel, emit_h=emit_h),
        grid=(m // tm,),
        in_specs=in_specs,
        out_specs=out_specs,
        out_shape=out_shape,
        compiler_params=_params("parallel"),
        name=name,
    )(*args)
    return (res[0], res[1]) if emit_h else (res[0], None)


def _swiglu_kernel(a_ref, wg_ref, wu_ref, o_ref):
    a = a_ref[...]
    g = jnp.dot(a, wg_ref[0], preferred_element_type=F32)
    u = jnp.dot(a, wu_ref[0], preferred_element_type=F32)
    o_ref[...] = (_silu(g) * u).astype(BF16)


def _swiglu(h, w_up, l):
    m, k = h.shape
    dff = w_up.shape[2] // 2
    tm, tn = TM_MM, 512
    nj = dff // tn
    return pl.pallas_call(
        _swiglu_kernel,
        grid=(m // tm, nj),
        in_specs=[pl.BlockSpec((tm, k), lambda i, j: (i, 0)),
                  pl.BlockSpec((1, k, tn), lambda i, j: (l, 0, j)),
                  pl.BlockSpec((1, k, tn), lambda i, j: (l, 0, nj + j))],
        out_specs=pl.BlockSpec((tm, tn), lambda i, j: (i, j)),
        out_shape=jax.ShapeDtypeStruct((m, dff), BF16),
        compiler_params=_params("parallel", "parallel"),
        name="ffn_up_swiglu",
    )(h, w_up, w_up)


def _layer_weights(l, w_in_t, rwkv_mu, rwkv_w0, rwkv_w2, rwkv_a0, rwkv_a2, rwkv_g2, rwkv_k_k, rwkv_k_a,
                   rwkv_r_k, rwkv_ln_w, rwkv_ln_b, rwkv_v0, rwkv_v1, rwkv_v2, gdn_conv_w, gdn_a_log,
                   gdn_dt_bias):
    c = MIX
    rows = lambda lo, hi: w_in_t[l, lo:hi].astype(BF16)
    o_rwkv = c
    o_lora = o_rwkv + 3 * c
    n_lora = rwkv_w2.shape[1] + rwkv_a2.shape[1] + rwkv_g2.shape[1]
    o_gdn = o_lora + n_lora
    o_ab = o_gdn + 4 * c
    o_gate = o_ab + 2 * GDN_HEADS
    pad_rows = N_SMALL - n_lora - 2 * GDN_HEADS
    w_main = jnp.concatenate([rows(0, o_lora), rows(o_gdn, o_ab)], axis=0)
    assert n_lora - (N_SMALL - LANES) == GDN_G_LANE and GDN_B_LANE == GDN_G_LANE + GDN_HEADS
    w_small = jnp.concatenate([rows(o_lora, o_gdn), rows(o_ab, o_gate),
                               jnp.zeros((pad_rows, w_in_t.shape[2]), BF16)], axis=0)
    w_gate = rows(o_gate, w_in_t.shape[1])

    mu = rwkv_mu[l]
    row = lambda v: v.reshape(1, -1)
    n_w, n_a, n_g = rwkv_w2.shape[1], rwkv_a2.shape[1], rwkv_g2.shape[1]
    padrows = lambda w, lo: jnp.zeros((N_SMALL, c), F32).at[lo:lo + w.shape[0]].set(w).astype(BF16)
    lane_vec = lambda v, lo: jnp.zeros((1, LANES), F32).at[0, lo:lo + v.shape[0]].set(v)
    rw = {
        "mu_r": row(mu[:c]), "mu_k": row(mu[c:2 * c]), "mu_v": row(mu[2 * c:3 * c]),
        "mu_s": jnp.zeros((1, N_SMALL), F32).at[0, :n_lora].set(mu[3 * c:]),
        "w0": row(rwkv_w0[l]), "a0": row(rwkv_a0[l]), "k_k": row(rwkv_k_k[l]), "k_a": row(rwkv_k_a[l]),
        "w2p": padrows(rwkv_w2[l], 0), "a2p": padrows(rwkv_a2[l], n_w), "g2p": padrows(rwkv_g2[l], n_w + n_a),
        "r_k": row(rwkv_r_k[l]), "ln_w": row(rwkv_ln_w[l]), "ln_b": row(rwkv_ln_b[l]),
    }
    if l > 0:
        nv = rwkv_v1.shape[2]
        rw["v0"] = row(rwkv_v0[l - 1])
        rw["v1p"] = jnp.zeros((c, LANES), F32).at[:, :nv].set(rwkv_v1[l - 1]).astype(BF16)
        rw["v2p"] = jnp.zeros((LANES, c), F32).at[:nv].set(rwkv_v2[l - 1]).astype(BF16)
    cw = gdn_conv_w[l]
    gd = {"cw_q": cw[:, :c], "cw_k": cw[:, c:2 * c], "cw_v": cw[:, 2 * c:],
          "alog": lane_vec(gdn_a_log[l], GDN_G_LANE), "dtb": lane_vec(gdn_dt_bias[l], GDN_G_LANE)}
    return w_main, w_small, w_gate, rw, gd


def kernel(x, c, ada_w, ada_b, w_in, pool_w, pool_scale, rwkv_mu, rwkv_w0, rwkv_w2, rwkv_a0, rwkv_a2, rwkv_g2, rwkv_k_k, rwkv_k_a, rwkv_r_k, rwkv_ln_w, rwkv_ln_b, rwkv_v0, rwkv_v1, rwkv_v2, gdn_conv_w, gdn_a_log, gdn_dt_bias, gdn_norm_w, w_branch_a, w_branch_b, w_branch_c, w_out, ln1_w, ln1_b, ffn_w_up, ffn_w_down, ln2_w, ln2_b):
    batch, seq, d = x.shape
    assert seq % TM_MM == 0 and d == 2 * MIX
    m = batch * seq
    xf = x.reshape(m, d)

    mod = _ada(c, ada_w, ada_b)
    mods = [[mod[l, :, i * d:(i + 1) * d].reshape(batch, 1, d) for i in range(6)]
            for l in range(DEPTH)]

    h = _modulate(xf, mods[0][1], mods[0][0], seq)
    w_in_t = jnp.swapaxes(w_in, 1, 2)
    wa, wb, wc = (w.astype(BF16) for w in (w_branch_a, w_branch_b, w_branch_c))
    w_out_b, w_up_b, w_down_b = (w.astype(BF16) for w in (w_out, ffn_w_up, ffn_w_down))
    v_first = None
    for l in range(DEPTH):
        sh_m, sc_m, gt_m, sh_f, sc_f, gt_f = mods[l]
        w_main, w_small, w_gate, rw, gd = _layer_weights(
            l, w_in_t, rwkv_mu, rwkv_w0, rwkv_w2, rwkv_a0, rwkv_a2, rwkv_g2, rwkv_k_k, rwkv_k_a,
            rwkv_r_k, rwkv_ln_w, rwkv_ln_b, rwkv_v0, rwkv_v1, rwkv_v2, gdn_conv_w, gdn_a_log,
            gdn_dt_bias)

        p = _matmul_nt(h, w_main, tn=2048, out_dtype=BF16, name="in_proj_main")
        psm = _matmul_nt(h, w_small, tn=N_SMALL, out_dtype=F32, name="in_proj_small")
        gates = _matmul_nt(h, w_gate, tn=2048, out_dtype=BF16, act="sigmoid", name="in_proj_gates")

        y_a = _pool(p, pool_w[l], pool_scale[l], seq)
        prep = _rwkv_prep(p, psm, rw, v_first, seq)
        if l == 0:
            v_first = prep[2]
        y_b = _rwkv_core(prep, rw, batch, seq)
        gq, gk, gv, ggb = _gdn_prep(p, psm, gd, seq)
        y_c = _gdn_core(gq, gk, gv, ggb, p, gdn_norm_w[l], batch, seq)

        merged = _merge(y_a, y_b, y_c, wa, wb, wc, l, gates)
        xf, h = _proj_ln(merged, w_out_b, l, xf, gt_m, ln1_w[l], ln1_b[l],
                         (sc_f, sh_f), seq, tm=512, name="out_proj_ln")

        act = _swiglu(h, w_up_b, l)
        nxt = (mods[l + 1][1], mods[l + 1][0]) if l + 1 < DEPTH else None
        xf, h = _proj_ln(act, w_down_b, l, xf, gt_f, ln2_w[l], ln2_b[l],
                         nxt, seq, tm=256, name="ffn_down_ln")
    return xf.reshape(batch, seq, d)
```

```python
import functools
import math

import jax
import jax.numpy as jnp
from jax import lax
from jax.experimental import pallas as pl
from jax.experimental.pallas import tpu as pltpu

F32 = jnp.float32
BF16 = jnp.bfloat16

DEPTH = 2
DN_ALPHA = (2 * DEPTH) ** 0.25
LN_EPS = 1e-5
MIX = 1024
POOL_WINDOWS = (2, 4, 8, 16)
POOL_GROUP = 256
RWKV_HEAD = 64
RWKV_LN_EPS = RWKV_HEAD * 1e-5
GDN_HEAD = 128
GDN_HEADS = 8
GDN_EPS = 1e-6
CHUNK = 64
LANES = 128
N_SMALL = 384
GDN_G_LANE = 32
GDN_B_LANE = 40
VMEM_LIMIT = 56 * 1024 * 1024

TM_MM = 1024
TM_LN = 512
TM_PREP = 256
RB_RWKV = 512
RB_GDN = 128
GDN_HEADS_PER_STEP = 8
CORE_GROUP = 32
CORE_STAGGER = 1


def _dot(a, b):
    return jnp.dot(a.astype(BF16), b.astype(BF16), preferred_element_type=F32)


def _dot_nt(a, b):
    return lax.dot_general(a.astype(BF16), b.astype(BF16), (((1,), (1,)), ((), ())),
                           preferred_element_type=F32)


def _split(a, n):
    terms, rest = [], a
    for _ in range(n):
        t = rest.astype(BF16)
        terms.append(t)
        rest = rest - t.astype(F32)
    return terms


def _dot3(a, b, nt=False):
    d = _dot_nt if nt else _dot
    a1, a2 = _split(a, 2)
    b1, b2 = _split(b, 2)
    return d(a1, b1) + (d(a1, b2) + d(a2, b1))


def _dot_sel_l(sel, b, n):
    out = None
    for t in _split(b, n):
        p = jnp.dot(sel, t, preferred_element_type=F32)
        out = p if out is None else out + p
    return out


def _sigmoid(x):
    return 0.5 * jnp.tanh(0.5 * x) + 0.5


def _silu(x):
    return x * _sigmoid(x)


def _softplus(x):
    return jnp.maximum(x, 0.0) + jnp.log(1.0 + jnp.exp(-jnp.abs(x)))


def _iota2(shape, dim):
    return lax.broadcasted_iota(jnp.int32, shape, dim)


def _pair_masks():
    r = _iota2((CHUNK, LANES), 0)
    c = _iota2((CHUNK, LANES), 1) & (CHUNK - 1)
    same = lambda s: (r >> s) == (c >> s)
    r2 = _iota2((LANES, LANES), 0)
    c2 = _iota2((LANES, LANES), 1)
    lane = _iota2((1, LANES), 1)
    return {
        "eye": (r == c).astype(F32),
        "bd8": same(3),
        "off16": same(4) & jnp.logical_not(same(3)),
        "off32": same(5) & jnp.logical_not(same(4)),
        "off64": jnp.logical_not(same(5)),
        "strict": c < r,
        "incl": c <= r,
        "eye128": (r2 == c2).astype(F32),
        "bd64": (r2 >> 6) == (c2 >> 6),
        "low": _iota2((CHUNK, LANES), 1) < CHUNK,
        "half0": (lane < CHUNK).astype(F32),
        "half1": (lane >= CHUNK).astype(F32),
    }


def _stack(x, m):
    xb = x.astype(BF16)
    zero = jnp.zeros_like(xb)
    return jnp.concatenate([jnp.where(m["low"], xb, zero), jnp.where(m["low"], zero, xb)], axis=0)


def _chunk_cumsum_matrix(n):
    r = _iota2((n, n), 0)
    c = _iota2((n, n), 1)
    return (((r >> 6) == (c >> 6)) & (c <= r)).astype(BF16)


def _tri_inv(mats, m):
    bd = lambda x: _stack(x, m)
    a8 = [jnp.where(m["bd8"], a, 0.0) for a in mats]
    s = [m["eye"] + x for x in a8]
    p = [_dot(x, bd(x)) for x in a8]
    yield
    q = [_dot(pi, jnp.concatenate([bd(si), bd(pi)], axis=1)) for pi, si in zip(p, s)]
    yield
    s = [si + qi[:, :LANES] for si, qi in zip(s, q)]
    s = [si + _dot(qi[:, LANES:], bd(si)) for si, qi in zip(s, q)]
    yield
    for key in ("off16", "off32", "off64"):
        t = [_dot(jnp.where(m[key], a, 0.0), bd(si)) for a, si in zip(mats, s)]
        yield
        s = [si + _dot(si, bd(ti)) for si, ti in zip(s, t)]
        yield
    return s


def _interleave(gens, stagger):
    active, pending, tick = [], list(gens), 0
    while active or pending:
        if pending and tick % stagger == 0:
            active.append(pending.pop(0))
        tick += 1
        for g in list(active):
            try:
                next(g)
            except StopIteration:
                active.remove(g)


def _shift_rows(x, halo, s):
    halo = halo[halo.shape[0] - 8:]
    xr = pltpu.roll(x, s, 0)
    hr = pltpu.roll(halo, s, 0)
    rows = _iota2(halo.shape, 0)
    top = jnp.where(rows < s, hr, xr[:8])
    return top if x.shape[0] == 8 else jnp.concatenate([top, xr[8:]], axis=0)


def _shift_rows_bf16(x, halo, shifts):
    tm = x.shape[0]
    r = _iota2((tm, tm), 0)
    c = _iota2((tm, tm), 1)
    sel = jnp.concatenate([(r - c == s).astype(BF16) for s in shifts], axis=0)
    moved = jnp.dot(sel, x, preferred_element_type=F32)
    x_top = x[:16].astype(F32)[:8]
    outs = []
    for j, s in enumerate(shifts):
        top = _shift_rows(x_top, halo, s)
        outs.append(jnp.concatenate([top, moved[j * tm + 8:(j + 1) * tm]], axis=0))
    return outs


def _resid_ln(x, y, gt, lnw, lnb):
    z = DN_ALPHA * x + (1.0 + gt) * y
    mu = jnp.mean(z, axis=-1, keepdims=True)
    zc = z - mu
    var = jnp.mean(zc * zc, axis=-1, keepdims=True)
    return zc * lax.rsqrt(var + LN_EPS) * lnw + lnb


def _params(*sem):
    return pltpu.CompilerParams(dimension_semantics=sem, vmem_limit_bytes=VMEM_LIMIT)


def _ada_kernel(c_ref, w_ref, b_ref, o_ref):
    c = c_ref[...]
    o_ref[0] = _dot3(_silu(c), w_ref[0]) + b_ref[0]


def _ada(c, ada_w, ada_b):
    nl, d, n = ada_w.shape
    b = c.shape[0]
    tn = 512
    cp = jnp.zeros((8, d), F32).at[:b].set(c)
    out = pl.pallas_call(
        _ada_kernel,
        grid=(nl, n // tn),
        in_specs=[pl.BlockSpec((8, d), lambda l, j: (0, 0)),
                  pl.BlockSpec((1, d, tn), lambda l, j: (l, 0, j)),
                  pl.BlockSpec((1, 1, tn), lambda l, j: (l, 0, j))],
        out_specs=pl.BlockSpec((1, 8, tn), lambda l, j: (l, 0, j)),
        out_shape=jax.ShapeDtypeStruct((nl, 8, n), F32),
        compiler_params=_params("parallel", "parallel"),
        name="ada_mod",
    )(cp, ada_w, ada_b.reshape(nl, 1, n))
    return out[:, :b]


def _modulate_kernel(x_ref, sc_ref, sh_ref, h_ref):
    h_ref[...] = (x_ref[...] * (1.0 + sc_ref[0]) + sh_ref[0]).astype(BF16)


def _modulate(x, sc, sh, seq):
    m, d = x.shape
    tm = TM_LN
    per = seq // tm
    mod = pl.BlockSpec((1, 1, d), lambda i: (i // per, 0, 0))
    return pl.pallas_call(
        _modulate_kernel,
        grid=(m // tm,),
        in_specs=[pl.BlockSpec((tm, d), lambda i: (i, 0)), mod, mod],
        out_specs=pl.BlockSpec((tm, d), lambda i: (i, 0)),
        out_shape=jax.ShapeDtypeStruct((m, d), BF16),
        compiler_params=_params("parallel"),
        name="modulate",
    )(x, sc, sh)


def _mm_kernel(a_ref, bt_ref, o_ref, *, act):
    acc = lax.dot_general(a_ref[...], bt_ref[...], (((1,), (1,)), ((), ())),
                          preferred_element_type=F32)
    if act == "sigmoid":
        acc = _sigmoid(acc)
    o_ref[...] = acc.astype(o_ref.dtype)


def _matmul_nt(a, bt, *, tn, out_dtype, act=None, name):
    m, k = a.shape
    n = bt.shape[0]
    tm = TM_MM
    return pl.pallas_call(
        functools.partial(_mm_kernel, act=act),
        grid=(m // tm, n // tn),
        in_specs=[pl.BlockSpec((tm, k), lambda i, j: (i, 0)),
                  pl.BlockSpec((tn, k), lambda i, j: (j, 0))],
        out_specs=pl.BlockSpec((tm, tn), lambda i, j: (i, j)),
        out_shape=jax.ShapeDtypeStruct((m, n), out_dtype),
        compiler_params=_params("parallel", "parallel"),
        name=name,
    )(a, bt)


def _pool_kernel(p_ref, halo_ref, w_ref, scale_ref, o_ref, *, seq, tm):
    t0 = (pl.program_id(0) * tm) % seq
    x = p_ref[...].astype(F32)
    halo = jnp.where(t0 == 0, 0.0, halo_ref[...].astype(F32))
    ext = jnp.concatenate([halo, x], axis=0)
    sums = [ext]
    for sh in (1, 2, 4, 8):
        sums.append(sums[-1] + pltpu.roll(sums[-1], sh, 0))
    t = t0 + _iota2((tm, POOL_GROUP), 0)
    ys = []
    for g, win in enumerate(POOL_WINDOWS):
        cols = slice(g * POOL_GROUP, (g + 1) * POOL_GROUP)
        cnt = jnp.minimum(t + 1, win).astype(F32)
        pooled = sums[g + 1][16:, cols] / cnt - x[:, cols]
        ys.append(_dot(pooled, w_ref[g]))
    o_ref[...] = (jnp.concatenate(ys, axis=1) * scale_ref[...]).astype(BF16)


def _pool(p, pool_w, pool_scale, seq):
    m = p.shape[0]
    tm = TM_LN
    return pl.pallas_call(
        functools.partial(_pool_kernel, seq=seq, tm=tm),
        grid=(m // tm,),
        in_specs=[pl.BlockSpec((tm, MIX), lambda i: (i, 0)),
                  pl.BlockSpec((16, MIX), lambda i: (jnp.maximum(i * (tm // 16) - 1, 0), 0)),
                  pl.BlockSpec(pool_w.shape, lambda i: (0, 0, 0)),
                  pl.BlockSpec((1, MIX), lambda i: (0, 0))],
        out_specs=pl.BlockSpec((tm, MIX), lambda i: (i, 0)),
        out_shape=jax.ShapeDtypeStruct((m, MIX), BF16),
        compiler_params=_params("parallel"),
        name="pool_mixer",
    )(p, p, pool_w.astype(BF16), pool_scale.reshape(1, MIX))


def _rwkv_prep_kernel(*refs, seq, tm, has_vres):
    (pr, pk, pv, hr, hk, hv, ps, hs, mur, muk, muv, mus, w0, a0, kkp, kap,
     w2p, a2p, g2p) = refs[:19]
    if has_vres:
        vf, v0, v1p, v2p = refs[19:23]
        outs = refs[23:]
    else:
        outs = refs[19:]
    r_o, k_o, v_o, kk_o, a_o, ld_o, g_o = outs

    first = (pl.program_id(0) * tm) % seq == 0

    def lerp(p_ref, h_ref, mu_ref):
        halo = jnp.where(first, 0.0, h_ref[...].astype(F32))
        if p_ref.dtype == BF16:
            (prev,) = _shift_rows_bf16(p_ref[...], halo, (1,))
        else:
            prev = _shift_rows(p_ref[...], halo, 1)
        p = p_ref[...].astype(F32)
        return p + (prev - p) * mu_ref[...]

    xr = lerp(pr, hr, mur)
    xk = lerp(pk, hk, muk)
    xv = lerp(pv, hv, muv)
    xs = lerp(ps, hs, mus)

    z = w0[...] + _dot(jnp.tanh(xs), w2p[...])
    t = jnp.exp(-jnp.abs(z))
    log_decay = jnp.where(z < 0.0, t, 1.0) * (-math.exp(-0.5) / (1.0 + t))
    a = _sigmoid(a0[...] + _dot(xs, a2p[...]))
    g = _dot(_sigmoid(xs), g2p[...])
    if has_vres:
        gate = _sigmoid(v0[...] + _dot(_dot(xv, v1p[...]), v2p[...]))
        xv = xv + (vf[...].astype(F32) - xv) * gate

    kk = xk * kkp[...]
    r2 = _iota2((LANES, LANES), 0)
    c2 = _iota2((LANES, LANES), 1)
    head_ones = ((r2 >> 6) == (c2 >> 6)).astype(BF16)
    kk2 = kk * kk
    ss = jnp.concatenate(
        [_dot(kk2[:, i * LANES:(i + 1) * LANES], head_ones) for i in range(MIX // LANES)],
        axis=1)
    kk = kk * lax.rsqrt(jnp.maximum(ss, 1e-24))

    r_o[...] = xr.astype(r_o.dtype)
    k_o[...] = (xk * (1.0 + (a - 1.0) * kap[...])).astype(k_o.dtype)
    v_o[...] = xv.astype(v_o.dtype)
    kk_o[...] = kk.astype(kk_o.dtype)
    a_o[...] = a.astype(a_o.dtype)
    ld_o[...] = _dot_sel_l(_chunk_cumsum_matrix(tm), log_decay, 2)
    g_o[...] = g.astype(g_o.dtype)


def _rwkv_prep(p, psm, wts, v_first, seq):
    m = p.shape[0]
    tm = TM_PREP
    has_vres = v_first is not None
    tile = lambda cb: pl.BlockSpec((tm, MIX), lambda i, cb=cb: (i, cb))
    halo = lambda cb: pl.BlockSpec((16, MIX), lambda i, cb=cb: (jnp.maximum(i * (tm // 16) - 1, 0), cb))
    vec = pl.BlockSpec((1, MIX), lambda i: (0, 0))
    full = lambda arr: pl.BlockSpec(arr.shape, lambda i: (0,) * arr.ndim)
    in_specs = [tile(1), tile(2), tile(3), halo(1), halo(2), halo(3),
                pl.BlockSpec((tm, N_SMALL), lambda i: (i, 0)),
                pl.BlockSpec((8, N_SMALL), lambda i: (jnp.maximum(i * (tm // 8) - 1, 0), 0)),
                vec, vec, vec, pl.BlockSpec((1, N_SMALL), lambda i: (0, 0)),
                vec, vec, vec, vec,
                full(wts["w2p"]), full(wts["a2p"]), full(wts["g2p"])]
    args = [p, p, p, p, p, p, psm, psm, wts["mu_r"], wts["mu_k"], wts["mu_v"], wts["mu_s"],
            wts["w0"], wts["a0"], wts["k_k"], wts["k_a"], wts["w2p"], wts["a2p"], wts["g2p"]]
    if has_vres:
        in_specs += [pl.BlockSpec((tm, MIX), lambda i: (i, 0)), vec, full(wts["v1p"]), full(wts["v2p"])]
        args += [v_first, wts["v0"], wts["v1p"], wts["v2p"]]
    out = pl.BlockSpec((tm, MIX), lambda i: (i, 0))
    return pl.pallas_call(
        functools.partial(_rwkv_prep_kernel, seq=seq, tm=tm, has_vres=has_vres),
        grid=(m // tm,),
        in_specs=in_specs,
        out_specs=[out] * 7,
        out_shape=[jax.ShapeDtypeStruct((m, MIX), dt) for dt in (BF16,) * 5 + (F32, BF16)],
        compiler_params=_params("parallel"),
        name="rwkv_prep",
    )(*args)


def _rwkv_core_kernel(r_ref, k_ref, v_ref, kk_ref, a_ref, gc_ref, g_ref, rk_ref, lnw_ref, lnb_ref,
                      o_ref, h_ref):
    @pl.when(pl.program_id(1) == 0)
    def _():
        h_ref[...] = jnp.zeros_like(h_ref)

    nb, rb, _ = r_ref.shape
    nch = rb // CHUNK
    units = [(b, c) for c in range(nch) for b in range(nb)]
    m = _pair_masks()
    head_ones = m["bd64"].astype(BF16)
    stack = lambda x: _stack(x, m)
    cat0 = lambda *xs: jnp.concatenate(xs, axis=0)
    cat1 = lambda *xs: jnp.concatenate(xs, axis=1)

    def load(ref, u):
        b, c = u
        return ref[b, c * CHUNK:(c + 1) * CHUNK, :].astype(F32)

    first_row = _iota2((CHUNK, LANES), 0) == 0
    q_t, y0, phi, psi = {}, {}, {}, {}

    def chunk_terms(group):
        a_hat, r_hat, lhs, rhs, bbar_t, kbar_t, dlast, vs = [], [], [], [], [], [], [], []
        for u in group:
            gc = load(gc_ref, u)
            g_prev = jnp.where(first_row, 0.0, pltpu.roll(gc, 1, 0))
            kc, vc, kkc = load(k_ref, u), load(v_ref, u), load(kk_ref, u)
            bc = kkc * load(a_ref, u)
            glast = gc[CHUNK - 1:CHUNK]
            e_neg = jnp.exp(-gc)
            e_rest = jnp.exp(glast - gc)
            a_hat.append(-kkc * jnp.exp(g_prev))
            r_hat.append(load(r_ref, u) * jnp.exp(gc))
            lhs.append(cat0(a_hat[-1], r_hat[-1]))
            rhs.append(cat0(stack(bc * e_neg), stack(kc * e_neg)))
            bbar_t.append((bc * e_rest).T)
            kbar_t.append((kc * e_rest).T)
            dlast.append(jnp.exp(glast))
            vs.append(vc)
            yield
        big = [_dot_nt(x, y) for x, y in zip(lhs, rhs)]
        yield
        a_ab = [jnp.where(m["strict"], x[:CHUNK, :LANES], 0.0) for x in big]
        a_kr = [cat0(jnp.where(m["strict"], x[:CHUNK, LANES:], 0.0),
                     jnp.where(m["incl"], x[CHUNK:, LANES:], 0.0)) for x in big]
        a_rb = [jnp.where(m["incl"], x[CHUNK:, :LANES], 0.0) for x in big]
        x1 = [_dot(a, stack(v)) for a, v in zip(a_kr, vs)]
        t_inv = yield from _tri_inv(a_ab, m)
        x2 = [_dot(t, cat1(stack(a), stack(x[:CHUNK]))) for t, a, x in zip(t_inv, a_hat, x1)]
        yield
        x3 = [_dot(a, cat1(stack(x[:, :LANES]), stack(x[:, LANES:]))) for a, x in zip(a_rb, x2)]
        for i, u in enumerate(group):
            w_t, u_t = x2[i][:, :LANES], x2[i][:, LANES:]
            phi[u] = m["eye128"] * dlast[i] + jnp.where(m["bd64"], _dot(bbar_t[i], w_t), 0.0)
            psi[u] = jnp.where(m["bd64"], _dot(cat1(bbar_t[i], kbar_t[i]), cat0(u_t, vs[i])), 0.0)
        yield
        for i, u in enumerate(group):
            q_t[u] = r_hat[i] + x3[i][:, :LANES]
            y0[u] = x3[i][:, LANES:] + x1[i][CHUNK:]

    _interleave([chunk_terms(units[i:i + CORE_GROUP]) for i in range(0, len(units), CORE_GROUP)],
                CORE_STAGGER)

    ys = {}
    hs = [h_ref[b] for b in range(nb)]
    for u in units:
        b = u[0]
        qh = _dot(cat0(q_t[u], phi[u]), hs[b])
        ys[u] = y0[u] + qh[:CHUNK]
        hs[b] = qh[CHUNK:] + psi[u]
    for b in range(nb):
        h_ref[b] = hs[b]

    y = cat0(*[ys[(b, c)] for b in range(nb) for c in range(nch)])
    flat = lambda ref: ref[...].astype(F32).reshape(nb * rb, LANES)
    rr, kk_, vv = flat(r_ref), flat(k_ref), flat(v_ref)
    sums = _dot(cat0(y, rr * kk_ * rk_ref[...]), head_ones)
    yc = y - sums[:nb * rb] * (1.0 / RWKV_HEAD)
    var = _dot(yc * yc, head_ones) * (1.0 / RWKV_HEAD)
    yn = yc * lax.rsqrt(var + RWKV_LN_EPS) * lnw_ref[...] + lnb_ref[...]
    bonus = sums[nb * rb:] * vv
    o_ref[...] = ((yn + bonus) * flat(g_ref)).astype(BF16).reshape(nb, rb, LANES)


def _rwkv_core(prep, wts, batch, seq):
    rb = RB_RWKV
    tile = pl.BlockSpec((batch, rb, LANES), lambda p, j: (0, j, p))
    vec = pl.BlockSpec((1, LANES), lambda p, j: (0, p))
    out = pl.pallas_call(
        _rwkv_core_kernel,
        grid=(MIX // LANES, seq // rb),
        in_specs=[tile] * 7 + [vec] * 3,
        out_specs=tile,
        out_shape=jax.ShapeDtypeStruct((batch, seq, MIX), BF16),
        scratch_shapes=[pltpu.VMEM((batch, LANES, LANES), F32)],
        compiler_params=_params("parallel", "arbitrary"),
        name="rwkv_core",
    )(*[a.reshape(batch, seq, MIX) for a in prep], wts["r_k"], wts["ln_w"], wts["ln_b"])
    return out.reshape(batch * seq, MIX)


def _gdn_prep_kernel(pq, pk, pv, hq, hk, hv, ps, cwq, cwk, cwv, alog, dtb,
                     q_o, k_o, v_o, gb_o, *, seq, tm):
    first = (pl.program_id(0) * tm) % seq == 0
    ones = jnp.ones((LANES, LANES), BF16)

    def conv_silu(p_ref, h_ref, w_ref):
        halo = jnp.where(first, 0.0, h_ref[...].astype(F32))
        w = w_ref[...]
        y = p_ref[...].astype(F32) * w[3:4]
        for s, xs in zip((1, 2, 3), _shift_rows_bf16(p_ref[...], halo, (1, 2, 3))):
            y = y + xs * w[3 - s:4 - s]
        return _silu(y)

    def l2norm(y, scale):
        blocks = []
        for i in range(GDN_HEADS):
            blk = y[:, i * LANES:(i + 1) * LANES]
            ss = _dot(blk * blk, ones)
            blocks.append(blk * (lax.rsqrt(ss + GDN_EPS) * scale))
        return jnp.concatenate(blocks, axis=1)

    q_o[...] = l2norm(conv_silu(pq, hq, cwq), GDN_HEAD ** -0.5).astype(q_o.dtype)
    k_o[...] = l2norm(conv_silu(pk, hk, cwk), 1.0).astype(k_o.dtype)
    v_o[...] = conv_silu(pv, hv, cwv).astype(v_o.dtype)

    raw = ps[:, N_SMALL - LANES:]
    gval = -jnp.exp(alog[...]) * _softplus(raw + dtb[...])
    gcum = _dot_sel_l(_chunk_cumsum_matrix(tm), gval, 3)
    lane = _iota2(raw.shape, 1)
    is_g = (lane >= GDN_G_LANE) & (lane < GDN_G_LANE + GDN_HEADS)
    is_b = (lane >= GDN_B_LANE) & (lane < GDN_B_LANE + GDN_HEADS)
    gb_o[...] = jnp.where(is_g, gcum, jnp.where(is_b, _sigmoid(raw), 0.0))


def _gdn_prep(p, psm, wts, seq):
    m = p.shape[0]
    tm = TM_PREP
    tile = lambda cb: pl.BlockSpec((tm, MIX), lambda i, cb=cb: (i, cb))
    halo = lambda cb: pl.BlockSpec((16, MIX), lambda i, cb=cb: (jnp.maximum(i * (tm // 16) - 1, 0), cb))
    cw = pl.BlockSpec((4, MIX), lambda i: (0, 0))
    lv = pl.BlockSpec((1, LANES), lambda i: (0, 0))
    out = pl.BlockSpec((tm, MIX), lambda i: (i, 0))
    return pl.pallas_call(
        functools.partial(_gdn_prep_kernel, seq=seq, tm=tm),
        grid=(m // tm,),
        in_specs=[tile(4), tile(5), tile(6), halo(4), halo(5), halo(6),
                  pl.BlockSpec((tm, N_SMALL), lambda i: (i, 0)), cw, cw, cw, lv, lv],
        out_specs=[out, out, out, pl.BlockSpec((tm, LANES), lambda i: (i, 0))],
        out_shape=[jax.ShapeDtypeStruct((m, MIX), BF16)] * 3 + [jax.ShapeDtypeStruct((m, LANES), F32)],
        compiler_params=_params("parallel"),
        name="gdn_prep",
    )(p, p, p, p, p, p, psm, wts["cw_q"], wts["cw_k"], wts["cw_v"], wts["alog"], wts["dtb"])


def _gdn_core_kernel(q_ref, k_ref, v_ref, gb_ref, z_ref, nw_ref, o_ref, s_ref):
    @pl.when(pl.program_id(1) == 0)
    def _():
        s_ref[...] = jnp.zeros_like(s_ref)

    nb, rb, width = q_ref.shape
    nh = width // LANES
    units = [(b, hh) for hh in range(nh) for b in range(nb)]
    assert rb == 2 * CHUNK
    m = _pair_masks()
    ones = jnp.ones((LANES, LANES), BF16)
    cat0 = lambda *xs: jnp.concatenate(xs, axis=0)
    cat1 = lambda *xs: jnp.concatenate(xs, axis=1)

    def lane_bcast(x, lane):
        return jnp.broadcast_to(pltpu.roll(x, LANES - lane, 1)[:, 0:1], x.shape)

    nch = rb // CHUNK
    q_t, y0, phi, psi = {}, {}, {}, {}

    def tile_terms(group):
        gf, ks, lhs, kbd, rhs, e_g, qs, gamma = [], [], [], [], [], [], [], []
        zero = jnp.zeros((CHUNK, LANES), F32)
        for b, hh in group:
            hl = slice(hh * LANES, (hh + 1) * LANES)
            head = nh * pl.program_id(0) + hh
            gbv = gb_ref[b]
            g = lane_bcast(gbv, GDN_G_LANE + head)
            beta = lane_bcast(gbv, GDN_B_LANE + head)
            q, k, v = (r[b, :, hl].astype(F32) for r in (q_ref, k_ref, v_ref))
            g_col = g[:CHUNK] * m["half0"] + g[CHUNK:] * m["half1"]
            diff = g_col - g.T[:CHUNK]
            gamma.append(jnp.where(m["incl"], jnp.exp(jnp.where(m["incl"], diff, 0.0)), 0.0))
            eg = jnp.exp(g)
            kb = k * beta
            gf.append(g)
            ks.append(k)
            qs.append(q)
            e_g.append(eg)
            lhs.append(cat0(cat1(kb[:CHUNK], kb[CHUNK:]), cat1(q[:CHUNK], q[CHUNK:])))
            kbd.append(cat0(cat1(k[:CHUNK], zero), cat1(zero, k[CHUNK:])))
            rhs.append(cat1(v * beta, kb * eg))
            yield
        kq = [_dot_nt(x, y) for x, y in zip(lhs, kbd)]
        yield
        a_neg = [-jnp.where(m["strict"], x[:CHUNK] * gm, 0.0) for x, gm in zip(kq, gamma)]
        a_qk = [x[CHUNK:] * gm for x, gm in zip(kq, gamma)]
        k_bar_t = [[(k[c * CHUNK:(c + 1) * CHUNK]
                     * jnp.exp(g[(c + 1) * CHUNK - 1:(c + 1) * CHUNK] - g[c * CHUNK:(c + 1) * CHUNK])).T
                    for c in range(nch)] for k, g in zip(ks, gf)]
        t_inv = yield from _tri_inv(a_neg, m)
        uw = [_dot(_stack(t, m), x) for t, x in zip(t_inv, rhs)]
        yield
        x3 = [_dot(_stack(a, m), x) for a, x in zip(a_qk, uw)]
        for i, u in enumerate(group):
            for c in range(nch):
                glast = gf[i][(c + 1) * CHUNK - 1:(c + 1) * CHUNK]
                pp = _dot(k_bar_t[i][c], uw[i][c * CHUNK:(c + 1) * CHUNK])
                phi[(u, c)] = m["eye128"] * jnp.exp(glast) - pp[:, LANES:]
                psi[(u, c)] = pp[:, :LANES]
        yield
        for i, u in enumerate(group):
            y0[u] = x3[i][:, :LANES]
            q_t[u] = qs[i] * e_g[i] - x3[i][:, LANES:]

    _interleave([tile_terms(units[i:i + CORE_GROUP]) for i in range(0, len(units), CORE_GROUP)],
                CORE_STAGGER)

    ys = {u: [] for u in units}
    st = {(b, hh): s_ref[hh, b] for b, hh in units}
    for c in range(nch):
        cs = slice(c * CHUNK, (c + 1) * CHUNK)
        for u in units:
            qs_ = _dot(cat0(q_t[u][cs], phi[(u, c)]), st[u])
            ys[u].append(y0[u][cs] + qs_[:CHUNK])
            st[u] = qs_[CHUNK:] + psi[(u, c)]
    for b, hh in units:
        s_ref[hh, b] = st[(b, hh)]

    for hh in range(nh):
        hl = slice(hh * LANES, (hh + 1) * LANES)
        y = cat0(*[yc for b in range(nb) for yc in ys[(b, hh)]])
        ms = _dot(y * y, ones) * (1.0 / GDN_HEAD)
        out = y * lax.rsqrt(ms + GDN_EPS) * nw_ref[...]
        z = z_ref[:, :, hl].astype(F32).reshape(nb * rb, LANES)
        o_ref[:, :, hl] = (out * _silu(z)).astype(BF16).reshape(nb, rb, LANES)


def _gdn_core(q, k, v, gb, p, norm_w, batch, seq):
    rb, nh = RB_GDN, GDN_HEADS_PER_STEP
    width = nh * LANES
    tile = pl.BlockSpec((batch, rb, width), lambda h, j: (0, j, h))
    z_col0 = 7 * (MIX // width)
    r3 = lambda a: a.reshape(batch, seq, a.shape[-1])
    out = pl.pallas_call(
        _gdn_core_kernel,
        grid=(GDN_HEADS // nh, seq // rb),
        in_specs=[tile, tile, tile,
                  pl.BlockSpec((batch, rb, LANES), lambda h, j: (0, j, 0)),
                  pl.BlockSpec((batch, rb, width), lambda h, j: (0, j, z_col0 + h)),
                  pl.BlockSpec((1, LANES), lambda h, j: (0, 0))],
        out_specs=tile,
        out_shape=jax.ShapeDtypeStruct((batch, seq, MIX), BF16),
        scratch_shapes=[pltpu.VMEM((nh, batch, LANES, LANES), F32)],
        compiler_params=_params("parallel", "arbitrary"),
        name="gdn_core",
    )(r3(q), r3(k), r3(v), r3(gb), r3(p), norm_w.reshape(1, LANES))
    return out.reshape(batch * seq, MIX)


def _merge_kernel(ya, yb, yc, wa, wb, wc, ga, gb, gc, o_ref):
    dot = lambda y, w: jnp.dot(y[...], w[0], preferred_element_type=F32)
    acc = ga[...].astype(F32) * dot(ya, wa)
    acc = acc + gb[...].astype(F32) * dot(yb, wb)
    acc = acc + gc[...].astype(F32) * dot(yc, wc)
    o_ref[...] = acc.astype(BF16)


def _merge(ya, yb, yc, wa, wb, wc, l, gates):
    m = ya.shape[0]
    d = wa.shape[2]
    tm = TM_LN
    y = pl.BlockSpec((tm, MIX), lambda i: (i, 0))
    w = pl.BlockSpec((1, MIX, d), lambda i: (l, 0, 0), pipeline_mode=pl.Buffered(1))
    g = lambda br: pl.BlockSpec((tm, d), lambda i, br=br: (i, br))
    return pl.pallas_call(
        _merge_kernel,
        grid=(m // tm,),
        in_specs=[y, y, y, w, w, w, g(0), g(1), g(2)],
        out_specs=pl.BlockSpec((tm, d), lambda i: (i, 0)),
        out_shape=jax.ShapeDtypeStruct((m, d), BF16),
        compiler_params=_params("parallel"),
        name="merge",
    )(ya, yb, yc, wa, wb, wc, gates, gates, gates)


def _proj_ln_kernel(*refs, emit_h):
    a_ref, w_ref, x_ref, gt_ref, lnw_ref, lnb_ref = refs[:6]
    y = jnp.dot(a_ref[...], w_ref[0], preferred_element_type=F32)
    xn = _resid_ln(x_ref[...], y, gt_ref[0], lnw_ref[...], lnb_ref[...])
    if emit_h:
        sc_ref, sh_ref, xo_ref, ho_ref = refs[6:]
        ho_ref[...] = (xn * (1.0 + sc_ref[0]) + sh_ref[0]).astype(BF16)
    else:
        (xo_ref,) = refs[6:]
    xo_ref[...] = xn


def _proj_ln(a, w, l, x, gt, lnw, lnb, nxt, seq, *, tm, name):
    m, k = a.shape
    d = w.shape[2]
    per = seq // tm
    emit_h = nxt is not None
    row = pl.BlockSpec((tm, d), lambda i: (i, 0))
    mod = pl.BlockSpec((1, 1, d), lambda i: (i // per, 0, 0))
    vec = pl.BlockSpec((1, d), lambda i: (0, 0))
    in_specs = [pl.BlockSpec((tm, k), lambda i: (i, 0)),
                pl.BlockSpec((1, k, d), lambda i: (l, 0, 0), pipeline_mode=pl.Buffered(1)),
                row, mod, vec, vec]
    args = [a, w, x, gt, lnw.reshape(1, d), lnb.reshape(1, d)]
    out_specs, out_shape = [row], [jax.ShapeDtypeStruct((m, d), F32)]
    if emit_h:
        in_specs += [mod, mod]
        args += list(nxt)
        out_specs.append(row)
        out_shape.append(jax.ShapeDtypeStruct((m, d), BF16))
    res = pl.pallas_call(
        functools.partial(_proj_ln_kernel, emit_h=emit_h),
        grid=(m // tm,),
        in_specs=in_specs,
        out_specs=out_specs,
        out_shape=out_shape,
        compiler_params=_params("parallel"),
        name=name,
    )(*args)
    return (res[0], res[1]) if emit_h else (res[0], None)


def _swiglu_kernel(a_ref, wg_ref, wu_ref, o_ref):
    a = a_ref[...]
    g = jnp.dot(a, wg_ref[0], preferred_element_type=F32)
    u = jnp.dot(a, wu_ref[0], preferred_element_type=F32)
    o_ref[...] = (_silu(g) * u).astype(BF16)


def _swiglu(h, w_up, l):
    m, k = h.shape
    dff = w_up.shape[2] // 2
    tm, tn = TM_MM, 512
    nj = dff // tn
    return pl.pallas_call(
        _swiglu_kernel,
        grid=(m // tm, nj),
        in_specs=[pl.BlockSpec((tm, k), lambda i, j: (i, 0)),
                  pl.BlockSpec((1, k, tn), lambda i, j: (l, 0, j)),
                  pl.BlockSpec((1, k, tn), lambda i, j: (l, 0, nj + j))],
        out_specs=pl.BlockSpec((tm, tn), lambda i, j: (i, j)),
        out_shape=jax.ShapeDtypeStruct((m, dff), BF16),
        compiler_params=_params("parallel", "parallel"),
        name="ffn_up_swiglu",
    )(h, w_up, w_up)


def _layer_weights(l, w_in_t, rwkv_mu, rwkv_w0, rwkv_w2, rwkv_a0, rwkv_a2, rwkv_g2, rwkv_k_k, rwkv_k_a,
                   rwkv_r_k, rwkv_ln_w, rwkv_ln_b, rwkv_v0, rwkv_v1, rwkv_v2, gdn_conv_w, gdn_a_log,
                   gdn_dt_bias):
    c = MIX
    rows = lambda lo, hi: w_in_t[l, lo:hi].astype(BF16)
    o_rwkv = c
    o_lora = o_rwkv + 3 * c
    n_lora = rwkv_w2.shape[1] + rwkv_a2.shape[1] + rwkv_g2.shape[1]
    o_gdn = o_lora + n_lora
    o_ab = o_gdn + 4 * c
    o_gate = o_ab + 2 * GDN_HEADS
    pad_rows = N_SMALL - n_lora - 2 * GDN_HEADS
    w_main = jnp.concatenate([rows(0, o_lora), rows(o_gdn, o_ab)], axis=0)
    assert n_lora - (N_SMALL - LANES) == GDN_G_LANE and GDN_B_LANE == GDN_G_LANE + GDN_HEADS
    w_small = jnp.concatenate([rows(o_lora, o_gdn), rows(o_ab, o_gate),
                               jnp.zeros((pad_rows, w_in_t.shape[2]), BF16)], axis=0)
    w_gate = rows(o_gate, w_in_t.shape[1])

    mu = rwkv_mu[l]
    row = lambda v: v.reshape(1, -1)
    n_w, n_a, n_g = rwkv_w2.shape[1], rwkv_a2.shape[1], rwkv_g2.shape[1]
    padrows = lambda w, lo: jnp.zeros((N_SMALL, c), F32).at[lo:lo + w.shape[0]].set(w).astype(BF16)
    lane_vec = lambda v, lo: jnp.zeros((1, LANES), F32).at[0, lo:lo + v.shape[0]].set(v)
    rw = {
        "mu_r": row(mu[:c]), "mu_k": row(mu[c:2 * c]), "mu_v": row(mu[2 * c:3 * c]),
        "mu_s": jnp.zeros((1, N_SMALL), F32).at[0, :n_lora].set(mu[3 * c:]),
        "w0": row(rwkv_w0[l]), "a0": row(rwkv_a0[l]), "k_k": row(rwkv_k_k[l]), "k_a": row(rwkv_k_a[l]),
        "w2p": padrows(rwkv_w2[l], 0), "a2p": padrows(rwkv_a2[l], n_w), "g2p": padrows(rwkv_g2[l], n_w + n_a),
        "r_k": row(rwkv_r_k[l]), "ln_w": row(rwkv_ln_w[l]), "ln_b": row(rwkv_ln_b[l]),
    }
    if l > 0:
        nv = rwkv_v1.shape[2]
        rw["v0"] = row(rwkv_v0[l - 1])
        rw["v1p"] = jnp.zeros((c, LANES), F32).at[:, :nv].set(rwkv_v1[l - 1]).astype(BF16)
        rw["v2p"] = jnp.zeros((LANES, c), F32).at[:nv].set(rwkv_v2[l - 1]).astype(BF16)
    cw = gdn_conv_w[l]
    gd = {"cw_q": cw[:, :c], "cw_k": cw[:, c:2 * c], "cw_v": cw[:, 2 * c:],
          "alog": lane_vec(gdn_a_log[l], GDN_G_LANE), "dtb": lane_vec(gdn_dt_bias[l], GDN_G_LANE)}
    return w_main, w_small, w_gate, rw, gd


def kernel(x, c, ada_w, ada_b, w_in, pool_w, pool_scale, rwkv_mu, rwkv_w0, rwkv_w2, rwkv_a0, rwkv_a2, rwkv_g2, rwkv_k_k, rwkv_k_a, rwkv_r_k, rwkv_ln_w, rwkv_ln_b, rwkv_v0, rwkv_v1, rwkv_v2, gdn_conv_w, gdn_a_log, gdn_dt_bias, gdn_norm_w, w_branch_a, w_branch_b, w_branch_c, w_out, ln1_w, ln1_b, ffn_w_up, ffn_w_down, ln2_w, ln2_b):
    batch, seq, d = x.shape
    assert seq % TM_MM == 0 and d == 2 * MIX
    m = batch * seq
    xf = x.reshape(m, d)

    mod = _ada(c, ada_w, ada_b)
    mods = [[mod[l, :, i * d:(i + 1) * d].reshape(batch, 1, d) for i in range(6)]
            for l in range(DEPTH)]

    h = _modulate(xf, mods[0][1], mods[0][0], seq)
    w_in_t = jnp.swapaxes(w_in, 1, 2)
    wa, wb, wc = (w.astype(BF16) for w in (w_branch_a, w_branch_b, w_branch_c))
    w_out_b, w_up_b, w_down_b = (w.astype(BF16) for w in (w_out, ffn_w_up, ffn_w_down))
    v_first = None
    for l in range(DEPTH):
        sh_m, sc_m, gt_m, sh_f, sc_f, gt_f = mods[l]
        w_main, w_small, w_gate, rw, gd = _layer_weights(
            l, w_in_t, rwkv_mu, rwkv_w0, rwkv_w2, rwkv_a0, rwkv_a2, rwkv_g2, rwkv_k_k, rwkv_k_a,
            rwkv_r_k, rwkv_ln_w, rwkv_ln_b, rwkv_v0, rwkv_v1, rwkv_v2, gdn_conv_w, gdn_a_log,
            gdn_dt_bias)

        p = _matmul_nt(h, w_main, tn=2048, out_dtype=BF16, name="in_proj_main")
        psm = _matmul_nt(h, w_small, tn=N_SMALL, out_dtype=F32, name="in_proj_small")
        gates = _matmul_nt(h, w_gate, tn=2048, out_dtype=BF16, act="sigmoid", name="in_proj_gates")

        y_a = _pool(p, pool_w[l], pool_scale[l], seq)
        prep = _rwkv_prep(p, psm, rw, v_first, seq)
        if l == 0:
            v_first = prep[2]
        y_b = _rwkv_core(prep, rw, batch, seq)
        gq, gk, gv, ggb = _gdn_prep(p, psm, gd, seq)
        y_c = _gdn_core(gq, gk, gv, ggb, p, gdn_norm_w[l], batch, seq)

        merged = _merge(y_a, y_b, y_c, wa, wb, wc, l, gates)
        xf, h = _proj_ln(merged, w_out_b, l, xf, gt_m, ln1_w[l], ln1_b[l],
                         (sc_f, sh_f), seq, tm=512, name="out_proj_ln")

        act = _swiglu(h, w_up_b, l)
        nxt = (mods[l + 1][1], mods[l + 1][0]) if l + 1 < DEPTH else None
        xf, h = _proj_ln(act, w_down_b, l, xf, gt_f, ln2_w[l], ln2_b[l],
                         nxt, seq, tm=256, name="ffn_down_ln")
    return xf.reshape(batch, seq, d)
```

```python
import functools
import math

import jax
import jax.numpy as jnp
from jax import lax
from jax.experimental import pallas as pl
from jax.experimental.pallas import tpu as pltpu

F32 = jnp.float32
BF16 = jnp.bfloat16

DEPTH = 2
DN_ALPHA = (2 * DEPTH) ** 0.25
LN_EPS = 1e-5
MIX = 1024
POOL_WINDOWS = (2, 4, 8, 16)
POOL_GROUP = 256
RWKV_HEAD = 64
RWKV_LN_EPS = RWKV_HEAD * 1e-5
GDN_HEAD = 128
GDN_HEADS = 8
GDN_EPS = 1e-6
CHUNK = 64
LANES = 128
N_SMALL = 384
GDN_G_LANE = 32
GDN_B_LANE = 40
VMEM_LIMIT = 56 * 1024 * 1024

TM_MM = 1024
TM_LN = 512
TM_PREP = 256
RB_RWKV = 512
RB_GDN = 128
GDN_HEADS_PER_STEP = 8


def _dot(a, b):
    return jnp.dot(a.astype(BF16), b.astype(BF16), preferred_element_type=F32)


def _dot_nt(a, b):
    return lax.dot_general(a.astype(BF16), b.astype(BF16), (((1,), (1,)), ((), ())),
                           preferred_element_type=F32)


def _split(a, n):
    terms, rest = [], a
    for _ in range(n):
        t = rest.astype(BF16)
        terms.append(t)
        rest = rest - t.astype(F32)
    return terms


def _dot3(a, b, nt=False):
    d = _dot_nt if nt else _dot
    a1, a2 = _split(a, 2)
    b1, b2 = _split(b, 2)
    return d(a1, b1) + (d(a1, b2) + d(a2, b1))


def _dot_sel_l(sel, b, n):
    out = None
    for t in _split(b, n):
        p = jnp.dot(sel, t, preferred_element_type=F32)
        out = p if out is None else out + p
    return out


def _sigmoid(x):
    return 0.5 * jnp.tanh(0.5 * x) + 0.5


def _silu(x):
    return x * _sigmoid(x)


def _softplus(x):
    return jnp.maximum(x, 0.0) + jnp.log(1.0 + jnp.exp(-jnp.abs(x)))


def _iota2(shape, dim):
    return lax.broadcasted_iota(jnp.int32, shape, dim)


def _pair_masks():
    r = _iota2((CHUNK, LANES), 0)
    c = _iota2((CHUNK, LANES), 1) & (CHUNK - 1)
    same = lambda s: (r >> s) == (c >> s)
    r2 = _iota2((LANES, LANES), 0)
    c2 = _iota2((LANES, LANES), 1)
    lane = _iota2((1, LANES), 1)
    return {
        "eye": (r == c).astype(F32),
        "bd8": same(3),
        "off16": same(4) & jnp.logical_not(same(3)),
        "off32": same(5) & jnp.logical_not(same(4)),
        "off64": jnp.logical_not(same(5)),
        "strict": c < r,
        "incl": c <= r,
        "eye128": (r2 == c2).astype(F32),
        "bd64": (r2 >> 6) == (c2 >> 6),
        "low": _iota2((CHUNK, LANES), 1) < CHUNK,
        "half0": (lane < CHUNK).astype(F32),
        "half1": (lane >= CHUNK).astype(F32),
    }


def _stack(x, m):
    xb = x.astype(BF16)
    zero = jnp.zeros_like(xb)
    return jnp.concatenate([jnp.where(m["low"], xb, zero), jnp.where(m["low"], zero, xb)], axis=0)


def _chunk_cumsum_matrix(n):
    r = _iota2((n, n), 0)
    c = _iota2((n, n), 1)
    return (((r >> 6) == (c >> 6)) & (c <= r)).astype(BF16)


def _tri_inv(mats, m):
    bd = lambda x: _stack(x, m)
    a8 = [jnp.where(m["bd8"], a, 0.0) for a in mats]
    s = [m["eye"] + x for x in a8]
    p = [_dot(x, bd(x)) for x in a8]
    q = [_dot(pi, jnp.concatenate([bd(si), bd(pi)], axis=1)) for pi, si in zip(p, s)]
    s = [si + qi[:, :LANES] for si, qi in zip(s, q)]
    s = [si + _dot(qi[:, LANES:], bd(si)) for si, qi in zip(s, q)]
    for key in ("off16", "off32", "off64"):
        t = [_dot(jnp.where(m[key], a, 0.0), bd(si)) for a, si in zip(mats, s)]
        s = [si + _dot(si, bd(ti)) for si, ti in zip(s, t)]
    return s


def _shift_rows(x, halo, s):
    halo = halo[halo.shape[0] - 8:]
    xr = pltpu.roll(x, s, 0)
    hr = pltpu.roll(halo, s, 0)
    rows = _iota2(halo.shape, 0)
    top = jnp.where(rows < s, hr, xr[:8])
    return top if x.shape[0] == 8 else jnp.concatenate([top, xr[8:]], axis=0)


def _shift_rows_bf16(x, halo, shifts):
    tm = x.shape[0]
    r = _iota2((tm, tm), 0)
    c = _iota2((tm, tm), 1)
    sel = jnp.concatenate([(r - c == s).astype(BF16) for s in shifts], axis=0)
    moved = jnp.dot(sel, x, preferred_element_type=F32)
    x_top = x[:16].astype(F32)[:8]
    outs = []
    for j, s in enumerate(shifts):
        top = _shift_rows(x_top, halo, s)
        outs.append(jnp.concatenate([top, moved[j * tm + 8:(j + 1) * tm]], axis=0))
    return outs


def _resid_ln(x, y, gt, lnw, lnb):
    z = DN_ALPHA * x + (1.0 + gt) * y
    mu = jnp.mean(z, axis=-1, keepdims=True)
    zc = z - mu
    var = jnp.mean(zc * zc, axis=-1, keepdims=True)
    return zc * lax.rsqrt(var + LN_EPS) * lnw + lnb


def _params(*sem):
    return pltpu.CompilerParams(dimension_semantics=sem, vmem_limit_bytes=VMEM_LIMIT)


def _ada_kernel(c_ref, w_ref, b_ref, o_ref):
    c = c_ref[...]
    o_ref[0] = _dot3(_silu(c), w_ref[0]) + b_ref[0]


def _ada(c, ada_w, ada_b):
    nl, d, n = ada_w.shape
    b = c.shape[0]
    tn = 512
    cp = jnp.zeros((8, d), F32).at[:b].set(c)
    out = pl.pallas_call(
        _ada_kernel,
        grid=(nl, n // tn),
        in_specs=[pl.BlockSpec((8, d), lambda l, j: (0, 0)),
                  pl.BlockSpec((1, d, tn), lambda l, j: (l, 0, j)),
                  pl.BlockSpec((1, 1, tn), lambda l, j: (l, 0, j))],
        out_specs=pl.BlockSpec((1, 8, tn), lambda l, j: (l, 0, j)),
        out_shape=jax.ShapeDtypeStruct((nl, 8, n), F32),
        compiler_params=_params("parallel", "parallel"),
        name="ada_mod",
    )(cp, ada_w, ada_b.reshape(nl, 1, n))
    return out[:, :b]


def _modulate_kernel(x_ref, sc_ref, sh_ref, h_ref):
    h_ref[...] = (x_ref[...] * (1.0 + sc_ref[0]) + sh_ref[0]).astype(BF16)


def _modulate(x, sc, sh, seq):
    m, d = x.shape
    tm = TM_LN
    per = seq // tm
    mod = pl.BlockSpec((1, 1, d), lambda i: (i // per, 0, 0))
    return pl.pallas_call(
        _modulate_kernel,
        grid=(m // tm,),
        in_specs=[pl.BlockSpec((tm, d), lambda i: (i, 0)), mod, mod],
        out_specs=pl.BlockSpec((tm, d), lambda i: (i, 0)),
        out_shape=jax.ShapeDtypeStruct((m, d), BF16),
        compiler_params=_params("parallel"),
        name="modulate",
    )(x, sc, sh)


def _mm_kernel(a_ref, bt_ref, o_ref, *, act):
    acc = lax.dot_general(a_ref[...], bt_ref[...], (((1,), (1,)), ((), ())),
                          preferred_element_type=F32)
    if act == "sigmoid":
        acc = _sigmoid(acc)
    o_ref[...] = acc.astype(o_ref.dtype)


def _matmul_nt(a, bt, *, tn, out_dtype, act=None, name):
    m, k = a.shape
    n = bt.shape[0]
    tm = TM_MM
    return pl.pallas_call(
        functools.partial(_mm_kernel, act=act),
        grid=(m // tm, n // tn),
        in_specs=[pl.BlockSpec((tm, k), lambda i, j: (i, 0)),
                  pl.BlockSpec((tn, k), lambda i, j: (j, 0))],
        out_specs=pl.BlockSpec((tm, tn), lambda i, j: (i, j)),
        out_shape=jax.ShapeDtypeStruct((m, n), out_dtype),
        compiler_params=_params("parallel", "parallel"),
        name=name,
    )(a, bt)


def _pool_kernel(p_ref, halo_ref, w_ref, scale_ref, o_ref, *, seq, tm):
    t0 = (pl.program_id(0) * tm) % seq
    x = p_ref[...].astype(F32)
    halo = jnp.where(t0 == 0, 0.0, halo_ref[...].astype(F32))
    ext = jnp.concatenate([halo, x], axis=0)
    sums = [ext]
    for sh in (1, 2, 4, 8):
        sums.append(sums[-1] + pltpu.roll(sums[-1], sh, 0))
    t = t0 + _iota2((tm, POOL_GROUP), 0)
    ys = []
    for g, win in enumerate(POOL_WINDOWS):
        cols = slice(g * POOL_GROUP, (g + 1) * POOL_GROUP)
        cnt = jnp.minimum(t + 1, win).astype(F32)
        pooled = sums[g + 1][16:, cols] / cnt - x[:, cols]
        ys.append(_dot(pooled, w_ref[g]))
    o_ref[...] = (jnp.concatenate(ys, axis=1) * scale_ref[...]).astype(BF16)


def _pool(p, pool_w, pool_scale, seq):
    m = p.shape[0]
    tm = TM_LN
    return pl.pallas_call(
        functools.partial(_pool_kernel, seq=seq, tm=tm),
        grid=(m // tm,),
        in_specs=[pl.BlockSpec((tm, MIX), lambda i: (i, 0)),
                  pl.BlockSpec((16, MIX), lambda i: (jnp.maximum(i * (tm // 16) - 1, 0), 0)),
                  pl.BlockSpec(pool_w.shape, lambda i: (0, 0, 0)),
                  pl.BlockSpec((1, MIX), lambda i: (0, 0))],
        out_specs=pl.BlockSpec((tm, MIX), lambda i: (i, 0)),
        out_shape=jax.ShapeDtypeStruct((m, MIX), BF16),
        compiler_params=_params("parallel"),
        name="pool_mixer",
    )(p, p, pool_w.astype(BF16), pool_scale.reshape(1, MIX))


def _rwkv_prep_kernel(*refs, seq, tm, has_vres):
    (pr, pk, pv, hr, hk, hv, ps, hs, mur, muk, muv, mus, w0, a0, kkp, kap,
     w2p, a2p, g2p) = refs[:19]
    if has_vres:
        vf, v0, v1p, v2p = refs[19:23]
        outs = refs[23:]
    else:
        outs = refs[19:]
    r_o, k_o, v_o, kk_o, a_o, ld_o, g_o = outs

    first = (pl.program_id(0) * tm) % seq == 0

    def lerp(p_ref, h_ref, mu_ref):
        halo = jnp.where(first, 0.0, h_ref[...].astype(F32))
        if p_ref.dtype == BF16:
            (prev,) = _shift_rows_bf16(p_ref[...], halo, (1,))
        else:
            prev = _shift_rows(p_ref[...], halo, 1)
        p = p_ref[...].astype(F32)
        return p + (prev - p) * mu_ref[...]

    xr = lerp(pr, hr, mur)
    xk = lerp(pk, hk, muk)
    xv = lerp(pv, hv, muv)
    xs = lerp(ps, hs, mus)

    z = w0[...] + _dot(jnp.tanh(xs), w2p[...])
    t = jnp.exp(-jnp.abs(z))
    log_decay = jnp.where(z < 0.0, t, 1.0) * (-math.exp(-0.5) / (1.0 + t))
    a = _sigmoid(a0[...] + _dot(xs, a2p[...]))
    g = _dot(_sigmoid(xs), g2p[...])
    if has_vres:
        gate = _sigmoid(v0[...] + _dot(_dot(xv, v1p[...]), v2p[...]))
        xv = xv + (vf[...].astype(F32) - xv) * gate

    kk = xk * kkp[...]
    r2 = _iota2((LANES, LANES), 0)
    c2 = _iota2((LANES, LANES), 1)
    head_ones = ((r2 >> 6) == (c2 >> 6)).astype(BF16)
    kk2 = kk * kk
    ss = jnp.concatenate(
        [_dot(kk2[:, i * LANES:(i + 1) * LANES], head_ones) for i in range(MIX // LANES)],
        axis=1)
    kk = kk * lax.rsqrt(jnp.maximum(ss, 1e-24))

    r_o[...] = xr.astype(r_o.dtype)
    k_o[...] = (xk * (1.0 + (a - 1.0) * kap[...])).astype(k_o.dtype)
    v_o[...] = xv.astype(v_o.dtype)
    kk_o[...] = kk.astype(kk_o.dtype)
    a_o[...] = a.astype(a_o.dtype)
    ld_o[...] = _dot_sel_l(_chunk_cumsum_matrix(tm), log_decay, 2)
    g_o[...] = g.astype(g_o.dtype)


def _rwkv_prep(p, psm, wts, v_first, seq):
    m = p.shape[0]
    tm = TM_PREP
    has_vres = v_first is not None
    tile = lambda cb: pl.BlockSpec((tm, MIX), lambda i, cb=cb: (i, cb))
    halo = lambda cb: pl.BlockSpec((16, MIX), lambda i, cb=cb: (jnp.maximum(i * (tm // 16) - 1, 0), cb))
    vec = pl.BlockSpec((1, MIX), lambda i: (0, 0))
    full = lambda arr: pl.BlockSpec(arr.shape, lambda i: (0,) * arr.ndim)
    in_specs = [tile(1), tile(2), tile(3), halo(1), halo(2), halo(3),
                pl.BlockSpec((tm, N_SMALL), lambda i: (i, 0)),
                pl.BlockSpec((8, N_SMALL), lambda i: (jnp.maximum(i * (tm // 8) - 1, 0), 0)),
                vec, vec, vec, pl.BlockSpec((1, N_SMALL), lambda i: (0, 0)),
                vec, vec, vec, vec,
                full(wts["w2p"]), full(wts["a2p"]), full(wts["g2p"])]
    args = [p, p, p, p, p, p, psm, psm, wts["mu_r"], wts["mu_k"], wts["mu_v"], wts["mu_s"],
            wts["w0"], wts["a0"], wts["k_k"], wts["k_a"], wts["w2p"], wts["a2p"], wts["g2p"]]
    if has_vres:
        in_specs += [pl.BlockSpec((tm, MIX), lambda i: (i, 0)), vec, full(wts["v1p"]), full(wts["v2p"])]
        args += [v_first, wts["v0"], wts["v1p"], wts["v2p"]]
    out = pl.BlockSpec((tm, MIX), lambda i: (i, 0))
    return pl.pallas_call(
        functools.partial(_rwkv_prep_kernel, seq=seq, tm=tm, has_vres=has_vres),
        grid=(m // tm,),
        in_specs=in_specs,
        out_specs=[out] * 7,
        out_shape=[jax.ShapeDtypeStruct((m, MIX), dt) for dt in (BF16,) * 5 + (F32, BF16)],
        compiler_params=_params("parallel"),
        name="rwkv_prep",
    )(*args)


def _rwkv_core_kernel(r_ref, k_ref, v_ref, kk_ref, a_ref, gc_ref, g_ref, rk_ref, lnw_ref, lnb_ref,
                      o_ref, h_ref):
    @pl.when(pl.program_id(1) == 0)
    def _():
        h_ref[...] = jnp.zeros_like(h_ref)

    nb, rb, _ = r_ref.shape
    nch = rb // CHUNK
    units = [(b, c) for c in range(nch) for b in range(nb)]
    m = _pair_masks()
    head_ones = m["bd64"].astype(BF16)
    stack = lambda x: _stack(x, m)
    cat0 = lambda *xs: jnp.concatenate(xs, axis=0)
    cat1 = lambda *xs: jnp.concatenate(xs, axis=1)

    def load(ref, u):
        b, c = u
        return ref[b, c * CHUNK:(c + 1) * CHUNK, :].astype(F32)

    first_row = _iota2((CHUNK, LANES), 0) == 0
    q_t, y0, phi, psi = {}, {}, {}, {}

    a_hat, r_hat, lhs, rhs, bbar_t, kbar_t, dlast, vs = [], [], [], [], [], [], [], []
    for u in units:
        gc = load(gc_ref, u)
        g_prev = jnp.where(first_row, 0.0, pltpu.roll(gc, 1, 0))
        kc, vc, kkc = load(k_ref, u), load(v_ref, u), load(kk_ref, u)
        bc = kkc * load(a_ref, u)
        glast = gc[CHUNK - 1:CHUNK]
        e_neg = jnp.exp(-gc)
        e_rest = jnp.exp(glast - gc)
        a_hat.append(-kkc * jnp.exp(g_prev))
        r_hat.append(load(r_ref, u) * jnp.exp(gc))
        lhs.append(cat0(a_hat[-1], r_hat[-1]))
        rhs.append(cat0(stack(bc * e_neg), stack(kc * e_neg)))
        bbar_t.append((bc * e_rest).T)
        kbar_t.append((kc * e_rest).T)
        dlast.append(jnp.exp(glast))
        vs.append(vc)
    big = [_dot_nt(x, y) for x, y in zip(lhs, rhs)]
    a_ab = [jnp.where(m["strict"], x[:CHUNK, :LANES], 0.0) for x in big]
    a_kr = [cat0(jnp.where(m["strict"], x[:CHUNK, LANES:], 0.0),
                 jnp.where(m["incl"], x[CHUNK:, LANES:], 0.0)) for x in big]
    a_rb = [jnp.where(m["incl"], x[CHUNK:, :LANES], 0.0) for x in big]
    x1 = [_dot(a, stack(v)) for a, v in zip(a_kr, vs)]
    t_inv = _tri_inv(a_ab, m)
    x2 = [_dot(t, cat1(stack(a), stack(x[:CHUNK]))) for t, a, x in zip(t_inv, a_hat, x1)]
    x3 = [_dot(a, cat1(stack(x[:, :LANES]), stack(x[:, LANES:]))) for a, x in zip(a_rb, x2)]
    for i, u in enumerate(units):
        w_t, u_t = x2[i][:, :LANES], x2[i][:, LANES:]
        phi[u] = m["eye128"] * dlast[i] + jnp.where(m["bd64"], _dot(bbar_t[i], w_t), 0.0)
        psi[u] = jnp.where(m["bd64"], _dot(cat1(bbar_t[i], kbar_t[i]), cat0(u_t, vs[i])), 0.0)
    for i, u in enumerate(units):
        q_t[u] = r_hat[i] + x3[i][:, :LANES]
        y0[u] = x3[i][:, LANES:] + x1[i][CHUNK:]

    ys = {}
    hs = [h_ref[b] for b in range(nb)]
    for u in units:
        b = u[0]
        qh = _dot(cat0(q_t[u], phi[u]), hs[b])
        ys[u] = y0[u] + qh[:CHUNK]
        hs[b] = qh[CHUNK:] + psi[u]
    for b in range(nb):
        h_ref[b] = hs[b]

    y = cat0(*[ys[(b, c)] for b in range(nb) for c in range(nch)])
    flat = lambda ref: ref[...].astype(F32).reshape(nb * rb, LANES)
    rr, kk_, vv = flat(r_ref), flat(k_ref), flat(v_ref)
    sums = _dot(cat0(y, rr * kk_ * rk_ref[...]), head_ones)
    yc = y - sums[:nb * rb] * (1.0 / RWKV_HEAD)
    var = _dot(yc * yc, head_ones) * (1.0 / RWKV_HEAD)
    yn = yc * lax.rsqrt(var + RWKV_LN_EPS) * lnw_ref[...] + lnb_ref[...]
    bonus = sums[nb * rb:] * vv
    o_ref[...] = ((yn + bonus) * flat(g_ref)).astype(BF16).reshape(nb, rb, LANES)


def _rwkv_core(prep, wts, batch, seq):
    rb = RB_RWKV
    tile = pl.BlockSpec((batch, rb, LANES), lambda p, j: (0, j, p))
    vec = pl.BlockSpec((1, LANES), lambda p, j: (0, p))
    out = pl.pallas_call(
        _rwkv_core_kernel,
        grid=(MIX // LANES, seq // rb),
        in_specs=[tile] * 7 + [vec] * 3,
        out_specs=tile,
        out_shape=jax.ShapeDtypeStruct((batch, seq, MIX), BF16),
        scratch_shapes=[pltpu.VMEM((batch, LANES, LANES), F32)],
        compiler_params=_params("parallel", "arbitrary"),
        name="rwkv_core",
    )(*[a.reshape(batch, seq, MIX) for a in prep], wts["r_k"], wts["ln_w"], wts["ln_b"])
    return out.reshape(batch * seq, MIX)


def _gdn_prep_kernel(pq, pk, pv, hq, hk, hv, ps, cwq, cwk, cwv, alog, dtb,
                     q_o, k_o, v_o, gb_o, *, seq, tm):
    first = (pl.program_id(0) * tm) % seq == 0
    ones = jnp.ones((LANES, LANES), BF16)

    def conv_silu(p_ref, h_ref, w_ref):
        halo = jnp.where(first, 0.0, h_ref[...].astype(F32))
        w = w_ref[...]
        y = p_ref[...].astype(F32) * w[3:4]
        for s, xs in zip((1, 2, 3), _shift_rows_bf16(p_ref[...], halo, (1, 2, 3))):
            y = y + xs * w[3 - s:4 - s]
        return _silu(y)

    def l2norm(y, scale):
        blocks = []
        for i in range(GDN_HEADS):
            blk = y[:, i * LANES:(i + 1) * LANES]
            ss = _dot(blk * blk, ones)
            blocks.append(blk * (lax.rsqrt(ss + GDN_EPS) * scale))
        return jnp.concatenate(blocks, axis=1)

    q_o[...] = l2norm(conv_silu(pq, hq, cwq), GDN_HEAD ** -0.5).astype(q_o.dtype)
    k_o[...] = l2norm(conv_silu(pk, hk, cwk), 1.0).astype(k_o.dtype)
    v_o[...] = conv_silu(pv, hv, cwv).astype(v_o.dtype)

    raw = ps[:, N_SMALL - LANES:]
    gval = -jnp.exp(alog[...]) * _softplus(raw + dtb[...])
    gcum = _dot_sel_l(_chunk_cumsum_matrix(tm), gval, 3)
    lane = _iota2(raw.shape, 1)
    is_g = (lane >= GDN_G_LANE) & (lane < GDN_G_LANE + GDN_HEADS)
    is_b = (lane >= GDN_B_LANE) & (lane < GDN_B_LANE + GDN_HEADS)
    gb_o[...] = jnp.where(is_g, gcum, jnp.where(is_b, _sigmoid(raw), 0.0))


def _gdn_prep(p, psm, wts, seq):
    m = p.shape[0]
    tm = TM_PREP
    tile = lambda cb: pl.BlockSpec((tm, MIX), lambda i, cb=cb: (i, cb))
    halo = lambda cb: pl.BlockSpec((16, MIX), lambda i, cb=cb: (jnp.maximum(i * (tm // 16) - 1, 0), cb))
    cw = pl.BlockSpec((4, MIX), lambda i: (0, 0))
    lv = pl.BlockSpec((1, LANES), lambda i: (0, 0))
    out = pl.BlockSpec((tm, MIX), lambda i: (i, 0))
    return pl.pallas_call(
        functools.partial(_gdn_prep_kernel, seq=seq, tm=tm),
        grid=(m // tm,),
        in_specs=[tile(4), tile(5), tile(6), halo(4), halo(5), halo(6),
                  pl.BlockSpec((tm, N_SMALL), lambda i: (i, 0)), cw, cw, cw, lv, lv],
        out_specs=[out, out, out, pl.BlockSpec((tm, LANES), lambda i: (i, 0))],
        out_shape=[jax.ShapeDtypeStruct((m, MIX), BF16)] * 3 + [jax.ShapeDtypeStruct((m, LANES), F32)],
        compiler_params=_params("parallel"),
        name="gdn_prep",
    )(p, p, p, p, p, p, psm, wts["cw_q"], wts["cw_k"], wts["cw_v"], wts["alog"], wts["dtb"])


def _gdn_core_kernel(q_ref, k_ref, v_ref, gb_ref, z_ref, nw_ref, o_ref, s_ref):
    @pl.when(pl.program_id(1) == 0)
    def _():
        s_ref[...] = jnp.zeros_like(s_ref)

    nb, rb, width = q_ref.shape
    nh = width // LANES
    units = [(b, hh) for hh in range(nh) for b in range(nb)]
    assert rb == 2 * CHUNK
    m = _pair_masks()
    ones = jnp.ones((LANES, LANES), BF16)
    cat0 = lambda *xs: jnp.concatenate(xs, axis=0)
    cat1 = lambda *xs: jnp.concatenate(xs, axis=1)

    def lane_bcast(x, lane):
        return jnp.broadcast_to(pltpu.roll(x, LANES - lane, 1)[:, 0:1], x.shape)

    nch = rb // CHUNK
    q_t, y0, phi, psi = {}, {}, {}, {}

    gf, ks, lhs, kbd, rhs, e_g, qs, gamma = [], [], [], [], [], [], [], []
    zero = jnp.zeros((CHUNK, LANES), F32)
    for b, hh in units:
        hl = slice(hh * LANES, (hh + 1) * LANES)
        head = nh * pl.program_id(0) + hh
        gbv = gb_ref[b]
        g = lane_bcast(gbv, GDN_G_LANE + head)
        beta = lane_bcast(gbv, GDN_B_LANE + head)
        q, k, v = (r[b, :, hl].astype(F32) for r in (q_ref, k_ref, v_ref))
        g_col = g[:CHUNK] * m["half0"] + g[CHUNK:] * m["half1"]
        diff = g_col - g.T[:CHUNK]
        gamma.append(jnp.where(m["incl"], jnp.exp(jnp.where(m["incl"], diff, 0.0)), 0.0))
        eg = jnp.exp(g)
        kb = k * beta
        gf.append(g)
        ks.append(k)
        qs.append(q)
        e_g.append(eg)
        lhs.append(cat0(cat1(kb[:CHUNK], kb[CHUNK:]), cat1(q[:CHUNK], q[CHUNK:])))
        kbd.append(cat0(cat1(k[:CHUNK], zero), cat1(zero, k[CHUNK:])))
        rhs.append(cat1(v * beta, kb * eg))
    kq = [_dot_nt(x, y) for x, y in zip(lhs, kbd)]
    a_neg = [-jnp.where(m["strict"], x[:CHUNK] * gm, 0.0) for x, gm in zip(kq, gamma)]
    a_qk = [x[CHUNK:] * gm for x, gm in zip(kq, gamma)]
    k_bar_t = [[(k[c * CHUNK:(c + 1) * CHUNK]
                 * jnp.exp(g[(c + 1) * CHUNK - 1:(c + 1) * CHUNK] - g[c * CHUNK:(c + 1) * CHUNK])).T
                for c in range(nch)] for k, g in zip(ks, gf)]
    t_inv = _tri_inv(a_neg, m)
    uw = [_dot(_stack(t, m), x) for t, x in zip(t_inv, rhs)]
    x3 = [_dot(_stack(a, m), x) for a, x in zip(a_qk, uw)]
    for i, u in enumerate(units):
        for c in range(nch):
            glast = gf[i][(c + 1) * CHUNK - 1:(c + 1) * CHUNK]
            pp = _dot(k_bar_t[i][c], uw[i][c * CHUNK:(c + 1) * CHUNK])
            phi[(u, c)] = m["eye128"] * jnp.exp(glast) - pp[:, LANES:]
            psi[(u, c)] = pp[:, :LANES]
    for i, u in enumerate(units):
        y0[u] = x3[i][:, :LANES]
        q_t[u] = qs[i] * e_g[i] - x3[i][:, LANES:]

    ys = {u: [] for u in units}
    st = {(b, hh): s_ref[hh, b] for b, hh in units}
    for c in range(nch):
        cs = slice(c * CHUNK, (c + 1) * CHUNK)
        for u in units:
            qs_ = _dot(cat0(q_t[u][cs], phi[(u, c)]), st[u])
            ys[u].append(y0[u][cs] + qs_[:CHUNK])
            st[u] = qs_[CHUNK:] + psi[(u, c)]
    for b, hh in units:
        s_ref[hh, b] = st[(b, hh)]

    for hh in range(nh):
        hl = slice(hh * LANES, (hh + 1) * LANES)
        y = cat0(*[yc for b in range(nb) for yc in ys[(b, hh)]])
        ms = _dot(y * y, ones) * (1.0 / GDN_HEAD)
        out = y * lax.rsqrt(ms + GDN_EPS) * nw_ref[...]
        z = z_ref[:, :, hl].astype(F32).reshape(nb * rb, LANES)
        o_ref[:, :, hl] = (out * _silu(z)).astype(BF16).reshape(nb, rb, LANES)


def _gdn_core(q, k, v, gb, p, norm_w, batch, seq):
    rb, nh = RB_GDN, GDN_HEADS_PER_STEP
    width = nh * LANES
    tile = pl.BlockSpec((batch, rb, width), lambda h, j: (0, j, h))
    z_col0 = 7 * (MIX // width)
    r3 = lambda a: a.reshape(batch, seq, a.shape[-1])
    out = pl.pallas_call(
        _gdn_core_kernel,
        grid=(GDN_HEADS // nh, seq // rb),
        in_specs=[tile, tile, tile,
                  pl.BlockSpec((batch, rb, LANES), lambda h, j: (0, j, 0)),
                  pl.BlockSpec((batch, rb, width), lambda h, j: (0, j, z_col0 + h)),
                  pl.BlockSpec((1, LANES), lambda h, j: (0, 0))],
        out_specs=tile,
        out_shape=jax.ShapeDtypeStruct((batch, seq, MIX), BF16),
        scratch_shapes=[pltpu.VMEM((nh, batch, LANES, LANES), F32)],
        compiler_params=_params("parallel", "arbitrary"),
        name="gdn_core",
    )(r3(q), r3(k), r3(v), r3(gb), r3(p), norm_w.reshape(1, LANES))
    return out.reshape(batch * seq, MIX)


def _merge_kernel(ya, yb, yc, wa, wb, wc, ga, gb, gc, o_ref):
    dot = lambda y, w: jnp.dot(y[...], w[0], preferred_element_type=F32)
    acc = ga[...].astype(F32) * dot(ya, wa)
    acc = acc + gb[...].astype(F32) * dot(yb, wb)
    acc = acc + gc[...].astype(F32) * dot(yc, wc)
    o_ref[...] = acc.astype(BF16)


def _merge(ya, yb, yc, wa, wb, wc, l, gates):
    m = ya.shape[0]
    d = wa.shape[2]
    tm = TM_LN
    y = pl.BlockSpec((tm, MIX), lambda i: (i, 0))
    w = pl.BlockSpec((1, MIX, d), lambda i: (l, 0, 0), pipeline_mode=pl.Buffered(1))
    g = lambda br: pl.BlockSpec((tm, d), lambda i, br=br: (i, br))
    return pl.pallas_call(
        _merge_kernel,
        grid=(m // tm,),
        in_specs=[y, y, y, w, w, w, g(0), g(1), g(2)],
        out_specs=pl.BlockSpec((tm, d), lambda i: (i, 0)),
        out_shape=jax.ShapeDtypeStruct((m, d), BF16),
        compiler_params=_params("parallel"),
        name="merge",
    )(ya, yb, yc, wa, wb, wc, gates, gates, gates)


def _proj_ln_kernel(*refs, emit_h):
    a_ref, w_ref, x_ref, gt_ref, lnw_ref, lnb_ref = refs[:6]
    y = jnp.dot(a_ref[...], w_ref[0], preferred_element_type=F32)
    xn = _resid_ln(x_ref[...], y, gt_ref[0], lnw_ref[...], lnb_ref[...])
    if emit_h:
        sc_ref, sh_ref, xo_ref, ho_ref = refs[6:]
        ho_ref[...] = (xn * (1.0 + sc_ref[0]) + sh_ref[0]).astype(BF16)
    else:
        (xo_ref,) = refs[6:]
    xo_ref[...] = xn


def _proj_ln(a, w, l, x, gt, lnw, lnb, nxt, seq, *, tm, name):
    m, k = a.shape
    d = w.shape[2]
    per = seq // tm
    emit_h = nxt is not None
    row = pl.BlockSpec((tm, d), lambda i: (i, 0))
    mod = pl.BlockSpec((1, 1, d), lambda i: (i // per, 0, 0))
    vec = pl.BlockSpec((1, d), lambda i: (0, 0))
    in_specs = [pl.BlockSpec((tm, k), lambda i: (i, 0)),
                pl.BlockSpec((1, k, d), lambda i: (l, 0, 0), pipeline_mode=pl.Buffered(1)),
                row, mod, vec, vec]
    args = [a, w, x, gt, lnw.reshape(1, d), lnb.reshape(1, d)]
    out_specs, out_shape = [row], [jax.ShapeDtypeStruct((m, d), F32)]
    if emit_h:
        in_specs += [mod, mod]
        args += list(nxt)
        out_specs.append(row)
        out_shape.append(jax.ShapeDtypeStruct((m, d), BF16))
    res = pl.pallas_call(
        functools.partial(_proj_ln_kernel, emit_h=emit_h),
        grid=(m // tm,),
        in_specs=in_specs,
        out_specs=out_specs,
        out_shape=out_shape,
        compiler_params=_params("parallel"),
        name=name,
    )(*args)
    return (res[0], res[1]) if emit_h else (res[0], None)


def _swiglu_kernel(a_ref, wg_ref, wu_ref, o_ref):
    a = a_ref[...]
    g = jnp.dot(a, wg_ref[0], preferred_element_type=F32)
    u = jnp.dot(a, wu_ref[0], preferred_element_type=F32)
    o_ref[...] = (_silu(g) * u).astype(BF16)


def _swiglu(h, w_up, l):
    m, k = h.shape
    dff = w_up.shape[2] // 2
    tm, tn = TM_MM, 512
    nj = dff // tn
    return pl.pallas_call(
        _swiglu_kernel,
        grid=(m // tm, nj),
        in_specs=[pl.BlockSpec((tm, k), lambda i, j: (i, 0)),
                  pl.BlockSpec((1, k, tn), lambda i, j: (l, 0, j)),
                  pl.BlockSpec((1, k, tn), lambda i, j: (l, 0, nj + j))],
        out_specs=pl.BlockSpec((tm, tn), lambda i, j: (i, j)),
        out_shape=jax.ShapeDtypeStruct((m, dff), BF16),
        compiler_params=_params("parallel", "parallel"),
        name="ffn_up_swiglu",
    )(h, w_up, w_up)


def _layer_weights(l, w_in_t, rwkv_mu, rwkv_w0, rwkv_w2, rwkv_a0, rwkv_a2, rwkv_g2, rwkv_k_k, rwkv_k_a,
                   rwkv_r_k, rwkv_ln_w, rwkv_ln_b, rwkv_v0, rwkv_v1, rwkv_v2, gdn_conv_w, gdn_a_log,
                   gdn_dt_bias):
    c = MIX
    rows = lambda lo, hi: w_in_t[l, lo:hi].astype(BF16)
    o_rwkv = c
    o_lora = o_rwkv + 3 * c
    n_lora = rwkv_w2.shape[1] + rwkv_a2.shape[1] + rwkv_g2.shape[1]
    o_gdn = o_lora + n_lora
    o_ab = o_gdn + 4 * c
    o_gate = o_ab + 2 * GDN_HEADS
    pad_rows = N_SMALL - n_lora - 2 * GDN_HEADS
    w_main = jnp.concatenate([rows(0, o_lora), rows(o_gdn, o_ab)], axis=0)
    assert n_lora - (N_SMALL - LANES) == GDN_G_LANE and GDN_B_LANE == GDN_G_LANE + GDN_HEADS
    w_small = jnp.concatenate([rows(o_lora, o_gdn), rows(o_ab, o_gate),
                               jnp.zeros((pad_rows, w_in_t.shape[2]), BF16)], axis=0)
    w_gate = rows(o_gate, w_in_t.shape[1])

    mu = rwkv_mu[l]
    row = lambda v: v.reshape(1, -1)
    n_w, n_a, n_g = rwkv_w2.shape[1], rwkv_a2.shape[1], rwkv_g2.shape[1]
    padrows = lambda w, lo: jnp.zeros((N_SMALL, c), F32).at[lo:lo + w.shape[0]].set(w).astype(BF16)
    lane_vec = lambda v, lo: jnp.zeros((1, LANES), F32).at[0, lo:lo + v.shape[0]].set(v)
    rw = {
        "mu_r": row(mu[:c]), "mu_k": row(mu[c:2 * c]), "mu_v": row(mu[2 * c:3 * c]),
        "mu_s": jnp.zeros((1, N_SMALL), F32).at[0, :n_lora].set(mu[3 * c:]),
        "w0": row(rwkv_w0[l]), "a0": row(rwkv_a0[l]), "k_k": row(rwkv_k_k[l]), "k_a": row(rwkv_k_a[l]),
        "w2p": padrows(rwkv_w2[l], 0), "a2p": padrows(rwkv_a2[l], n_w), "g2p": padrows(rwkv_g2[l], n_w + n_a),
        "r_k": row(rwkv_r_k[l]), "ln_w": row(rwkv_ln_w[l]), "ln_b": row(rwkv_ln_b[l]),
    }
    if l > 0:
        nv = rwkv_v1.shape[2]
        rw["v0"] = row(rwkv_v0[l - 1])
        rw["v1p"] = jnp.zeros((c, LANES), F32).at[:, :nv].set(rwkv_v1[l - 1]).astype(BF16)
        rw["v2p"] = jnp.zeros((LANES, c), F32).at[:nv].set(rwkv_v2[l - 1]).astype(BF16)
    cw = gdn_conv_w[l]
    gd = {"cw_q": cw[:, :c], "cw_k": cw[:, c:2 * c], "cw_v": cw[:, 2 * c:],
          "alog": lane_vec(gdn_a_log[l], GDN_G_LANE), "dtb": lane_vec(gdn_dt_bias[l], GDN_G_LANE)}
    return w_main, w_small, w_gate, rw, gd


def kernel(x, c, ada_w, ada_b, w_in, pool_w, pool_scale, rwkv_mu, rwkv_w0, rwkv_w2, rwkv_a0, rwkv_a2, rwkv_g2, rwkv_k_k, rwkv_k_a, rwkv_r_k, rwkv_ln_w, rwkv_ln_b, rwkv_v0, rwkv_v1, rwkv_v2, gdn_conv_w, gdn_a_log, gdn_dt_bias, gdn_norm_w, w_branch_a, w_branch_b, w_branch_c, w_out, ln1_w, ln1_b, ffn_w_up, ffn_w_down, ln2_w, ln2_b):
    batch, seq, d = x.shape
    assert seq % TM_MM == 0 and d == 2 * MIX
    m = batch * seq
    xf = x.reshape(m, d)

    mod = _ada(c, ada_w, ada_b)
    mods = [[mod[l, :, i * d:(i + 1) * d].reshape(batch, 1, d) for i in range(6)]
            for l in range(DEPTH)]

    h = _modulate(xf, mods[0][1], mods[0][0], seq)
    w_in_t = jnp.swapaxes(w_in, 1, 2)
    wa, wb, wc = (w.astype(BF16) for w in (w_branch_a, w_branch_b, w_branch_c))
    w_out_b, w_up_b, w_down_b = (w.astype(BF16) for w in (w_out, ffn_w_up, ffn_w_down))
    v_first = None
    for l in range(DEPTH):
        sh_m, sc_m, gt_m, sh_f, sc_f, gt_f = mods[l]
        w_main, w_small, w_gate, rw, gd = _layer_weights(
            l, w_in_t, rwkv_mu, rwkv_w0, rwkv_w2, rwkv_a0, rwkv_a2, rwkv_g2, rwkv_k_k, rwkv_k_a,
            rwkv_r_k, rwkv_ln_w, rwkv_ln_b, rwkv_v0, rwkv_v1, rwkv_v2, gdn_conv_w, gdn_a_log,
            gdn_dt_bias)

        p = _matmul_nt(h, w_main, tn=2048, out_dtype=BF16, name="in_proj_main")
        psm = _matmul_nt(h, w_small, tn=N_SMALL, out_dtype=F32, name="in_proj_small")
        gates = _matmul_nt(h, w_gate, tn=2048, out_dtype=BF16, act="sigmoid", name="in_proj_gates")

        y_a = _pool(p, pool_w[l], pool_scale[l], seq)
        prep = _rwkv_prep(p, psm, rw, v_first, seq)
        if l == 0:
            v_first = prep[2]
        y_b = _rwkv_core(prep, rw, batch, seq)
        gq, gk, gv, ggb = _gdn_prep(p, psm, gd, seq)
        y_c = _gdn_core(gq, gk, gv, ggb, p, gdn_norm_w[l], batch, seq)

        merged = _merge(y_a, y_b, y_c, wa, wb, wc, l, gates)
        xf, h = _proj_ln(merged, w_out_b, l, xf, gt_m, ln1_w[l], ln1_b[l],
                         (sc_f, sh_f), seq, tm=512, name="out_proj_ln")

        act = _swiglu(h, w_up_b, l)
        nxt = (mods[l + 1][1], mods[l + 1][0]) if l + 1 < DEPTH else None
        xf, h = _proj_ln(act, w_down_b, l, xf, gt_f, ln2_w[l], ln2_b[l],
                         nxt, seq, tm=256, name="ffn_down_ln")
    return xf.reshape(batch, seq, d)
```

```python
import functools
import math

import jax
import jax.numpy as jnp
from jax import lax
from jax.experimental import pallas as pl
from jax.experimental.pallas import tpu as pltpu

F32 = jnp.float32
BF16 = jnp.bfloat16

DEPTH = 2
DN_ALPHA = (2 * DEPTH) ** 0.25
LN_EPS = 1e-5
MIX = 1024
POOL_WINDOWS = (2, 4, 8, 16)
POOL_GROUP = 256
RWKV_HEAD = 64
RWKV_LN_EPS = RWKV_HEAD * 1e-5
GDN_HEAD = 128
GDN_HEADS = 8
GDN_EPS = 1e-6
CHUNK = 64
LANES = 128
N_SMALL = 384
GDN_G_LANE = 32
GDN_B_LANE = 40
VMEM_LIMIT = 56 * 1024 * 1024

TM_MM = 1024
TM_LN = 512
TM_PREP = 256
RB_RWKV = 512
RB_GDN = 128
GDN_HEADS_PER_STEP = 8


def _dot(a, b):
    return jnp.dot(a.astype(BF16), b.astype(BF16), preferred_element_type=F32)


def _dot_nt(a, b):
    return lax.dot_general(a.astype(BF16), b.astype(BF16), (((1,), (1,)), ((), ())),
                           preferred_element_type=F32)


def _split(a, n):
    terms, rest = [], a
    for _ in range(n):
        t = rest.astype(BF16)
        terms.append(t)
        rest = rest - t.astype(F32)
    return terms


def _dot3(a, b, nt=False):
    d = _dot_nt if nt else _dot
    a1, a2 = _split(a, 2)
    b1, b2 = _split(b, 2)
    return d(a1, b1) + (d(a1, b2) + d(a2, b1))


def _dot_sel_l(sel, b, n):
    out = None
    for t in _split(b, n):
        p = jnp.dot(sel, t, preferred_element_type=F32)
        out = p if out is None else out + p
    return out


def _sigmoid(x):
    return 0.5 * jnp.tanh(0.5 * x) + 0.5


def _silu(x):
    return x * _sigmoid(x)


def _softplus(x):
    return jnp.maximum(x, 0.0) + jnp.log(1.0 + jnp.exp(-jnp.abs(x)))


def _iota2(shape, dim):
    return lax.broadcasted_iota(jnp.int32, shape, dim)


def _pair_masks():
    r = _iota2((CHUNK, LANES), 0)
    c = _iota2((CHUNK, LANES), 1) & (CHUNK - 1)
    same = lambda s: (r >> s) == (c >> s)
    r2 = _iota2((LANES, LANES), 0)
    c2 = _iota2((LANES, LANES), 1)
    lane = _iota2((1, LANES), 1)
    return {
        "eye": (r == c).astype(F32),
        "bd8": same(3),
        "off16": same(4) & jnp.logical_not(same(3)),
        "off32": same(5) & jnp.logical_not(same(4)),
        "off64": jnp.logical_not(same(5)),
        "strict": c < r,
        "incl": c <= r,
        "eye128": (r2 == c2).astype(F32),
        "bd64": (r2 >> 6) == (c2 >> 6),
        "low": _iota2((CHUNK, LANES), 1) < CHUNK,
        "half0": (lane < CHUNK).astype(F32),
        "half1": (lane >= CHUNK).astype(F32),
    }


def _stack(x, m):
    xb = x.astype(BF16)
    zero = jnp.zeros_like(xb)
    return jnp.concatenate([jnp.where(m["low"], xb, zero), jnp.where(m["low"], zero, xb)], axis=0)


def _chunk_cumsum_matrix(n):
    r = _iota2((n, n), 0)
    c = _iota2((n, n), 1)
    return (((r >> 6) == (c >> 6)) & (c <= r)).astype(BF16)


def _tri_inv(mats, m):
    bd = lambda x: _stack(x, m)
    a8 = [jnp.where(m["bd8"], a, 0.0) for a in mats]
    s = [m["eye"] + x for x in a8]
    p = [_dot(x, bd(x)) for x in a8]
    q = [_dot(pi, jnp.concatenate([bd(si), bd(pi)], axis=1)) for pi, si in zip(p, s)]
    s = [si + qi[:, :LANES] for si, qi in zip(s, q)]
    s = [si + _dot(qi[:, LANES:], bd(si)) for si, qi in zip(s, q)]
    for key in ("off16", "off32", "off64"):
        t = [_dot(jnp.where(m[key], a, 0.0), bd(si)) for a, si in zip(mats, s)]
        s = [si + _dot(si, bd(ti)) for si, ti in zip(s, t)]
    return s


def _shift_rows(x, halo, s):
    halo = halo[halo.shape[0] - 8:]
    xr = pltpu.roll(x, s, 0)
    hr = pltpu.roll(halo, s, 0)
    rows = _iota2(halo.shape, 0)
    top = jnp.where(rows < s, hr, xr[:8])
    return top if x.shape[0] == 8 else jnp.concatenate([top, xr[8:]], axis=0)


def _shift_rows_bf16(x, halo, shifts):
    tm = x.shape[0]
    r = _iota2((tm, tm), 0)
    c = _iota2((tm, tm), 1)
    sel = jnp.concatenate([(r - c == s).astype(BF16) for s in shifts], axis=0)
    moved = jnp.dot(sel, x, preferred_element_type=F32)
    x_top = x[:16].astype(F32)[:8]
    outs = []
    for j, s in enumerate(shifts):
        top = _shift_rows(x_top, halo, s)
        outs.append(jnp.concatenate([top, moved[j * tm + 8:(j + 1) * tm]], axis=0))
    return outs


def _resid_ln(x, y, gt, lnw, lnb):
    z = DN_ALPHA * x + (1.0 + gt) * y
    mu = jnp.mean(z, axis=-1, keepdims=True)
    zc = z - mu
    var = jnp.mean(zc * zc, axis=-1, keepdims=True)
    return zc * lax.rsqrt(var + LN_EPS) * lnw + lnb


def _params(*sem):
    return pltpu.CompilerParams(dimension_semantics=sem, vmem_limit_bytes=VMEM_LIMIT)


def _ada_kernel(c_ref, w_ref, b_ref, o_ref):
    c = c_ref[...]
    o_ref[0] = _dot3(_silu(c), w_ref[0]) + b_ref[0]


def _ada(c, ada_w, ada_b):
    nl, d, n = ada_w.shape
    b = c.shape[0]
    tn = 512
    cp = jnp.zeros((8, d), F32).at[:b].set(c)
    out = pl.pallas_call(
        _ada_kernel,
        grid=(nl, n // tn),
        in_specs=[pl.BlockSpec((8, d), lambda l, j: (0, 0)),
                  pl.BlockSpec((1, d, tn), lambda l, j: (l, 0, j)),
                  pl.BlockSpec((1, 1, tn), lambda l, j: (l, 0, j))],
        out_specs=pl.BlockSpec((1, 8, tn), lambda l, j: (l, 0, j)),
        out_shape=jax.ShapeDtypeStruct((nl, 8, n), F32),
        compiler_params=_params("parallel", "parallel"),
        name="ada_mod",
    )(cp, ada_w, ada_b.reshape(nl, 1, n))
    return out[:, :b]


def _modulate_small_kernel(x_ref, sc_ref, sh_ref, wt_ref, h_ref, ps_ref):
    h = (x_ref[...] * (1.0 + sc_ref[0]) + sh_ref[0]).astype(BF16)
    h_ref[...] = h
    ps_ref[...] = lax.dot_general(h, wt_ref[...], (((1,), (1,)), ((), ())), preferred_element_type=F32)


def _modulate_small(x, sc, sh, w_small_t, seq):
    m, d = x.shape
    n = w_small_t.shape[0]
    tm = TM_LN
    per = seq // tm
    mod = pl.BlockSpec((1, 1, d), lambda i: (i // per, 0, 0))
    return pl.pallas_call(
        _modulate_small_kernel,
        grid=(m // tm,),
        in_specs=[pl.BlockSpec((tm, d), lambda i: (i, 0)), mod, mod,
                  pl.BlockSpec((n, d), lambda i: (0, 0))],
        out_specs=[pl.BlockSpec((tm, d), lambda i: (i, 0)), pl.BlockSpec((tm, n), lambda i: (i, 0))],
        out_shape=[jax.ShapeDtypeStruct((m, d), BF16), jax.ShapeDtypeStruct((m, n), F32)],
        compiler_params=_params("parallel"),
        name="modulate_small",
    )(x, sc, sh, w_small_t)


def _mm_kernel(a_ref, bt_ref, o_ref, *, act):
    acc = lax.dot_general(a_ref[...], bt_ref[...], (((1,), (1,)), ((), ())),
                          preferred_element_type=F32)
    if act == "sigmoid":
        acc = _sigmoid(acc)
    o_ref[...] = acc.astype(o_ref.dtype)


def _matmul_nt(a, bt, *, tn, out_dtype, act=None, name):
    m, k = a.shape
    n = bt.shape[0]
    tm = TM_MM
    return pl.pallas_call(
        functools.partial(_mm_kernel, act=act),
        grid=(m // tm, n // tn),
        in_specs=[pl.BlockSpec((tm, k), lambda i, j: (i, 0)),
                  pl.BlockSpec((tn, k), lambda i, j: (j, 0))],
        out_specs=pl.BlockSpec((tm, tn), lambda i, j: (i, j)),
        out_shape=jax.ShapeDtypeStruct((m, n), out_dtype),
        compiler_params=_params("parallel", "parallel"),
        name=name,
    )(a, bt)


def _mm2_kernel(a_ref, b1_ref, b2_ref, o_ref, *, nj1):
    dims = (((1,), (1,)), ((), ()))

    @pl.when(pl.program_id(1) < nj1)
    def _():
        o_ref[...] = lax.dot_general(a_ref[...], b1_ref[0], dims,
                                     preferred_element_type=F32).astype(o_ref.dtype)

    @pl.when(pl.program_id(1) >= nj1)
    def _():
        o_ref[...] = lax.dot_general(a_ref[...], b2_ref[...], dims,
                                     preferred_element_type=F32).astype(o_ref.dtype)


def _matmul_nt2(a, w_all, l, n1, b2, *, tn, out_dtype, name):
    m, k = a.shape
    n2 = b2.shape[0]
    tm = TM_MM
    nj1 = n1 // tn
    return pl.pallas_call(
        functools.partial(_mm2_kernel, nj1=nj1),
        grid=(m // tm, (n1 + n2) // tn),
        in_specs=[pl.BlockSpec((tm, k), lambda i, j: (i, 0)),
                  pl.BlockSpec((1, tn, k), lambda i, j: (l, jnp.minimum(j, nj1 - 1), 0)),
                  pl.BlockSpec((tn, k), lambda i, j: (jnp.maximum(j - nj1, 0), 0))],
        out_specs=pl.BlockSpec((tm, tn), lambda i, j: (i, j)),
        out_shape=jax.ShapeDtypeStruct((m, n1 + n2), out_dtype),
        compiler_params=_params("parallel", "parallel"),
        name=name,
    )(a, w_all, b2)


def _pool_kernel(p_ref, halo_ref, w_ref, scale_ref, o_ref, *, seq, tm):
    t0 = (pl.program_id(0) * tm) % seq
    x = p_ref[...].astype(F32)
    halo = jnp.where(t0 == 0, 0.0, halo_ref[...].astype(F32))
    ext = jnp.concatenate([halo, x], axis=0)
    sums = [ext]
    for sh in (1, 2, 4, 8):
        sums.append(sums[-1] + pltpu.roll(sums[-1], sh, 0))
    t = t0 + _iota2((tm, POOL_GROUP), 0)
    ys = []
    for g, win in enumerate(POOL_WINDOWS):
        cols = slice(g * POOL_GROUP, (g + 1) * POOL_GROUP)
        cnt = jnp.minimum(t + 1, win).astype(F32)
        pooled = sums[g + 1][16:, cols] / cnt - x[:, cols]
        ys.append(_dot(pooled, w_ref[g]))
    o_ref[...] = (jnp.concatenate(ys, axis=1) * scale_ref[...]).astype(BF16)


def _pool(p, pool_w, pool_scale, seq):
    m = p.shape[0]
    tm = TM_LN
    return pl.pallas_call(
        functools.partial(_pool_kernel, seq=seq, tm=tm),
        grid=(m // tm,),
        in_specs=[pl.BlockSpec((tm, MIX), lambda i: (i, 0)),
                  pl.BlockSpec((16, MIX), lambda i: (jnp.maximum(i * (tm // 16) - 1, 0), 0)),
                  pl.BlockSpec(pool_w.shape, lambda i: (0, 0, 0)),
                  pl.BlockSpec((1, MIX), lambda i: (0, 0))],
        out_specs=pl.BlockSpec((tm, MIX), lambda i: (i, 0)),
        out_shape=jax.ShapeDtypeStruct((m, MIX), BF16),
        compiler_params=_params("parallel"),
        name="pool_mixer",
    )(p, p, pool_w.astype(BF16), pool_scale.reshape(1, MIX))


def _rwkv_prep_kernel(*refs, seq, tm, has_vres):
    (pr, pk, pv, hr, hk, hv, ps, hs, mur, muk, muv, mus, w0, a0, kkp, kap,
     w2p, a2p, g2p) = refs[:19]
    if has_vres:
        vf, v0, v1p, v2p = refs[19:23]
        outs = refs[23:]
    else:
        outs = refs[19:]
    r_o, k_o, v_o, kk_o, a_o, ld_o, g_o = outs

    first = (pl.program_id(0) * tm) % seq == 0

    def lerp(p_ref, h_ref, mu_ref):
        halo = jnp.where(first, 0.0, h_ref[...].astype(F32))
        if p_ref.dtype == BF16:
            (prev,) = _shift_rows_bf16(p_ref[...], halo, (1,))
        else:
            prev = _shift_rows(p_ref[...], halo, 1)
        p = p_ref[...].astype(F32)
        return p + (prev - p) * mu_ref[...]

    xr = lerp(pr, hr, mur)
    xk = lerp(pk, hk, muk)
    xv = lerp(pv, hv, muv)
    xs = lerp(ps, hs, mus)

    z = w0[...] + _dot(jnp.tanh(xs), w2p[...])
    t = jnp.exp(-jnp.abs(z))
    log_decay = jnp.where(z < 0.0, t, 1.0) * (-math.exp(-0.5) / (1.0 + t))
    a = _sigmoid(a0[...] + _dot(xs, a2p[...]))
    g = _dot(_sigmoid(xs), g2p[...])
    if has_vres:
        gate = _sigmoid(v0[...] + _dot(_dot(xv, v1p[...]), v2p[...]))
        xv = xv + (vf[...].astype(F32) - xv) * gate

    kk = xk * kkp[...]
    r2 = _iota2((LANES, LANES), 0)
    c2 = _iota2((LANES, LANES), 1)
    head_ones = ((r2 >> 6) == (c2 >> 6)).astype(BF16)
    kk2 = kk * kk
    ss = jnp.concatenate(
        [_dot(kk2[:, i * LANES:(i + 1) * LANES], head_ones) for i in range(MIX // LANES)],
        axis=1)
    kk = kk * lax.rsqrt(jnp.maximum(ss, 1e-24))

    r_o[...] = xr.astype(r_o.dtype)
    k_o[...] = (xk * (1.0 + (a - 1.0) * kap[...])).astype(k_o.dtype)
    v_o[...] = xv.astype(v_o.dtype)
    kk_o[...] = kk.astype(kk_o.dtype)
    a_o[...] = a.astype(a_o.dtype)
    ld_o[...] = _dot_sel_l(_chunk_cumsum_matrix(tm), log_decay, 2)
    g_o[...] = g.astype(g_o.dtype)


def _rwkv_prep(p, psm, wts, v_first, seq):
    m = p.shape[0]
    tm = TM_PREP
    has_vres = v_first is not None
    tile = lambda cb: pl.BlockSpec((tm, MIX), lambda i, cb=cb: (i, cb))
    halo = lambda cb: pl.BlockSpec((16, MIX), lambda i, cb=cb: (jnp.maximum(i * (tm // 16) - 1, 0), cb))
    vec = pl.BlockSpec((1, MIX), lambda i: (0, 0))
    full = lambda arr: pl.BlockSpec(arr.shape, lambda i: (0,) * arr.ndim)
    in_specs = [tile(1), tile(2), tile(3), halo(1), halo(2), halo(3),
                pl.BlockSpec((tm, N_SMALL), lambda i: (i, 0)),
                pl.BlockSpec((8, N_SMALL), lambda i: (jnp.maximum(i * (tm // 8) - 1, 0), 0)),
                vec, vec, vec, pl.BlockSpec((1, N_SMALL), lambda i: (0, 0)),
                vec, vec, vec, vec,
                full(wts["w2p"]), full(wts["a2p"]), full(wts["g2p"])]
    args = [p, p, p, p, p, p, psm, psm, wts["mu_r"], wts["mu_k"], wts["mu_v"], wts["mu_s"],
            wts["w0"], wts["a0"], wts["k_k"], wts["k_a"], wts["w2p"], wts["a2p"], wts["g2p"]]
    if has_vres:
        in_specs += [pl.BlockSpec((tm, MIX), lambda i: (i, 0)), vec, full(wts["v1p"]), full(wts["v2p"])]
        args += [v_first, wts["v0"], wts["v1p"], wts["v2p"]]
    out = pl.BlockSpec((tm, MIX), lambda i: (i, 0))
    return pl.pallas_call(
        functools.partial(_rwkv_prep_kernel, seq=seq, tm=tm, has_vres=has_vres),
        grid=(m // tm,),
        in_specs=in_specs,
        out_specs=[out] * 7,
        out_shape=[jax.ShapeDtypeStruct((m, MIX), dt) for dt in (BF16,) * 5 + (F32, BF16)],
        compiler_params=_params("parallel"),
        name="rwkv_prep",
    )(*args)


def _rwkv_core_kernel(r_ref, k_ref, v_ref, kk_ref, a_ref, gc_ref, g_ref, rk_ref, lnw_ref, lnb_ref,
                      o_ref, h_ref):
    @pl.when(pl.program_id(1) == 0)
    def _():
        h_ref[...] = jnp.zeros_like(h_ref)

    nb, rb, _ = r_ref.shape
    nch = rb // CHUNK
    units = [(b, c) for c in range(nch) for b in range(nb)]
    m = _pair_masks()
    head_ones = m["bd64"].astype(BF16)
    stack = lambda x: _stack(x, m)
    cat0 = lambda *xs: jnp.concatenate(xs, axis=0)
    cat1 = lambda *xs: jnp.concatenate(xs, axis=1)

    def load(ref, u):
        b, c = u
        return ref[b, c * CHUNK:(c + 1) * CHUNK, :].astype(F32)

    first_row = _iota2((CHUNK, LANES), 0) == 0
    q_t, y0, phi, psi = {}, {}, {}, {}

    a_hat, r_hat, lhs, rhs, bbar_t, kbar_t, dlast, vs = [], [], [], [], [], [], [], []
    for u in units:
        gc = load(gc_ref, u)
        g_prev = jnp.where(first_row, 0.0, pltpu.roll(gc, 1, 0))
        kc, vc, kkc = load(k_ref, u), load(v_ref, u), load(kk_ref, u)
        bc = kkc * load(a_ref, u)
        glast = gc[CHUNK - 1:CHUNK]
        e_neg = jnp.exp(-gc)
        e_rest = jnp.exp(glast - gc)
        a_hat.append(-kkc * jnp.exp(g_prev))
        r_hat.append(load(r_ref, u) * jnp.exp(gc))
        lhs.append(cat0(a_hat[-1], r_hat[-1]))
        rhs.append(cat0(stack(bc * e_neg), stack(kc * e_neg)))
        bbar_t.append((bc * e_rest).T)
        kbar_t.append((kc * e_rest).T)
        dlast.append(jnp.exp(glast))
        vs.append(vc)
    big = [_dot_nt(x, y) for x, y in zip(lhs, rhs)]
    a_ab = [jnp.where(m["strict"], x[:CHUNK, :LANES], 0.0) for x in big]
    a_kr = [cat0(jnp.where(m["strict"], x[:CHUNK, LANES:], 0.0),
                 jnp.where(m["incl"], x[CHUNK:, LANES:], 0.0)) for x in big]
    a_rb = [jnp.where(m["incl"], x[CHUNK:, :LANES], 0.0) for x in big]
    x1 = [_dot(a, stack(v)) for a, v in zip(a_kr, vs)]
    t_inv = _tri_inv(a_ab, m)
    x2 = [_dot(t, cat1(stack(a), stack(x[:CHUNK]))) for t, a, x in zip(t_inv, a_hat, x1)]
    x3 = [_dot(a, cat1(stack(x[:, :LANES]), stack(x[:, LANES:]))) for a, x in zip(a_rb, x2)]
    for i, u in enumerate(units):
        w_t, u_t = x2[i][:, :LANES], x2[i][:, LANES:]
        phi[u] = m["eye128"] * dlast[i] + jnp.where(m["bd64"], _dot(bbar_t[i], w_t), 0.0)
        psi[u] = jnp.where(m["bd64"], _dot(cat1(bbar_t[i], kbar_t[i]), cat0(u_t, vs[i])), 0.0)
    for i, u in enumerate(units):
        q_t[u] = r_hat[i] + x3[i][:, :LANES]
        y0[u] = x3[i][:, LANES:] + x1[i][CHUNK:]

    ys = {}
    hs = [h_ref[b] for b in range(nb)]
    for u in units:
        b = u[0]
        qh = _dot(cat0(q_t[u], phi[u]), hs[b])
        ys[u] = y0[u] + qh[:CHUNK]
        hs[b] = qh[CHUNK:] + psi[u]
    for b in range(nb):
        h_ref[b] = hs[b]

    y = cat0(*[ys[(b, c)] for b in range(nb) for c in range(nch)])
    flat = lambda ref: ref[...].astype(F32).reshape(nb * rb, LANES)
    rr, kk_, vv = flat(r_ref), flat(k_ref), flat(v_ref)
    sums = _dot(cat0(y, rr * kk_ * rk_ref[...]), head_ones)
    yc = y - sums[:nb * rb] * (1.0 / RWKV_HEAD)
    var = _dot(yc * yc, head_ones) * (1.0 / RWKV_HEAD)
    yn = yc * lax.rsqrt(var + RWKV_LN_EPS) * lnw_ref[...] + lnb_ref[...]
    bonus = sums[nb * rb:] * vv
    o_ref[...] = ((yn + bonus) * flat(g_ref)).astype(BF16).reshape(nb, rb, LANES)


def _rwkv_core(prep, wts, batch, seq):
    rb = RB_RWKV
    tile = pl.BlockSpec((batch, rb, LANES), lambda p, j: (0, j, p))
    vec = pl.BlockSpec((1, LANES), lambda p, j: (0, p))
    out = pl.pallas_call(
        _rwkv_core_kernel,
        grid=(MIX // LANES, seq // rb),
        in_specs=[tile] * 7 + [vec] * 3,
        out_specs=tile,
        out_shape=jax.ShapeDtypeStruct((batch, seq, MIX), BF16),
        scratch_shapes=[pltpu.VMEM((batch, LANES, LANES), F32)],
        compiler_params=_params("parallel", "arbitrary"),
        name="rwkv_core",
    )(*[a.reshape(batch, seq, MIX) for a in prep], wts["r_k"], wts["ln_w"], wts["ln_b"])
    return out.reshape(batch * seq, MIX)


def _gdn_prep_kernel(pq, pk, pv, hq, hk, hv, ps, cwq, cwk, cwv, alog, dtb,
                     q_o, k_o, v_o, gb_o, *, seq, tm):
    first = (pl.program_id(0) * tm) % seq == 0
    ones = jnp.ones((LANES, LANES), BF16)

    def conv_silu(p_ref, h_ref, w_ref):
        halo = jnp.where(first, 0.0, h_ref[...].astype(F32))
        w = w_ref[...]
        y = p_ref[...].astype(F32) * w[3:4]
        for s, xs in zip((1, 2, 3), _shift_rows_bf16(p_ref[...], halo, (1, 2, 3))):
            y = y + xs * w[3 - s:4 - s]
        return _silu(y)

    def l2norm(y, scale):
        blocks = []
        for i in range(GDN_HEADS):
            blk = y[:, i * LANES:(i + 1) * LANES]
            ss = _dot(blk * blk, ones)
            blocks.append(blk * (lax.rsqrt(ss + GDN_EPS) * scale))
        return jnp.concatenate(blocks, axis=1)

    q_o[...] = l2norm(conv_silu(pq, hq, cwq), GDN_HEAD ** -0.5).astype(q_o.dtype)
    k_o[...] = l2norm(conv_silu(pk, hk, cwk), 1.0).astype(k_o.dtype)
    v_o[...] = conv_silu(pv, hv, cwv).astype(v_o.dtype)

    raw = ps[:, N_SMALL - LANES:]
    gval = -jnp.exp(alog[...]) * _softplus(raw + dtb[...])
    gcum = _dot_sel_l(_chunk_cumsum_matrix(tm), gval, 3)
    lane = _iota2(raw.shape, 1)
    is_g = (lane >= GDN_G_LANE) & (lane < GDN_G_LANE + GDN_HEADS)
    is_b = (lane >= GDN_B_LANE) & (lane < GDN_B_LANE + GDN_HEADS)
    gb_o[...] = jnp.where(is_g, gcum, jnp.where(is_b, _sigmoid(raw), 0.0))


def _gdn_prep(p, psm, wts, seq):
    m = p.shape[0]
    tm = TM_PREP
    tile = lambda cb: pl.BlockSpec((tm, MIX), lambda i, cb=cb: (i, cb))
    halo = lambda cb: pl.BlockSpec((16, MIX), lambda i, cb=cb: (jnp.maximum(i * (tm // 16) - 1, 0), cb))
    cw = pl.BlockSpec((4, MIX), lambda i: (0, 0))
    lv = pl.BlockSpec((1, LANES), lambda i: (0, 0))
    out = pl.BlockSpec((tm, MIX), lambda i: (i, 0))
    return pl.pallas_call(
        functools.partial(_gdn_prep_kernel, seq=seq, tm=tm),
        grid=(m // tm,),
        in_specs=[tile(4), tile(5), tile(6), halo(4), halo(5), halo(6),
                  pl.BlockSpec((tm, N_SMALL), lambda i: (i, 0)), cw, cw, cw, lv, lv],
        out_specs=[out, out, out, pl.BlockSpec((tm, LANES), lambda i: (i, 0))],
        out_shape=[jax.ShapeDtypeStruct((m, MIX), BF16)] * 3 + [jax.ShapeDtypeStruct((m, LANES), F32)],
        compiler_params=_params("parallel"),
        name="gdn_prep",
    )(p, p, p, p, p, p, psm, wts["cw_q"], wts["cw_k"], wts["cw_v"], wts["alog"], wts["dtb"])


def _gdn_core_kernel(q_ref, k_ref, v_ref, gb_ref, z_ref, nw_ref, o_ref, s_ref):
    @pl.when(pl.program_id(1) == 0)
    def _():
        s_ref[...] = jnp.zeros_like(s_ref)

    nb, rb, width = q_ref.shape
    nh = width // LANES
    units = [(b, hh) for hh in range(nh) for b in range(nb)]
    assert rb == 2 * CHUNK
    m = _pair_masks()
    ones = jnp.ones((LANES, LANES), BF16)
    cat0 = lambda *xs: jnp.concatenate(xs, axis=0)
    cat1 = lambda *xs: jnp.concatenate(xs, axis=1)

    def lane_bcast(x, lane):
        return jnp.broadcast_to(pltpu.roll(x, LANES - lane, 1)[:, 0:1], x.shape)

    nch = rb // CHUNK
    q_t, y0, phi, psi = {}, {}, {}, {}

    gf, ks, lhs, kbd, rhs, e_g, qs, gamma = [], [], [], [], [], [], [], []
    zero = jnp.zeros((CHUNK, LANES), F32)
    for b, hh in units:
        hl = slice(hh * LANES, (hh + 1) * LANES)
        head = nh * pl.program_id(0) + hh
        gbv = gb_ref[b]
        g = lane_bcast(gbv, GDN_G_LANE + head)
        beta = lane_bcast(gbv, GDN_B_LANE + head)
        q, k, v = (r[b, :, hl].astype(F32) for r in (q_ref, k_ref, v_ref))
        g_col = g[:CHUNK] * m["half0"] + g[CHUNK:] * m["half1"]
        diff = g_col - g.T[:CHUNK]
        gamma.append(jnp.where(m["incl"], jnp.exp(jnp.where(m["incl"], diff, 0.0)), 0.0))
        eg = jnp.exp(g)
        kb = k * beta
        gf.append(g)
        ks.append(k)
        qs.append(q)
        e_g.append(eg)
        lhs.append(cat0(cat1(kb[:CHUNK], kb[CHUNK:]), cat1(q[:CHUNK], q[CHUNK:])))
        kbd.append(cat0(cat1(k[:CHUNK], zero), cat1(zero, k[CHUNK:])))
        rhs.append(cat1(v * beta, kb * eg))
    kq = [_dot_nt(x, y) for x, y in zip(lhs, kbd)]
    a_neg = [-jnp.where(m["strict"], x[:CHUNK] * gm, 0.0) for x, gm in zip(kq, gamma)]
    a_qk = [x[CHUNK:] * gm for x, gm in zip(kq, gamma)]
    k_bar_t = [[(k[c * CHUNK:(c + 1) * CHUNK]
                 * jnp.exp(g[(c + 1) * CHUNK - 1:(c + 1) * CHUNK] - g[c * CHUNK:(c + 1) * CHUNK])).T
                for c in range(nch)] for k, g in zip(ks, gf)]
    t_inv = _tri_inv(a_neg, m)
    uw = [_dot(_stack(t, m), x) for t, x in zip(t_inv, rhs)]
    x3 = [_dot(_stack(a, m), x) for a, x in zip(a_qk, uw)]
    for i, u in enumerate(units):
        for c in range(nch):
            glast = gf[i][(c + 1) * CHUNK - 1:(c + 1) * CHUNK]
            pp = _dot(k_bar_t[i][c], uw[i][c * CHUNK:(c + 1) * CHUNK])
            phi[(u, c)] = m["eye128"] * jnp.exp(glast) - pp[:, LANES:]
            psi[(u, c)] = pp[:, :LANES]
    for i, u in enumerate(units):
        y0[u] = x3[i][:, :LANES]
        q_t[u] = qs[i] * e_g[i] - x3[i][:, LANES:]

    ys = {u: [] for u in units}
    st = {(b, hh): s_ref[hh, b] for b, hh in units}
    for c in range(nch):
        cs = slice(c * CHUNK, (c + 1) * CHUNK)
        for u in units:
            qs_ = _dot(cat0(q_t[u][cs], phi[(u, c)]), st[u])
            ys[u].append(y0[u][cs] + qs_[:CHUNK])
            st[u] = qs_[CHUNK:] + psi[(u, c)]
    for b, hh in units:
        s_ref[hh, b] = st[(b, hh)]

    for hh in range(nh):
        hl = slice(hh * LANES, (hh + 1) * LANES)
        y = cat0(*[yc for b in range(nb) for yc in ys[(b, hh)]])
        ms = _dot(y * y, ones) * (1.0 / GDN_HEAD)
        out = y * lax.rsqrt(ms + GDN_EPS) * nw_ref[...]
        z = z_ref[:, :, hl].astype(F32).reshape(nb * rb, LANES)
        o_ref[:, :, hl] = (out * _silu(z)).astype(BF16).reshape(nb, rb, LANES)


def _gdn_core(q, k, v, gb, p, norm_w, batch, seq):
    rb, nh = RB_GDN, GDN_HEADS_PER_STEP
    width = nh * LANES
    tile = pl.BlockSpec((batch, rb, width), lambda h, j: (0, j, h))
    z_col0 = 7 * (MIX // width)
    r3 = lambda a: a.reshape(batch, seq, a.shape[-1])
    out = pl.pallas_call(
        _gdn_core_kernel,
        grid=(GDN_HEADS // nh, seq // rb),
        in_specs=[tile, tile, tile,
                  pl.BlockSpec((batch, rb, LANES), lambda h, j: (0, j, 0)),
                  pl.BlockSpec((batch, rb, width), lambda h, j: (0, j, z_col0 + h)),
                  pl.BlockSpec((1, LANES), lambda h, j: (0, 0))],
        out_specs=tile,
        out_shape=jax.ShapeDtypeStruct((batch, seq, MIX), BF16),
        scratch_shapes=[pltpu.VMEM((nh, batch, LANES, LANES), F32)],
        compiler_params=_params("parallel", "arbitrary"),
        name="gdn_core",
    )(r3(q), r3(k), r3(v), r3(gb), r3(p), norm_w.reshape(1, LANES))
    return out.reshape(batch * seq, MIX)


def _merge_kernel(ya, yb, yc, wa, wb, wc, ga, gb, gc, o_ref):
    dot = lambda y, w: jnp.dot(y[...], w[0], preferred_element_type=F32)
    acc = ga[...].astype(F32) * dot(ya, wa)
    acc = acc + gb[...].astype(F32) * dot(yb, wb)
    acc = acc + gc[...].astype(F32) * dot(yc, wc)
    o_ref[...] = acc.astype(BF16)


def _merge(ya, yb, yc, wa, wb, wc, l, gates):
    m = ya.shape[0]
    d = wa.shape[2]
    tm = TM_LN
    y = pl.BlockSpec((tm, MIX), lambda i: (i, 0))
    w = pl.BlockSpec((1, MIX, d), lambda i: (l, 0, 0), pipeline_mode=pl.Buffered(1))
    g = lambda br: pl.BlockSpec((tm, d), lambda i, br=br: (i, br))
    return pl.pallas_call(
        _merge_kernel,
        grid=(m // tm,),
        in_specs=[y, y, y, w, w, w, g(0), g(1), g(2)],
        out_specs=pl.BlockSpec((tm, d), lambda i: (i, 0)),
        out_shape=jax.ShapeDtypeStruct((m, d), BF16),
        compiler_params=_params("parallel"),
        name="merge",
    )(ya, yb, yc, wa, wb, wc, gates, gates, gates)


def _proj_ln_kernel(*refs, emit_h):
    a_ref, w_ref, x_ref, gt_ref, lnw_ref, lnb_ref = refs[:6]
    y = jnp.dot(a_ref[...], w_ref[0], preferred_element_type=F32)
    xn = _resid_ln(x_ref[...], y, gt_ref[0], lnw_ref[...], lnb_ref[...])
    if emit_h:
        sc_ref, sh_ref, xo_ref, ho_ref = refs[6:]
        ho_ref[...] = (xn * (1.0 + sc_ref[0]) + sh_ref[0]).astype(BF16)
    else:
        (xo_ref,) = refs[6:]
    xo_ref[...] = xn


def _proj_ln(a, w, l, x, gt, lnw, lnb, nxt, seq, *, tm, name):
    m, k = a.shape
    d = w.shape[2]
    per = seq // tm
    emit_h = nxt is not None
    row = pl.BlockSpec((tm, d), lambda i: (i, 0))
    mod = pl.BlockSpec((1, 1, d), lambda i: (i // per, 0, 0))
    vec = pl.BlockSpec((1, d), lambda i: (0, 0))
    in_specs = [pl.BlockSpec((tm, k), lambda i: (i, 0)),
                pl.BlockSpec((1, k, d), lambda i: (l, 0, 0), pipeline_mode=pl.Buffered(1)),
                row, mod, vec, vec]
    args = [a, w, x, gt, lnw.reshape(1, d), lnb.reshape(1, d)]
    out_specs, out_shape = [row], [jax.ShapeDtypeStruct((m, d), F32)]
    if emit_h:
        in_specs += [mod, mod]
        args += list(nxt)
        out_specs.append(row)
        out_shape.append(jax.ShapeDtypeStruct((m, d), BF16))
    res = pl.pallas_call(
        functools.partial(_proj_ln_kernel, emit_h=emit_h),
        grid=(m // tm,),
        in_specs=in_specs,
        out_specs=out_specs,
        out_shape=out_shape,
        compiler_params=_params("parallel"),
        name=name,
    )(*args)
    return (res[0], res[1]) if emit_h else (res[0], None)


def _swiglu_kernel(a_ref, wg_ref, wu_ref, o_ref):
    a = a_ref[...]
    g = jnp.dot(a, wg_ref[0], preferred_element_type=F32)
    u = jnp.dot(a, wu_ref[0], preferred_element_type=F32)
    o_ref[...] = (_silu(g) * u).astype(BF16)


def _swiglu(h, w_up, l):
    m, k = h.shape
    dff = w_up.shape[2] // 2
    tm, tn = TM_MM, 512
    nj = dff // tn
    return pl.pallas_call(
        _swiglu_kernel,
        grid=(m // tm, nj),
        in_specs=[pl.BlockSpec((tm, k), lambda i, j: (i, 0)),
                  pl.BlockSpec((1, k, tn), lambda i, j: (l, 0, j)),
                  pl.BlockSpec((1, k, tn), lambda i, j: (l, 0, nj + j))],
        out_specs=pl.BlockSpec((tm, tn), lambda i, j: (i, j)),
        out_shape=jax.ShapeDtypeStruct((m, dff), BF16),
        compiler_params=_params("parallel", "parallel"),
        name="ffn_up_swiglu",
    )(h, w_up, w_up)


def _layer_weights(l, w_in_t, rwkv_mu, rwkv_w0, rwkv_w2, rwkv_a0, rwkv_a2, rwkv_g2, rwkv_k_k, rwkv_k_a,
                   rwkv_r_k, rwkv_ln_w, rwkv_ln_b, rwkv_v0, rwkv_v1, rwkv_v2, gdn_conv_w, gdn_a_log,
                   gdn_dt_bias):
    c = MIX
    rows = lambda lo, hi: w_in_t[l, lo:hi]
    o_rwkv = c
    o_lora = o_rwkv + 3 * c
    n_lora = rwkv_w2.shape[1] + rwkv_a2.shape[1] + rwkv_g2.shape[1]
    o_gdn = o_lora + n_lora
    o_ab = o_gdn + 4 * c
    o_gate = o_ab + 2 * GDN_HEADS
    pad_rows = N_SMALL - n_lora - 2 * GDN_HEADS
    w_main = (o_lora, rows(o_gdn, o_ab))
    assert n_lora - (N_SMALL - LANES) == GDN_G_LANE and GDN_B_LANE == GDN_G_LANE + GDN_HEADS
    w_small = jnp.concatenate([rows(o_lora, o_gdn), rows(o_ab, o_gate),
                               jnp.zeros((pad_rows, w_in_t.shape[2]), BF16)], axis=0)
    w_gate = rows(o_gate, w_in_t.shape[1])

    mu = rwkv_mu[l]
    row = lambda v: v.reshape(1, -1)
    n_w, n_a, n_g = rwkv_w2.shape[1], rwkv_a2.shape[1], rwkv_g2.shape[1]
    padrows = lambda w, lo: jnp.zeros((N_SMALL, c), F32).at[lo:lo + w.shape[0]].set(w).astype(BF16)
    lane_vec = lambda v, lo: jnp.zeros((1, LANES), F32).at[0, lo:lo + v.shape[0]].set(v)
    rw = {
        "mu_r": row(mu[:c]), "mu_k": row(mu[c:2 * c]), "mu_v": row(mu[2 * c:3 * c]),
        "mu_s": jnp.zeros((1, N_SMALL), F32).at[0, :n_lora].set(mu[3 * c:]),
        "w0": row(rwkv_w0[l]), "a0": row(rwkv_a0[l]), "k_k": row(rwkv_k_k[l]), "k_a": row(rwkv_k_a[l]),
        "w2p": padrows(rwkv_w2[l], 0), "a2p": padrows(rwkv_a2[l], n_w), "g2p": padrows(rwkv_g2[l], n_w + n_a),
        "r_k": row(rwkv_r_k[l]), "ln_w": row(rwkv_ln_w[l]), "ln_b": row(rwkv_ln_b[l]),
    }
    if l > 0:
        nv = rwkv_v1.shape[2]
        rw["v0"] = row(rwkv_v0[l - 1])
        rw["v1p"] = jnp.zeros((c, LANES), F32).at[:, :nv].set(rwkv_v1[l - 1]).astype(BF16)
        rw["v2p"] = jnp.zeros((LANES, c), F32).at[:nv].set(rwkv_v2[l - 1]).astype(BF16)
    cw = gdn_conv_w[l]
    gd = {"cw_q": cw[:, :c], "cw_k": cw[:, c:2 * c], "cw_v": cw[:, 2 * c:],
          "alog": lane_vec(gdn_a_log[l], GDN_G_LANE), "dtb": lane_vec(gdn_dt_bias[l], GDN_G_LANE)}
    return w_main, w_small, w_gate, rw, gd


def kernel(x, c, ada_w, ada_b, w_in, pool_w, pool_scale, rwkv_mu, rwkv_w0, rwkv_w2, rwkv_a0, rwkv_a2, rwkv_g2, rwkv_k_k, rwkv_k_a, rwkv_r_k, rwkv_ln_w, rwkv_ln_b, rwkv_v0, rwkv_v1, rwkv_v2, gdn_conv_w, gdn_a_log, gdn_dt_bias, gdn_norm_w, w_branch_a, w_branch_b, w_branch_c, w_out, ln1_w, ln1_b, ffn_w_up, ffn_w_down, ln2_w, ln2_b):
    batch, seq, d = x.shape
    assert seq % TM_MM == 0 and d == 2 * MIX
    m = batch * seq
    xf = x.reshape(m, d)

    mod = _ada(c, ada_w, ada_b)
    mods = [[mod[l, :, i * d:(i + 1) * d].reshape(batch, 1, d) for i in range(6)]
            for l in range(DEPTH)]

    w_in_t = jnp.swapaxes(w_in, 1, 2).astype(BF16)
    wa, wb, wc = (w.astype(BF16) for w in (w_branch_a, w_branch_b, w_branch_c))
    w_out_b, w_up_b, w_down_b = (w.astype(BF16) for w in (w_out, ffn_w_up, ffn_w_down))
    v_first = None
    for l in range(DEPTH):
        sh_m, sc_m, gt_m, sh_f, sc_f, gt_f = mods[l]
        (n_main1, w_main2), w_small, w_gate, rw, gd = _layer_weights(
            l, w_in_t, rwkv_mu, rwkv_w0, rwkv_w2, rwkv_a0, rwkv_a2, rwkv_g2, rwkv_k_k, rwkv_k_a,
            rwkv_r_k, rwkv_ln_w, rwkv_ln_b, rwkv_v0, rwkv_v1, rwkv_v2, gdn_conv_w, gdn_a_log,
            gdn_dt_bias)

        if l == 0:
            h, psm = _modulate_small(xf, sc_m, sh_m, w_small, seq)
        else:
            psm = _matmul_nt(h, w_small, tn=N_SMALL, out_dtype=F32, name="in_proj_small")
        p = _matmul_nt2(h, w_in_t, l, n_main1, w_main2, tn=2048, out_dtype=BF16, name="in_proj_main")
        gates = _matmul_nt(h, w_gate, tn=2048, out_dtype=BF16, act="sigmoid", name="in_proj_gates")

        y_a = _pool(p, pool_w[l], pool_scale[l], seq)
        prep = _rwkv_prep(p, psm, rw, v_first, seq)
        if l == 0:
            v_first = prep[2]
        y_b = _rwkv_core(prep, rw, batch, seq)
        gq, gk, gv, ggb = _gdn_prep(p, psm, gd, seq)
        y_c = _gdn_core(gq, gk, gv, ggb, p, gdn_norm_w[l], batch, seq)

        merged = _merge(y_a, y_b, y_c, wa, wb, wc, l, gates)
        xf, h = _proj_ln(merged, w_out_b, l, xf, gt_m, ln1_w[l], ln1_b[l],
                         (sc_f, sh_f), seq, tm=512, name="out_proj_ln")

        act = _swiglu(h, w_up_b, l)
        nxt = (mods[l + 1][1], mods[l + 1][0]) if l + 1 < DEPTH else None
        xf, h = _proj_ln(act, w_down_b, l, xf, gt_f, ln2_w[l], ln2_b[l],
                         nxt, seq, tm=256, name="ffn_down_ln")
    return xf.reshape(batch, seq, d)
```

```python
import functools
import math

import jax
import jax.numpy as jnp
from jax import lax
from jax.experimental import pallas as pl
from jax.experimental.pallas import tpu as pltpu

F32 = jnp.float32
BF16 = jnp.bfloat16

DEPTH = 2
DN_ALPHA = (2 * DEPTH) ** 0.25
LN_EPS = 1e-5
MIX = 1024
POOL_WINDOWS = (2, 4, 8, 16)
POOL_GROUP = 256
RWKV_HEAD = 64
RWKV_LN_EPS = RWKV_HEAD * 1e-5
GDN_HEAD = 128
GDN_HEADS = 8
GDN_EPS = 1e-6
CHUNK = 64
LANES = 128
N_SMALL = 384
GDN_G_LANE = 32
GDN_B_LANE = 40
VMEM_LIMIT = 56 * 1024 * 1024

TM_MM = 1024
TM_LN = 512
TM_PREP = 256
RB_RWKV = 512
RB_GDN = 128
GDN_HEADS_PER_STEP = 8


def _dot(a, b):
    return jnp.dot(a.astype(BF16), b.astype(BF16), preferred_element_type=F32)


def _dot_nt(a, b):
    return lax.dot_general(a.astype(BF16), b.astype(BF16), (((1,), (1,)), ((), ())),
                           preferred_element_type=F32)


def _split(a, n):
    terms, rest = [], a
    for _ in range(n):
        t = rest.astype(BF16)
        terms.append(t)
        rest = rest - t.astype(F32)
    return terms


def _dot3(a, b, nt=False):
    d = _dot_nt if nt else _dot
    a1, a2 = _split(a, 2)
    b1, b2 = _split(b, 2)
    return d(a1, b1) + (d(a1, b2) + d(a2, b1))


def _dot_sel_l(sel, b, n):
    out = None
    for t in _split(b, n):
        p = jnp.dot(sel, t, preferred_element_type=F32)
        out = p if out is None else out + p
    return out


def _sigmoid(x):
    return 0.5 * jnp.tanh(0.5 * x) + 0.5


def _silu(x):
    return x * _sigmoid(x)


def _softplus(x):
    return jnp.maximum(x, 0.0) + jnp.log(1.0 + jnp.exp(-jnp.abs(x)))


def _iota2(shape, dim):
    return lax.broadcasted_iota(jnp.int32, shape, dim)


def _pair_masks():
    r = _iota2((CHUNK, LANES), 0)
    c = _iota2((CHUNK, LANES), 1) & (CHUNK - 1)
    same = lambda s: (r >> s) == (c >> s)
    r2 = _iota2((LANES, LANES), 0)
    c2 = _iota2((LANES, LANES), 1)
    lane = _iota2((1, LANES), 1)
    return {
        "eye": (r == c).astype(F32),
        "bd8": same(3),
        "off16": same(4) & jnp.logical_not(same(3)),
        "off32": same(5) & jnp.logical_not(same(4)),
        "off64": jnp.logical_not(same(5)),
        "strict": c < r,
        "incl": c <= r,
        "eye128": (r2 == c2).astype(F32),
        "bd64": (r2 >> 6) == (c2 >> 6),
        "low": _iota2((CHUNK, LANES), 1) < CHUNK,
        "half0": (lane < CHUNK).astype(F32),
        "half1": (lane >= CHUNK).astype(F32),
    }


def _stack(x, m):
    xb = x.astype(BF16)
    zero = jnp.zeros_like(xb)
    return jnp.concatenate([jnp.where(m["low"], xb, zero), jnp.where(m["low"], zero, xb)], axis=0)


def _chunk_cumsum_matrix(n):
    r = _iota2((n, n), 0)
    c = _iota2((n, n), 1)
    return (((r >> 6) == (c >> 6)) & (c <= r)).astype(BF16)


def _tri_inv(mats, m):
    bd = lambda x: _stack(x, m)
    a8 = [jnp.where(m["bd8"], a, 0.0) for a in mats]
    s = [m["eye"] + x for x in a8]
    p = [_dot(x, bd(x)) for x in a8]
    q = [_dot(pi, jnp.concatenate([bd(si), bd(pi)], axis=1)) for pi, si in zip(p, s)]
    s = [si + qi[:, :LANES] for si, qi in zip(s, q)]
    s = [si + _dot(qi[:, LANES:], bd(si)) for si, qi in zip(s, q)]
    for key in ("off16", "off32", "off64"):
        t = [_dot(jnp.where(m[key], a, 0.0), bd(si)) for a, si in zip(mats, s)]
        s = [si + _dot(si, bd(ti)) for si, ti in zip(s, t)]
    return s


def _shift_rows(x, halo, s):
    halo = halo[halo.shape[0] - 8:]
    xr = pltpu.roll(x, s, 0)
    hr = pltpu.roll(halo, s, 0)
    rows = _iota2(halo.shape, 0)
    top = jnp.where(rows < s, hr, xr[:8])
    return top if x.shape[0] == 8 else jnp.concatenate([top, xr[8:]], axis=0)


def _shift_rows_bf16(x, halo, shifts):
    tm = x.shape[0]
    r = _iota2((tm, tm), 0)
    c = _iota2((tm, tm), 1)
    sel = jnp.concatenate([(r - c == s).astype(BF16) for s in shifts], axis=0)
    moved = jnp.dot(sel, x, preferred_element_type=F32)
    x_top = x[:16].astype(F32)[:8]
    outs = []
    for j, s in enumerate(shifts):
        top = _shift_rows(x_top, halo, s)
        outs.append(jnp.concatenate([top, moved[j * tm + 8:(j + 1) * tm]], axis=0))
    return outs


def _resid_ln(x, y, gt, lnw, lnb):
    z = DN_ALPHA * x + (1.0 + gt) * y
    mu = jnp.mean(z, axis=-1, keepdims=True)
    zc = z - mu
    var = jnp.mean(zc * zc, axis=-1, keepdims=True)
    return zc * lax.rsqrt(var + LN_EPS) * lnw + lnb


def _params(*sem):
    return pltpu.CompilerParams(dimension_semantics=sem, vmem_limit_bytes=VMEM_LIMIT)


def _ada_kernel(c_ref, w_ref, b_ref, o_ref):
    c = c_ref[...]
    o_ref[0] = _dot3(_silu(c), w_ref[0]) + b_ref[0]


def _ada(c, ada_w, ada_b):
    nl, d, n = ada_w.shape
    b = c.shape[0]
    tn = 512
    cp = jnp.zeros((8, d), F32).at[:b].set(c)
    out = pl.pallas_call(
        _ada_kernel,
        grid=(nl, n // tn),
        in_specs=[pl.BlockSpec((8, d), lambda l, j: (0, 0)),
                  pl.BlockSpec((1, d, tn), lambda l, j: (l, 0, j)),
                  pl.BlockSpec((1, 1, tn), lambda l, j: (l, 0, j))],
        out_specs=pl.BlockSpec((1, 8, tn), lambda l, j: (l, 0, j)),
        out_shape=jax.ShapeDtypeStruct((nl, 8, n), F32),
        compiler_params=_params("parallel", "parallel"),
        name="ada_mod",
    )(cp, ada_w, ada_b.reshape(nl, 1, n))
    return out[:, :b]


def _modulate_small_kernel(x_ref, sc_ref, sh_ref, wt_ref, h_ref, ps_ref):
    h = (x_ref[...] * (1.0 + sc_ref[0]) + sh_ref[0]).astype(BF16)
    h_ref[...] = h
    ps_ref[...] = lax.dot_general(h, wt_ref[...], (((1,), (1,)), ((), ())), preferred_element_type=F32)


def _modulate_small(x, sc, sh, w_small_t, seq):
    m, d = x.shape
    n = w_small_t.shape[0]
    tm = TM_LN
    per = seq // tm
    mod = pl.BlockSpec((1, 1, d), lambda i: (i // per, 0, 0))
    return pl.pallas_call(
        _modulate_small_kernel,
        grid=(m // tm,),
        in_specs=[pl.BlockSpec((tm, d), lambda i: (i, 0)), mod, mod,
                  pl.BlockSpec((n, d), lambda i: (0, 0))],
        out_specs=[pl.BlockSpec((tm, d), lambda i: (i, 0)), pl.BlockSpec((tm, n), lambda i: (i, 0))],
        out_shape=[jax.ShapeDtypeStruct((m, d), BF16), jax.ShapeDtypeStruct((m, n), F32)],
        compiler_params=_params("parallel"),
        name="modulate_small",
    )(x, sc, sh, w_small_t)


def _mm_kernel(a_ref, bt_ref, o_ref, *, act):
    acc = lax.dot_general(a_ref[...], bt_ref[...], (((1,), (1,)), ((), ())),
                          preferred_element_type=F32)
    if act == "sigmoid":
        acc = _sigmoid(acc)
    o_ref[...] = acc.astype(o_ref.dtype)


def _matmul_nt(a, bt, *, tn, out_dtype, act=None, name):
    m, k = a.shape
    n = bt.shape[0]
    tm = TM_MM
    return pl.pallas_call(
        functools.partial(_mm_kernel, act=act),
        grid=(m // tm, n // tn),
        in_specs=[pl.BlockSpec((tm, k), lambda i, j: (i, 0)),
                  pl.BlockSpec((tn, k), lambda i, j: (j, 0))],
        out_specs=pl.BlockSpec((tm, tn), lambda i, j: (i, j)),
        out_shape=jax.ShapeDtypeStruct((m, n), out_dtype),
        compiler_params=_params("parallel", "parallel"),
        name=name,
    )(a, bt)


def _mm2_kernel(a_ref, b1_ref, b2_ref, o_ref, *, nj1):
    dims = (((1,), (1,)), ((), ()))

    @pl.when(pl.program_id(1) < nj1)
    def _():
        o_ref[...] = lax.dot_general(a_ref[...], b1_ref[0], dims,
                                     preferred_element_type=F32).astype(o_ref.dtype)

    @pl.when(pl.program_id(1) >= nj1)
    def _():
        o_ref[...] = lax.dot_general(a_ref[...], b2_ref[...], dims,
                                     preferred_element_type=F32).astype(o_ref.dtype)


def _matmul_nt2(a, w_all, l, n1, b2, *, tn, out_dtype, name):
    m, k = a.shape
    n2 = b2.shape[0]
    tm = TM_MM
    nj1 = n1 // tn
    return pl.pallas_call(
        functools.partial(_mm2_kernel, nj1=nj1),
        grid=(m // tm, (n1 + n2) // tn),
        in_specs=[pl.BlockSpec((tm, k), lambda i, j: (i, 0)),
                  pl.BlockSpec((1, tn, k), lambda i, j: (l, jnp.minimum(j, nj1 - 1), 0)),
                  pl.BlockSpec((tn, k), lambda i, j: (jnp.maximum(j - nj1, 0), 0))],
        out_specs=pl.BlockSpec((tm, tn), lambda i, j: (i, j)),
        out_shape=jax.ShapeDtypeStruct((m, n1 + n2), out_dtype),
        compiler_params=_params("parallel", "parallel"),
        name=name,
    )(a, w_all, b2)


def _pool_kernel(p_ref, halo_ref, w_ref, scale_ref, o_ref, *, seq, tm):
    t0 = (pl.program_id(0) * tm) % seq
    x = p_ref[...].astype(F32)
    halo = jnp.where(t0 == 0, 0.0, halo_ref[...].astype(F32))
    ext = jnp.concatenate([halo, x], axis=0)
    sums = [ext]
    for sh in (1, 2, 4, 8):
        sums.append(sums[-1] + pltpu.roll(sums[-1], sh, 0))
    t = t0 + _iota2((tm, POOL_GROUP), 0)
    ys = []
    for g, win in enumerate(POOL_WINDOWS):
        cols = slice(g * POOL_GROUP, (g + 1) * POOL_GROUP)
        cnt = jnp.minimum(t + 1, win).astype(F32)
        pooled = sums[g + 1][16:, cols] / cnt - x[:, cols]
        ys.append(_dot(pooled, w_ref[g]))
    o_ref[...] = (jnp.concatenate(ys, axis=1) * scale_ref[...]).astype(BF16)


def _pool(p, pool_w, pool_scale, seq):
    m = p.shape[0]
    tm = TM_LN
    return pl.pallas_call(
        functools.partial(_pool_kernel, seq=seq, tm=tm),
        grid=(m // tm,),
        in_specs=[pl.BlockSpec((tm, MIX), lambda i: (i, 0)),
                  pl.BlockSpec((16, MIX), lambda i: (jnp.maximum(i * (tm // 16) - 1, 0), 0)),
                  pl.BlockSpec(pool_w.shape, lambda i: (0, 0, 0)),
                  pl.BlockSpec((1, MIX), lambda i: (0, 0))],
        out_specs=pl.BlockSpec((tm, MIX), lambda i: (i, 0)),
        out_shape=jax.ShapeDtypeStruct((m, MIX), BF16),
        compiler_params=_params("parallel"),
        name="pool_mixer",
    )(p, p, pool_w.astype(BF16), pool_scale.reshape(1, MIX))


def _rwkv_prep_kernel(*refs, seq, tm, has_vres):
    (pr, pk, pv, hr, hk, hv, ps, hs, mur, muk, muv, mus, w0, a0, kkp, kap,
     w2p, a2p, g2p) = refs[:19]
    if has_vres:
        vf, v0, v1p, v2p = refs[19:23]
        outs = refs[23:]
    else:
        outs = refs[19:]
    r_o, k_o, v_o, kk_o, a_o, ld_o, g_o = outs

    first = (pl.program_id(0) * tm) % seq == 0

    def lerp(p_ref, h_ref, mu_ref):
        halo = jnp.where(first, 0.0, h_ref[...].astype(F32))
        if p_ref.dtype == BF16:
            (prev,) = _shift_rows_bf16(p_ref[...], halo, (1,))
        else:
            prev = _shift_rows(p_ref[...], halo, 1)
        p = p_ref[...].astype(F32)
        return p + (prev - p) * mu_ref[...]

    xr = lerp(pr, hr, mur)
    xk = lerp(pk, hk, muk)
    xv = lerp(pv, hv, muv)
    xs = lerp(ps, hs, mus)

    z = w0[...] + _dot(jnp.tanh(xs), w2p[...])
    t = jnp.exp(-jnp.abs(z))
    log_decay = jnp.where(z < 0.0, t, 1.0) * (-math.exp(-0.5) / (1.0 + t))
    a = _sigmoid(a0[...] + _dot(xs, a2p[...]))
    g = _dot(_sigmoid(xs), g2p[...])
    if has_vres:
        gate = _sigmoid(v0[...] + _dot(_dot(xv, v1p[...]), v2p[...]))
        xv = xv + (vf[...].astype(F32) - xv) * gate

    kk = xk * kkp[...]
    r2 = _iota2((LANES, LANES), 0)
    c2 = _iota2((LANES, LANES), 1)
    head_ones = ((r2 >> 6) == (c2 >> 6)).astype(BF16)
    kk2 = kk * kk
    ss = jnp.concatenate(
        [_dot(kk2[:, i * LANES:(i + 1) * LANES], head_ones) for i in range(MIX // LANES)],
        axis=1)
    kk = kk * lax.rsqrt(jnp.maximum(ss, 1e-24))

    r_o[...] = xr.astype(r_o.dtype)
    k_o[...] = (xk * (1.0 + (a - 1.0) * kap[...])).astype(k_o.dtype)
    v_o[...] = xv.astype(v_o.dtype)
    kk_o[...] = kk.astype(kk_o.dtype)
    a_o[...] = a.astype(a_o.dtype)
    ld_o[...] = _dot_sel_l(_chunk_cumsum_matrix(tm), log_decay, 2)
    g_o[...] = g.astype(g_o.dtype)


def _rwkv_prep(p, psm, wts, v_first, seq):
    m = p.shape[0]
    tm = TM_PREP
    has_vres = v_first is not None
    tile = lambda cb: pl.BlockSpec((tm, MIX), lambda i, cb=cb: (i, cb))
    halo = lambda cb: pl.BlockSpec((16, MIX), lambda i, cb=cb: (jnp.maximum(i * (tm // 16) - 1, 0), cb))
    vec = pl.BlockSpec((1, MIX), lambda i: (0, 0))
    full = lambda arr: pl.BlockSpec(arr.shape, lambda i: (0,) * arr.ndim)
    in_specs = [tile(1), tile(2), tile(3), halo(1), halo(2), halo(3),
                pl.BlockSpec((tm, N_SMALL), lambda i: (i, 0)),
                pl.BlockSpec((8, N_SMALL), lambda i: (jnp.maximum(i * (tm // 8) - 1, 0), 0)),
                vec, vec, vec, pl.BlockSpec((1, N_SMALL), lambda i: (0, 0)),
                vec, vec, vec, vec,
                full(wts["w2p"]), full(wts["a2p"]), full(wts["g2p"])]
    args = [p, p, p, p, p, p, psm, psm, wts["mu_r"], wts["mu_k"], wts["mu_v"], wts["mu_s"],
            wts["w0"], wts["a0"], wts["k_k"], wts["k_a"], wts["w2p"], wts["a2p"], wts["g2p"]]
    if has_vres:
        in_specs += [pl.BlockSpec((tm, MIX), lambda i: (i, 0)), vec, full(wts["v1p"]), full(wts["v2p"])]
        args += [v_first, wts["v0"], wts["v1p"], wts["v2p"]]
    out = pl.BlockSpec((tm, MIX), lambda i: (i, 0))
    return pl.pallas_call(
        functools.partial(_rwkv_prep_kernel, seq=seq, tm=tm, has_vres=has_vres),
        grid=(m // tm,),
        in_specs=in_specs,
        out_specs=[out] * 7,
        out_shape=[jax.ShapeDtypeStruct((m, MIX), dt) for dt in (BF16,) * 5 + (F32, BF16)],
        compiler_params=_params("parallel"),
        name="rwkv_prep",
    )(*args)


def _rwkv_core_kernel(r_ref, k_ref, v_ref, kk_ref, a_ref, gc_ref, g_ref, rk_ref, lnw_ref, lnb_ref,
                      o_ref, h_ref):
    @pl.when(pl.program_id(1) == 0)
    def _():
        h_ref[...] = jnp.zeros_like(h_ref)

    nb, rb, _ = r_ref.shape
    nch = rb // CHUNK
    units = [(b, c) for c in range(nch) for b in range(nb)]
    m = _pair_masks()
    head_ones = m["bd64"].astype(BF16)
    stack = lambda x: _stack(x, m)
    cat0 = lambda *xs: jnp.concatenate(xs, axis=0)
    cat1 = lambda *xs: jnp.concatenate(xs, axis=1)

    def load(ref, u):
        b, c = u
        return ref[b, c * CHUNK:(c + 1) * CHUNK, :].astype(F32)

    first_row = _iota2((CHUNK, LANES), 0) == 0
    q_t, y0, phi, psi = {}, {}, {}, {}

    a_hat, r_hat, lhs, rhs, bbar_t, kbar_t, dlast, vs = [], [], [], [], [], [], [], []
    for u in units:
        gc = load(gc_ref, u)
        g_prev = jnp.where(first_row, 0.0, pltpu.roll(gc, 1, 0))
        kc, vc, kkc = load(k_ref, u), load(v_ref, u), load(kk_ref, u)
        bc = kkc * load(a_ref, u)
        glast = gc[CHUNK - 1:CHUNK]
        e_neg = jnp.exp(-gc)
        e_rest = jnp.exp(glast - gc)
        a_hat.append(-kkc * jnp.exp(g_prev))
        r_hat.append(load(r_ref, u) * jnp.exp(gc))
        lhs.append(cat0(a_hat[-1], r_hat[-1]))
        rhs.append(cat0(stack(bc * e_neg), stack(kc * e_neg)))
        bbar_t.append((bc * e_rest).T)
        kbar_t.append((kc * e_rest).T)
        dlast.append(jnp.exp(glast))
        vs.append(vc)
    big = [_dot_nt(x, y) for x, y in zip(lhs, rhs)]
    a_ab = [jnp.where(m["strict"], x[:CHUNK, :LANES], 0.0) for x in big]
    a_kr = [cat0(jnp.where(m["strict"], x[:CHUNK, LANES:], 0.0),
                 jnp.where(m["incl"], x[CHUNK:, LANES:], 0.0)) for x in big]
    a_rb = [jnp.where(m["incl"], x[CHUNK:, :LANES], 0.0) for x in big]
    x1 = [_dot(a, stack(v)) for a, v in zip(a_kr, vs)]
    t_inv = _tri_inv(a_ab, m)
    x2 = [_dot(t, cat1(stack(a), stack(x[:CHUNK]))) for t, a, x in zip(t_inv, a_hat, x1)]
    x3 = [_dot(a, cat1(stack(x[:, :LANES]), stack(x[:, LANES:]))) for a, x in zip(a_rb, x2)]
    for i, u in enumerate(units):
        w_t, u_t = x2[i][:, :LANES], x2[i][:, LANES:]
        phi[u] = m["eye128"] * dlast[i] + jnp.where(m["bd64"], _dot(bbar_t[i], w_t), 0.0)
        psi[u] = jnp.where(m["bd64"], _dot(cat1(bbar_t[i], kbar_t[i]), cat0(u_t, vs[i])), 0.0)
    for i, u in enumerate(units):
        q_t[u] = r_hat[i] + x3[i][:, :LANES]
        y0[u] = x3[i][:, LANES:] + x1[i][CHUNK:]

    ys = {}
    hs = [h_ref[b] for b in range(nb)]
    for u in units:
        b = u[0]
        qh = _dot(cat0(q_t[u], phi[u]), hs[b])
        ys[u] = y0[u] + qh[:CHUNK]
        hs[b] = qh[CHUNK:] + psi[u]
    for b in range(nb):
        h_ref[b] = hs[b]

    y = cat0(*[ys[(b, c)] for b in range(nb) for c in range(nch)])
    flat = lambda ref: ref[...].astype(F32).reshape(nb * rb, LANES)
    rr, kk_, vv = flat(r_ref), flat(k_ref), flat(v_ref)
    sums = _dot(cat0(y, rr * kk_ * rk_ref[...]), head_ones)
    yc = y - sums[:nb * rb] * (1.0 / RWKV_HEAD)
    var = _dot(yc * yc, head_ones) * (1.0 / RWKV_HEAD)
    yn = yc * lax.rsqrt(var + RWKV_LN_EPS) * lnw_ref[...] + lnb_ref[...]
    bonus = sums[nb * rb:] * vv
    o_ref[...] = ((yn + bonus) * flat(g_ref)).astype(BF16).reshape(nb, rb, LANES)


def _rwkv_core(prep, wts, batch, seq):
    rb = RB_RWKV
    tile = pl.BlockSpec((batch, rb, LANES), lambda p, j: (0, j, p))
    vec = pl.BlockSpec((1, LANES), lambda p, j: (0, p))
    out = pl.pallas_call(
        _rwkv_core_kernel,
        grid=(MIX // LANES, seq // rb),
        in_specs=[tile] * 7 + [vec] * 3,
        out_specs=tile,
        out_shape=jax.ShapeDtypeStruct((batch, seq, MIX), BF16),
        scratch_shapes=[pltpu.VMEM((batch, LANES, LANES), F32)],
        compiler_params=_params("parallel", "arbitrary"),
        name="rwkv_core",
    )(*[a.reshape(batch, seq, MIX) for a in prep], wts["r_k"], wts["ln_w"], wts["ln_b"])
    return out.reshape(batch * seq, MIX)


def _gdn_prep_kernel(pq, pk, pv, hq, hk, hv, ps, cwq, cwk, cwv, alog, dtb,
                     q_o, k_o, v_o, gb_o, *, seq, tm):
    first = (pl.program_id(0) * tm) % seq == 0
    ones = jnp.ones((LANES, LANES), BF16)

    def conv_silu(p_ref, h_ref, w_ref):
        halo = jnp.where(first, 0.0, h_ref[...].astype(F32))
        w = w_ref[...]
        y = p_ref[...].astype(F32) * w[3:4]
        for s, xs in zip((1, 2, 3), _shift_rows_bf16(p_ref[...], halo, (1, 2, 3))):
            y = y + xs * w[3 - s:4 - s]
        return _silu(y)

    def l2norm(y, scale):
        blocks = []
        for i in range(GDN_HEADS):
            blk = y[:, i * LANES:(i + 1) * LANES]
            ss = _dot(blk * blk, ones)
            blocks.append(blk * (lax.rsqrt(ss + GDN_EPS) * scale))
        return jnp.concatenate(blocks, axis=1)

    q_o[...] = l2norm(conv_silu(pq, hq, cwq), GDN_HEAD ** -0.5).astype(q_o.dtype)
    k_o[...] = l2norm(conv_silu(pk, hk, cwk), 1.0).astype(k_o.dtype)
    v_o[...] = conv_silu(pv, hv, cwv).astype(v_o.dtype)

    raw = ps[:, N_SMALL - LANES:]
    gval = -jnp.exp(alog[...]) * _softplus(raw + dtb[...])
    gcum = _dot_sel_l(_chunk_cumsum_matrix(tm), gval, 3)
    lane = _iota2(raw.shape, 1)
    is_g = (lane >= GDN_G_LANE) & (lane < GDN_G_LANE + GDN_HEADS)
    is_b = (lane >= GDN_B_LANE) & (lane < GDN_B_LANE + GDN_HEADS)
    gb_o[...] = jnp.where(is_g, gcum, jnp.where(is_b, _sigmoid(raw), 0.0))


def _gdn_prep(p, psm, wts, seq):
    m = p.shape[0]
    tm = TM_PREP
    tile = lambda cb: pl.BlockSpec((tm, MIX), lambda i, cb=cb: (i, cb))
    halo = lambda cb: pl.BlockSpec((16, MIX), lambda i, cb=cb: (jnp.maximum(i * (tm // 16) - 1, 0), cb))
    cw = pl.BlockSpec((4, MIX), lambda i: (0, 0))
    lv = pl.BlockSpec((1, LANES), lambda i: (0, 0))
    out = pl.BlockSpec((tm, MIX), lambda i: (i, 0))
    return pl.pallas_call(
        functools.partial(_gdn_prep_kernel, seq=seq, tm=tm),
        grid=(m // tm,),
        in_specs=[tile(4), tile(5), tile(6), halo(4), halo(5), halo(6),
                  pl.BlockSpec((tm, N_SMALL), lambda i: (i, 0)), cw, cw, cw, lv, lv],
        out_specs=[out, out, out, pl.BlockSpec((tm, LANES), lambda i: (i, 0))],
        out_shape=[jax.ShapeDtypeStruct((m, MIX), BF16)] * 3 + [jax.ShapeDtypeStruct((m, LANES), F32)],
        compiler_params=_params("parallel"),
        name="gdn_prep",
    )(p, p, p, p, p, p, psm, wts["cw_q"], wts["cw_k"], wts["cw_v"], wts["alog"], wts["dtb"])


def _gdn_core_kernel(q_ref, k_ref, v_ref, gb_ref, z_ref, nw_ref, o_ref, s_ref):
    @pl.when(pl.program_id(1) == 0)
    def _():
        s_ref[...] = jnp.zeros_like(s_ref)

    nb, rb, width = q_ref.shape
    nh = width // LANES
    units = [(b, hh) for hh in range(nh) for b in range(nb)]
    assert rb == 2 * CHUNK
    m = _pair_masks()
    ones = jnp.ones((LANES, LANES), BF16)
    cat0 = lambda *xs: jnp.concatenate(xs, axis=0)
    cat1 = lambda *xs: jnp.concatenate(xs, axis=1)

    def lane_bcast(x, lane):
        return jnp.broadcast_to(pltpu.roll(x, LANES - lane, 1)[:, 0:1], x.shape)

    nch = rb // CHUNK
    q_t, y0, phi, psi = {}, {}, {}, {}

    gf, ks, lhs, kbd, rhs, e_g, qs, gamma = [], [], [], [], [], [], [], []
    zero = jnp.zeros((CHUNK, LANES), F32)
    for b, hh in units:
        hl = slice(hh * LANES, (hh + 1) * LANES)
        head = nh * pl.program_id(0) + hh
        gbv = gb_ref[b]
        g = lane_bcast(gbv, GDN_G_LANE + head)
        beta = lane_bcast(gbv, GDN_B_LANE + head)
        q, k, v = (r[b, :, hl].astype(F32) for r in (q_ref, k_ref, v_ref))
        g_col = g[:CHUNK] * m["half0"] + g[CHUNK:] * m["half1"]
        diff = g_col - g.T[:CHUNK]
        gamma.append(jnp.where(m["incl"], jnp.exp(jnp.where(m["incl"], diff, 0.0)), 0.0))
        eg = jnp.exp(g)
        kb = k * beta
        gf.append(g)
        ks.append(k)
        qs.append(q)
        e_g.append(eg)
        lhs.append(cat0(cat1(kb[:CHUNK], kb[CHUNK:]), cat1(q[:CHUNK], q[CHUNK:])))
        kbd.append(cat0(cat1(k[:CHUNK], zero), cat1(zero, k[CHUNK:])))
        rhs.append(cat1(v * beta, kb * eg))
    kq = [_dot_nt(x, y) for x, y in zip(lhs, kbd)]
    a_neg = [-jnp.where(m["strict"], x[:CHUNK] * gm, 0.0) for x, gm in zip(kq, gamma)]
    a_qk = [x[CHUNK:] * gm for x, gm in zip(kq, gamma)]
    k_bar_t = [[(k[c * CHUNK:(c + 1) * CHUNK]
                 * jnp.exp(g[(c + 1) * CHUNK - 1:(c + 1) * CHUNK] - g[c * CHUNK:(c + 1) * CHUNK])).T
                for c in range(nch)] for k, g in zip(ks, gf)]
    t_inv = _tri_inv(a_neg, m)
    uw = [_dot(_stack(t, m), x) for t, x in zip(t_inv, rhs)]
    x3 = [_dot(_stack(a, m), x) for a, x in zip(a_qk, uw)]
    for i, u in enumerate(units):
        for c in range(nch):
            glast = gf[i][(c + 1) * CHUNK - 1:(c + 1) * CHUNK]
            pp = _dot(k_bar_t[i][c], uw[i][c * CHUNK:(c + 1) * CHUNK])
            phi[(u, c)] = m["eye128"] * jnp.exp(glast) - pp[:, LANES:]
            psi[(u, c)] = pp[:, :LANES]
    for i, u in enumerate(units):
        y0[u] = x3[i][:, :LANES]
        q_t[u] = qs[i] * e_g[i] - x3[i][:, LANES:]

    ys = {u: [] for u in units}
    st = {(b, hh): s_ref[hh, b] for b, hh in units}
    for c in range(nch):
        cs = slice(c * CHUNK, (c + 1) * CHUNK)
        for u in units:
            qs_ = _dot(cat0(q_t[u][cs], phi[(u, c)]), st[u])
            ys[u].append(y0[u][cs] + qs_[:CHUNK])
            st[u] = qs_[CHUNK:] + psi[(u, c)]
    for b, hh in units:
        s_ref[hh, b] = st[(b, hh)]

    for hh in range(nh):
        hl = slice(hh * LANES, (hh + 1) * LANES)
        y = cat0(*[yc for b in range(nb) for yc in ys[(b, hh)]])
        ms = _dot(y * y, ones) * (1.0 / GDN_HEAD)
        out = y * lax.rsqrt(ms + GDN_EPS) * nw_ref[...]
        z = z_ref[:, :, hl].astype(F32).reshape(nb * rb, LANES)
        o_ref[:, :, hl] = (out * _silu(z)).astype(BF16).reshape(nb, rb, LANES)


def _gdn_core(q, k, v, gb, p, norm_w, batch, seq):
    rb, nh = RB_GDN, GDN_HEADS_PER_STEP
    width = nh * LANES
    tile = pl.BlockSpec((batch, rb, width), lambda h, j: (0, j, h))
    z_col0 = 7 * (MIX // width)
    r3 = lambda a: a.reshape(batch, seq, a.shape[-1])
    out = pl.pallas_call(
        _gdn_core_kernel,
        grid=(GDN_HEADS // nh, seq // rb),
        in_specs=[tile, tile, tile,
                  pl.BlockSpec((batch, rb, LANES), lambda h, j: (0, j, 0)),
                  pl.BlockSpec((batch, rb, width), lambda h, j: (0, j, z_col0 + h)),
                  pl.BlockSpec((1, LANES), lambda h, j: (0, 0))],
        out_specs=tile,
        out_shape=jax.ShapeDtypeStruct((batch, seq, MIX), BF16),
        scratch_shapes=[pltpu.VMEM((nh, batch, LANES, LANES), F32)],
        compiler_params=_params("parallel", "arbitrary"),
        name="gdn_core",
    )(r3(q), r3(k), r3(v), r3(gb), r3(p), norm_w.reshape(1, LANES))
    return out.reshape(batch * seq, MIX)


def _merge_kernel(ya, yb, yc, wa, wb, wc, ga, gb, gc, o_ref):
    dot = lambda y, w: jnp.dot(y[...], w[0], preferred_element_type=F32)
    acc = ga[...].astype(F32) * dot(ya, wa)
    acc = acc + gb[...].astype(F32) * dot(yb, wb)
    acc = acc + gc[...].astype(F32) * dot(yc, wc)
    o_ref[...] = acc.astype(BF16)


def _merge(ya, yb, yc, wa, wb, wc, l, gates):
    m = ya.shape[0]
    d = wa.shape[2]
    tm = TM_LN
    y = pl.BlockSpec((tm, MIX), lambda i: (i, 0))
    w = pl.BlockSpec((1, MIX, d), lambda i: (l, 0, 0), pipeline_mode=pl.Buffered(1))
    g = lambda br: pl.BlockSpec((tm, d), lambda i, br=br: (i, br))
    return pl.pallas_call(
        _merge_kernel,
        grid=(m // tm,),
        in_specs=[y, y, y, w, w, w, g(0), g(1), g(2)],
        out_specs=pl.BlockSpec((tm, d), lambda i: (i, 0)),
        out_shape=jax.ShapeDtypeStruct((m, d), BF16),
        compiler_params=_params("parallel"),
        name="merge",
    )(ya, yb, yc, wa, wb, wc, gates, gates, gates)


def _proj_ln_kernel(*refs, emit_h):
    a_ref, w_ref, x_ref, gt_ref, lnw_ref, lnb_ref = refs[:6]
    y = jnp.dot(a_ref[...], w_ref[0], preferred_element_type=F32)
    xn = _resid_ln(x_ref[...], y, gt_ref[0], lnw_ref[...], lnb_ref[...])
    if emit_h:
        sc_ref, sh_ref, xo_ref, ho_ref = refs[6:]
        ho_ref[...] = (xn * (1.0 + sc_ref[0]) + sh_ref[0]).astype(BF16)
    else:
        (xo_ref,) = refs[6:]
    xo_ref[...] = xn


def _proj_ln(a, w, l, x, gt, lnw, lnb, nxt, seq, *, tm, name):
    m, k = a.shape
    d = w.shape[2]
    per = seq // tm
    emit_h = nxt is not None
    row = pl.BlockSpec((tm, d), lambda i: (i, 0))
    mod = pl.BlockSpec((1, 1, d), lambda i: (i // per, 0, 0))
    vec = pl.BlockSpec((1, d), lambda i: (0, 0))
    in_specs = [pl.BlockSpec((tm, k), lambda i: (i, 0)),
                pl.BlockSpec((1, k, d), lambda i: (l, 0, 0), pipeline_mode=pl.Buffered(1)),
                row, mod, vec, vec]
    args = [a, w, x, gt, lnw.reshape(1, d), lnb.reshape(1, d)]
    out_specs, out_shape = [row], [jax.ShapeDtypeStruct((m, d), F32)]
    if emit_h:
        in_specs += [mod, mod]
        args += list(nxt)
        out_specs.append(row)
        out_shape.append(jax.ShapeDtypeStruct((m, d), BF16))
    res = pl.pallas_call(
        functools.partial(_proj_ln_kernel, emit_h=emit_h),
        grid=(m // tm,),
        in_specs=in_specs,
        out_specs=out_specs,
        out_shape=out_shape,
        compiler_params=_params("parallel"),
        name=name,
    )(*args)
    return (res[0], res[1]) if emit_h else (res[0], None)


def _swiglu_kernel(a_ref, wg_ref, wu_ref, o_ref, wgb_ref, wub_ref):
    @pl.when(pl.program_id(1) == 0)
    def _():
        wgb_ref[...] = wg_ref[0].astype(BF16)
        wub_ref[...] = wu_ref[0].astype(BF16)

    a = a_ref[...]
    g = jnp.dot(a, wgb_ref[...], preferred_element_type=F32)
    u = jnp.dot(a, wub_ref[...], preferred_element_type=F32)
    o_ref[...] = (_silu(g) * u).astype(BF16)


def _swiglu(h, w_up, l):
    m, k = h.shape
    dff = w_up.shape[2] // 2
    tm, tn = TM_MM, 512
    nj = dff // tn
    return pl.pallas_call(
        _swiglu_kernel,
        grid=(nj, m // tm),
        in_specs=[pl.BlockSpec((tm, k), lambda j, i: (i, 0)),
                  pl.BlockSpec((1, k, tn), lambda j, i: (l, 0, j)),
                  pl.BlockSpec((1, k, tn), lambda j, i: (l, 0, nj + j))],
        out_specs=pl.BlockSpec((tm, tn), lambda j, i: (i, j)),
        out_shape=jax.ShapeDtypeStruct((m, dff), BF16),
        scratch_shapes=[pltpu.VMEM((k, tn), BF16)] * 2,
        compiler_params=_params("parallel", "arbitrary"),
        name="ffn_up_swiglu",
    )(h, w_up, w_up)


def _layer_weights(l, w_in_t, rwkv_mu, rwkv_w0, rwkv_w2, rwkv_a0, rwkv_a2, rwkv_g2, rwkv_k_k, rwkv_k_a,
                   rwkv_r_k, rwkv_ln_w, rwkv_ln_b, rwkv_v0, rwkv_v1, rwkv_v2, gdn_conv_w, gdn_a_log,
                   gdn_dt_bias):
    c = MIX
    rows = lambda lo, hi: w_in_t[l, lo:hi]
    o_rwkv = c
    o_lora = o_rwkv + 3 * c
    n_lora = rwkv_w2.shape[1] + rwkv_a2.shape[1] + rwkv_g2.shape[1]
    o_gdn = o_lora + n_lora
    o_ab = o_gdn + 4 * c
    o_gate = o_ab + 2 * GDN_HEADS
    pad_rows = N_SMALL - n_lora - 2 * GDN_HEADS
    w_main = (o_lora, rows(o_gdn, o_ab))
    assert n_lora - (N_SMALL - LANES) == GDN_G_LANE and GDN_B_LANE == GDN_G_LANE + GDN_HEADS
    w_small = jnp.concatenate([rows(o_lora, o_gdn), rows(o_ab, o_gate),
                               jnp.zeros((pad_rows, w_in_t.shape[2]), BF16)], axis=0)
    w_gate = rows(o_gate, w_in_t.shape[1])

    mu = rwkv_mu[l]
    row = lambda v: v.reshape(1, -1)
    n_w, n_a, n_g = rwkv_w2.shape[1], rwkv_a2.shape[1], rwkv_g2.shape[1]
    padrows = lambda w, lo: jnp.zeros((N_SMALL, c), F32).at[lo:lo + w.shape[0]].set(w).astype(BF16)
    lane_vec = lambda v, lo: jnp.zeros((1, LANES), F32).at[0, lo:lo + v.shape[0]].set(v)
    rw = {
        "mu_r": row(mu[:c]), "mu_k": row(mu[c:2 * c]), "mu_v": row(mu[2 * c:3 * c]),
        "mu_s": jnp.zeros((1, N_SMALL), F32).at[0, :n_lora].set(mu[3 * c:]),
        "w0": row(rwkv_w0[l]), "a0": row(rwkv_a0[l]), "k_k": row(rwkv_k_k[l]), "k_a": row(rwkv_k_a[l]),
        "w2p": padrows(rwkv_w2[l], 0), "a2p": padrows(rwkv_a2[l], n_w), "g2p": padrows(rwkv_g2[l], n_w + n_a),
        "r_k": row(rwkv_r_k[l]), "ln_w": row(rwkv_ln_w[l]), "ln_b": row(rwkv_ln_b[l]),
    }
    if l > 0:
        nv = rwkv_v1.shape[2]
        rw["v0"] = row(rwkv_v0[l - 1])
        rw["v1p"] = jnp.zeros((c, LANES), F32).at[:, :nv].set(rwkv_v1[l - 1]).astype(BF16)
        rw["v2p"] = jnp.zeros((LANES, c), F32).at[:nv].set(rwkv_v2[l - 1]).astype(BF16)
    cw = gdn_conv_w[l]
    gd = {"cw_q": cw[:, :c], "cw_k": cw[:, c:2 * c], "cw_v": cw[:, 2 * c:],
          "alog": lane_vec(gdn_a_log[l], GDN_G_LANE), "dtb": lane_vec(gdn_dt_bias[l], GDN_G_LANE)}
    return w_main, w_small, w_gate, rw, gd


def kernel(x, c, ada_w, ada_b, w_in, pool_w, pool_scale, rwkv_mu, rwkv_w0, rwkv_w2, rwkv_a0, rwkv_a2, rwkv_g2, rwkv_k_k, rwkv_k_a, rwkv_r_k, rwkv_ln_w, rwkv_ln_b, rwkv_v0, rwkv_v1, rwkv_v2, gdn_conv_w, gdn_a_log, gdn_dt_bias, gdn_norm_w, w_branch_a, w_branch_b, w_branch_c, w_out, ln1_w, ln1_b, ffn_w_up, ffn_w_down, ln2_w, ln2_b):
    batch, seq, d = x.shape
    assert seq % TM_MM == 0 and d == 2 * MIX
    m = batch * seq
    xf = x.reshape(m, d)

    mod = _ada(c, ada_w, ada_b)
    mods = [[mod[l, :, i * d:(i + 1) * d].reshape(batch, 1, d) for i in range(6)]
            for l in range(DEPTH)]

    w_in_t = jnp.swapaxes(w_in, 1, 2).astype(BF16)
    wa, wb, wc = (w.astype(BF16) for w in (w_branch_a, w_branch_b, w_branch_c))
    w_out_b, w_down_b = (w.astype(BF16) for w in (w_out, ffn_w_down))
    v_first = None
    for l in range(DEPTH):
        sh_m, sc_m, gt_m, sh_f, sc_f, gt_f = mods[l]
        (n_main1, w_main2), w_small, w_gate, rw, gd = _layer_weights(
            l, w_in_t, rwkv_mu, rwkv_w0, rwkv_w2, rwkv_a0, rwkv_a2, rwkv_g2, rwkv_k_k, rwkv_k_a,
            rwkv_r_k, rwkv_ln_w, rwkv_ln_b, rwkv_v0, rwkv_v1, rwkv_v2, gdn_conv_w, gdn_a_log,
            gdn_dt_bias)

        if l == 0:
            h, psm = _modulate_small(xf, sc_m, sh_m, w_small, seq)
        else:
            psm = _matmul_nt(h, w_small, tn=N_SMALL, out_dtype=F32, name="in_proj_small")
        p = _matmul_nt2(h, w_in_t, l, n_main1, w_main2, tn=2048, out_dtype=BF16, name="in_proj_main")
        gates = _matmul_nt(h, w_gate, tn=2048, out_dtype=BF16, act="sigmoid", name="in_proj_gates")

        y_a = _pool(p, pool_w[l], pool_scale[l], seq)
        prep = _rwkv_prep(p, psm, rw, v_first, seq)
        if l == 0:
            v_first = prep[2]
        y_b = _rwkv_core(prep, rw, batch, seq)
        gq, gk, gv, ggb = _gdn_prep(p, psm, gd, seq)
        y_c = _gdn_core(gq, gk, gv, ggb, p, gdn_norm_w[l], batch, seq)

        merged = _merge(y_a, y_b, y_c, wa, wb, wc, l, gates)
        xf, h = _proj_ln(merged, w_out_b, l, xf, gt_m, ln1_w[l], ln1_b[l],
                         (sc_f, sh_f), seq, tm=512, name="out_proj_ln")

        act = _swiglu(h, ffn_w_up, l)
        nxt = (mods[l + 1][1], mods[l + 1][0]) if l + 1 < DEPTH else None
        xf, h = _proj_ln(act, w_down_b, l, xf, gt_f, ln2_w[l], ln2_b[l],
                         nxt, seq, tm=256, name="ffn_down_ln")
    return xf.reshape(batch, seq, d)
```

```python
import functools
import math

import jax
import jax.numpy as jnp
from jax import lax
from jax.experimental import pallas as pl
from jax.experimental.pallas import tpu as pltpu

F32 = jnp.float32
BF16 = jnp.bfloat16

DEPTH = 2
DN_ALPHA = (2 * DEPTH) ** 0.25
LN_EPS = 1e-5
MIX = 1024
POOL_WINDOWS = (2, 4, 8, 16)
POOL_GROUP = 256
RWKV_HEAD = 64
RWKV_LN_EPS = RWKV_HEAD * 1e-5
GDN_HEAD = 128
GDN_HEADS = 8
GDN_EPS = 1e-6
CHUNK = 64
LANES = 128
N_SMALL = 384
GDN_G_LANE = 32
GDN_B_LANE = 40
VMEM_LIMIT = 56 * 1024 * 1024

TM_MM = 1024
TM_LN = 512
TM_PREP = 256
RB_RWKV = 512
RB_GDN = 128
GDN_HEADS_PER_STEP = 8


def _dot(a, b):
    return jnp.dot(a.astype(BF16), b.astype(BF16), preferred_element_type=F32)


def _dot_nt(a, b):
    return lax.dot_general(a.astype(BF16), b.astype(BF16), (((1,), (1,)), ((), ())),
                           preferred_element_type=F32)


def _split(a, n):
    terms, rest = [], a
    for _ in range(n):
        t = rest.astype(BF16)
        terms.append(t)
        rest = rest - t.astype(F32)
    return terms


def _dot3(a, b, nt=False):
    d = _dot_nt if nt else _dot
    a1, a2 = _split(a, 2)
    b1, b2 = _split(b, 2)
    return d(a1, b1) + (d(a1, b2) + d(a2, b1))


def _dot_sel_l(sel, b, n):
    out = None
    for t in _split(b, n):
        p = jnp.dot(sel, t, preferred_element_type=F32)
        out = p if out is None else out + p
    return out


def _sigmoid(x):
    return 0.5 * jnp.tanh(0.5 * x) + 0.5


def _silu(x):
    return x * _sigmoid(x)


def _softplus(x):
    return jnp.maximum(x, 0.0) + jnp.log(1.0 + jnp.exp(-jnp.abs(x)))


def _iota2(shape, dim):
    return lax.broadcasted_iota(jnp.int32, shape, dim)


def _pair_masks():
    r = _iota2((CHUNK, LANES), 0)
    c = _iota2((CHUNK, LANES), 1) & (CHUNK - 1)
    same = lambda s: (r >> s) == (c >> s)
    r2 = _iota2((LANES, LANES), 0)
    c2 = _iota2((LANES, LANES), 1)
    lane = _iota2((1, LANES), 1)
    return {
        "eye": (r == c).astype(F32),
        "bd8": same(3),
        "off16": same(4) & jnp.logical_not(same(3)),
        "off32": same(5) & jnp.logical_not(same(4)),
        "off64": jnp.logical_not(same(5)),
        "strict": c < r,
        "incl": c <= r,
        "eye128": (r2 == c2).astype(F32),
        "bd64": (r2 >> 6) == (c2 >> 6),
        "low": _iota2((CHUNK, LANES), 1) < CHUNK,
        "half0": (lane < CHUNK).astype(F32),
        "half1": (lane >= CHUNK).astype(F32),
    }


def _stack(x, m):
    xb = x.astype(BF16)
    zero = jnp.zeros_like(xb)
    return jnp.concatenate([jnp.where(m["low"], xb, zero), jnp.where(m["low"], zero, xb)], axis=0)


def _chunk_cumsum_matrix(n):
    r = _iota2((n, n), 0)
    c = _iota2((n, n), 1)
    return (((r >> 6) == (c >> 6)) & (c <= r)).astype(BF16)


def _tri_inv(mats, m):
    bd = lambda x: _stack(x, m)
    a8 = [jnp.where(m["bd8"], a, 0.0) for a in mats]
    s = [m["eye"] + x for x in a8]
    p = [_dot(x, bd(x)) for x in a8]
    q = [_dot(pi, jnp.concatenate([bd(si), bd(pi)], axis=1)) for pi, si in zip(p, s)]
    s = [si + qi[:, :LANES] for si, qi in zip(s, q)]
    s = [si + _dot(qi[:, LANES:], bd(si)) for si, qi in zip(s, q)]
    for key in ("off16", "off32", "off64"):
        t = [_dot(jnp.where(m[key], a, 0.0), bd(si)) for a, si in zip(mats, s)]
        s = [si + _dot(si, bd(ti)) for si, ti in zip(s, t)]
    return s


def _shift_rows(x, halo, s):
    halo = halo[halo.shape[0] - 8:]
    xr = pltpu.roll(x, s, 0)
    hr = pltpu.roll(halo, s, 0)
    rows = _iota2(halo.shape, 0)
    top = jnp.where(rows < s, hr, xr[:8])
    return top if x.shape[0] == 8 else jnp.concatenate([top, xr[8:]], axis=0)


def _shift_rows_bf16(x, halo, shifts):
    tm = x.shape[0]
    r = _iota2((tm, tm), 0)
    c = _iota2((tm, tm), 1)
    sel = jnp.concatenate([(r - c == s).astype(BF16) for s in shifts], axis=0)
    moved = jnp.dot(sel, x, preferred_element_type=F32)
    x_top = x[:16].astype(F32)[:8]
    outs = []
    for j, s in enumerate(shifts):
        top = _shift_rows(x_top, halo, s)
        outs.append(jnp.concatenate([top, moved[j * tm + 8:(j + 1) * tm]], axis=0))
    return outs


def _resid_ln(x, y, gt, lnw, lnb):
    z = DN_ALPHA * x + (1.0 + gt) * y
    mu = jnp.mean(z, axis=-1, keepdims=True)
    zc = z - mu
    var = jnp.mean(zc * zc, axis=-1, keepdims=True)
    return zc * lax.rsqrt(var + LN_EPS) * lnw + lnb


def _params(*sem):
    return pltpu.CompilerParams(dimension_semantics=sem, vmem_limit_bytes=VMEM_LIMIT)


def _ada_kernel(c_ref, w_ref, b_ref, o_ref):
    c = c_ref[...]
    o_ref[0] = _dot3(_silu(c), w_ref[0]) + b_ref[0]


def _ada(c, ada_w, ada_b):
    nl, d, n = ada_w.shape
    b = c.shape[0]
    tn = 512
    cp = jnp.zeros((8, d), F32).at[:b].set(c)
    out = pl.pallas_call(
        _ada_kernel,
        grid=(nl, n // tn),
        in_specs=[pl.BlockSpec((8, d), lambda l, j: (0, 0)),
                  pl.BlockSpec((1, d, tn), lambda l, j: (l, 0, j)),
                  pl.BlockSpec((1, 1, tn), lambda l, j: (l, 0, j))],
        out_specs=pl.BlockSpec((1, 8, tn), lambda l, j: (l, 0, j)),
        out_shape=jax.ShapeDtypeStruct((nl, 8, n), F32),
        compiler_params=_params("parallel", "parallel"),
        name="ada_mod",
    )(cp, ada_w, ada_b.reshape(nl, 1, n))
    return out[:, :b]


def _modulate_small_kernel(x_ref, sc_ref, sh_ref, wt_ref, h_ref, ps_ref):
    h = (x_ref[...] * (1.0 + sc_ref[0]) + sh_ref[0]).astype(BF16)
    h_ref[...] = h
    ps_ref[...] = lax.dot_general(h, wt_ref[...], (((1,), (1,)), ((), ())), preferred_element_type=F32)


def _modulate_small(x, sc, sh, w_small_t, seq):
    m, d = x.shape
    n = w_small_t.shape[0]
    tm = TM_LN
    per = seq // tm
    mod = pl.BlockSpec((1, 1, d), lambda i: (i // per, 0, 0))
    return pl.pallas_call(
        _modulate_small_kernel,
        grid=(m // tm,),
        in_specs=[pl.BlockSpec((tm, d), lambda i: (i, 0)), mod, mod,
                  pl.BlockSpec((n, d), lambda i: (0, 0))],
        out_specs=[pl.BlockSpec((tm, d), lambda i: (i, 0)), pl.BlockSpec((tm, n), lambda i: (i, 0))],
        out_shape=[jax.ShapeDtypeStruct((m, d), BF16), jax.ShapeDtypeStruct((m, n), F32)],
        compiler_params=_params("parallel"),
        name="modulate_small",
    )(x, sc, sh, w_small_t)


def _mm_kernel(a_ref, bt_ref, o_ref, *, act):
    acc = lax.dot_general(a_ref[...], bt_ref[...], (((1,), (1,)), ((), ())),
                          preferred_element_type=F32)
    if act == "sigmoid":
        acc = _sigmoid(acc)
    o_ref[...] = acc.astype(o_ref.dtype)


def _matmul_nt(a, bt, *, tn, out_dtype, act=None, name):
    m, k = a.shape
    n = bt.shape[0]
    tm = TM_MM
    return pl.pallas_call(
        functools.partial(_mm_kernel, act=act),
        grid=(m // tm, n // tn),
        in_specs=[pl.BlockSpec((tm, k), lambda i, j: (i, 0)),
                  pl.BlockSpec((tn, k), lambda i, j: (j, 0))],
        out_specs=pl.BlockSpec((tm, tn), lambda i, j: (i, j)),
        out_shape=jax.ShapeDtypeStruct((m, n), out_dtype),
        compiler_params=_params("parallel", "parallel"),
        name=name,
    )(a, bt)


def _mm2_kernel(a_ref, b1_ref, b2_ref, o_ref, *, nj1):
    dims = (((1,), (1,)), ((), ()))

    @pl.when(pl.program_id(1) < nj1)
    def _():
        o_ref[...] = lax.dot_general(a_ref[...], b1_ref[0], dims,
                                     preferred_element_type=F32).astype(o_ref.dtype)

    @pl.when(pl.program_id(1) >= nj1)
    def _():
        o_ref[...] = lax.dot_general(a_ref[...], b2_ref[...], dims,
                                     preferred_element_type=F32).astype(o_ref.dtype)


def _matmul_nt2(a, w_all, l, n1, b2, *, tn, out_dtype, name):
    m, k = a.shape
    n2 = b2.shape[0]
    tm = TM_MM
    nj1 = n1 // tn
    return pl.pallas_call(
        functools.partial(_mm2_kernel, nj1=nj1),
        grid=(m // tm, (n1 + n2) // tn),
        in_specs=[pl.BlockSpec((tm, k), lambda i, j: (i, 0)),
                  pl.BlockSpec((1, tn, k), lambda i, j: (l, jnp.minimum(j, nj1 - 1), 0)),
                  pl.BlockSpec((tn, k), lambda i, j: (jnp.maximum(j - nj1, 0), 0))],
        out_specs=pl.BlockSpec((tm, tn), lambda i, j: (i, j)),
        out_shape=jax.ShapeDtypeStruct((m, n1 + n2), out_dtype),
        compiler_params=_params("parallel", "parallel"),
        name=name,
    )(a, w_all, b2)


def _pool_kernel(p_ref, halo_ref, w_ref, scale_ref, o_ref, *, seq, tm):
    t0 = (pl.program_id(0) * tm) % seq
    x = p_ref[...].astype(F32)
    halo = jnp.where(t0 == 0, 0.0, halo_ref[...].astype(F32))
    ext = jnp.concatenate([halo, x], axis=0)
    sums = [ext]
    for sh in (1, 2, 4, 8):
        sums.append(sums[-1] + pltpu.roll(sums[-1], sh, 0))
    t = t0 + _iota2((tm, POOL_GROUP), 0)
    ys = []
    for g, win in enumerate(POOL_WINDOWS):
        cols = slice(g * POOL_GROUP, (g + 1) * POOL_GROUP)
        cnt = jnp.minimum(t + 1, win).astype(F32)
        pooled = sums[g + 1][16:, cols] / cnt - x[:, cols]
        ys.append(_dot(pooled, w_ref[g]))
    o_ref[...] = (jnp.concatenate(ys, axis=1) * scale_ref[...]).astype(BF16)


def _pool_io(p, pool_w, pool_scale):
    m = p.shape[0]
    tm = TM_PREP
    in_specs = [pl.BlockSpec((tm, MIX), lambda i: (i, 0)),
                pl.BlockSpec((16, MIX), lambda i: (jnp.maximum(i * (tm // 16) - 1, 0), 0)),
                pl.BlockSpec(pool_w.shape, lambda i: (0, 0, 0)),
                pl.BlockSpec((1, MIX), lambda i: (0, 0))]
    args = [p, p, pool_w.astype(BF16), pool_scale.reshape(1, MIX)]
    return (in_specs, args, [pl.BlockSpec((tm, MIX), lambda i: (i, 0))],
            [jax.ShapeDtypeStruct((m, MIX), BF16)])


def _rwkv_prep_kernel(*refs, seq, tm, has_vres):
    (pr, pk, pv, hr, hk, hv, ps, hs, mur, muk, muv, mus, w0, a0, kkp, kap,
     w2p, a2p, g2p) = refs[:19]
    if has_vres:
        vf, v0, v1p, v2p = refs[19:23]
        outs = refs[23:]
    else:
        outs = refs[19:]
    r_o, k_o, v_o, kk_o, a_o, ld_o, g_o = outs

    first = (pl.program_id(0) * tm) % seq == 0

    def lerp(p_ref, h_ref, mu_ref):
        halo = jnp.where(first, 0.0, h_ref[...].astype(F32))
        if p_ref.dtype == BF16:
            (prev,) = _shift_rows_bf16(p_ref[...], halo, (1,))
        else:
            prev = _shift_rows(p_ref[...], halo, 1)
        p = p_ref[...].astype(F32)
        return p + (prev - p) * mu_ref[...]

    xr = lerp(pr, hr, mur)
    xk = lerp(pk, hk, muk)
    xv = lerp(pv, hv, muv)
    xs = lerp(ps, hs, mus)

    z = w0[...] + _dot(jnp.tanh(xs), w2p[...])
    t = jnp.exp(-jnp.abs(z))
    log_decay = jnp.where(z < 0.0, t, 1.0) * (-math.exp(-0.5) / (1.0 + t))
    a = _sigmoid(a0[...] + _dot(xs, a2p[...]))
    g = _dot(_sigmoid(xs), g2p[...])
    if has_vres:
        gate = _sigmoid(v0[...] + _dot(_dot(xv, v1p[...]), v2p[...]))
        xv = xv + (vf[...].astype(F32) - xv) * gate

    kk = xk * kkp[...]
    r2 = _iota2((LANES, LANES), 0)
    c2 = _iota2((LANES, LANES), 1)
    head_ones = ((r2 >> 6) == (c2 >> 6)).astype(BF16)
    kk2 = kk * kk
    ss = jnp.concatenate(
        [_dot(kk2[:, i * LANES:(i + 1) * LANES], head_ones) for i in range(MIX // LANES)],
        axis=1)
    kk = kk * lax.rsqrt(jnp.maximum(ss, 1e-24))

    r_o[...] = xr.astype(r_o.dtype)
    k_o[...] = (xk * (1.0 + (a - 1.0) * kap[...])).astype(k_o.dtype)
    v_o[...] = xv.astype(v_o.dtype)
    kk_o[...] = kk.astype(kk_o.dtype)
    a_o[...] = a.astype(a_o.dtype)
    ld_o[...] = _dot_sel_l(_chunk_cumsum_matrix(tm), log_decay, 2)
    g_o[...] = g.astype(g_o.dtype)


def _rwkv_prep_io(p, psm, wts, v_first):
    m = p.shape[0]
    tm = TM_PREP
    has_vres = v_first is not None
    tile = lambda cb: pl.BlockSpec((tm, MIX), lambda i, cb=cb: (i, cb))
    halo = lambda cb: pl.BlockSpec((16, MIX), lambda i, cb=cb: (jnp.maximum(i * (tm // 16) - 1, 0), cb))
    vec = pl.BlockSpec((1, MIX), lambda i: (0, 0))
    full = lambda arr: pl.BlockSpec(arr.shape, lambda i: (0,) * arr.ndim)
    in_specs = [tile(1), tile(2), tile(3), halo(1), halo(2), halo(3),
                pl.BlockSpec((tm, N_SMALL), lambda i: (i, 0)),
                pl.BlockSpec((8, N_SMALL), lambda i: (jnp.maximum(i * (tm // 8) - 1, 0), 0)),
                vec, vec, vec, pl.BlockSpec((1, N_SMALL), lambda i: (0, 0)),
                vec, vec, vec, vec,
                full(wts["w2p"]), full(wts["a2p"]), full(wts["g2p"])]
    args = [p, p, p, p, p, p, psm, psm, wts["mu_r"], wts["mu_k"], wts["mu_v"], wts["mu_s"],
            wts["w0"], wts["a0"], wts["k_k"], wts["k_a"], wts["w2p"], wts["a2p"], wts["g2p"]]
    if has_vres:
        in_specs += [pl.BlockSpec((tm, MIX), lambda i: (i, 0)), vec, full(wts["v1p"]), full(wts["v2p"])]
        args += [v_first, wts["v0"], wts["v1p"], wts["v2p"]]
    out = pl.BlockSpec((tm, MIX), lambda i: (i, 0))
    out_shape = [jax.ShapeDtypeStruct((m, MIX), dt) for dt in (BF16,) * 5 + (F32, BF16)]
    return in_specs, args, [out] * 7, out_shape


def _rwkv_core_kernel(r_ref, k_ref, v_ref, kk_ref, a_ref, gc_ref, g_ref, rk_ref, lnw_ref, lnb_ref,
                      o_ref, h_ref):
    @pl.when(pl.program_id(1) == 0)
    def _():
        h_ref[...] = jnp.zeros_like(h_ref)

    nb, rb, _ = r_ref.shape
    nch = rb // CHUNK
    units = [(b, c) for c in range(nch) for b in range(nb)]
    m = _pair_masks()
    head_ones = m["bd64"].astype(BF16)
    stack = lambda x: _stack(x, m)
    cat0 = lambda *xs: jnp.concatenate(xs, axis=0)
    cat1 = lambda *xs: jnp.concatenate(xs, axis=1)

    def load(ref, u):
        b, c = u
        return ref[b, c * CHUNK:(c + 1) * CHUNK, :].astype(F32)

    first_row = _iota2((CHUNK, LANES), 0) == 0
    q_t, y0, phi, psi = {}, {}, {}, {}

    a_hat, r_hat, lhs, rhs, bbar_t, kbar_t, dlast, vs = [], [], [], [], [], [], [], []
    for u in units:
        gc = load(gc_ref, u)
        g_prev = jnp.where(first_row, 0.0, pltpu.roll(gc, 1, 0))
        kc, vc, kkc = load(k_ref, u), load(v_ref, u), load(kk_ref, u)
        bc = kkc * load(a_ref, u)
        glast = gc[CHUNK - 1:CHUNK]
        e_neg = jnp.exp(-gc)
        e_rest = jnp.exp(glast - gc)
        a_hat.append(-kkc * jnp.exp(g_prev))
        r_hat.append(load(r_ref, u) * jnp.exp(gc))
        lhs.append(cat0(a_hat[-1], r_hat[-1]))
        rhs.append(cat0(stack(bc * e_neg), stack(kc * e_neg)))
        bbar_t.append((bc * e_rest).T)
        kbar_t.append((kc * e_rest).T)
        dlast.append(jnp.exp(glast))
        vs.append(vc)
    big = [_dot_nt(x, y) for x, y in zip(lhs, rhs)]
    a_ab = [jnp.where(m["strict"], x[:CHUNK, :LANES], 0.0) for x in big]
    a_kr = [cat0(jnp.where(m["strict"], x[:CHUNK, LANES:], 0.0),
                 jnp.where(m["incl"], x[CHUNK:, LANES:], 0.0)) for x in big]
    a_rb = [jnp.where(m["incl"], x[CHUNK:, :LANES], 0.0) for x in big]
    x1 = [_dot(a, stack(v)) for a, v in zip(a_kr, vs)]
    t_inv = _tri_inv(a_ab, m)
    x2 = [_dot(t, cat1(stack(a), stack(x[:CHUNK]))) for t, a, x in zip(t_inv, a_hat, x1)]
    x3 = [_dot(a, cat1(stack(x[:, :LANES]), stack(x[:, LANES:]))) for a, x in zip(a_rb, x2)]
    for i, u in enumerate(units):
        w_t, u_t = x2[i][:, :LANES], x2[i][:, LANES:]
        phi[u] = m["eye128"] * dlast[i] + jnp.where(m["bd64"], _dot(bbar_t[i], w_t), 0.0)
        psi[u] = jnp.where(m["bd64"], _dot(cat1(bbar_t[i], kbar_t[i]), cat0(u_t, vs[i])), 0.0)
    for i, u in enumerate(units):
        q_t[u] = r_hat[i] + x3[i][:, :LANES]
        y0[u] = x3[i][:, LANES:] + x1[i][CHUNK:]

    ys = {}
    hs = [h_ref[b] for b in range(nb)]
    for u in units:
        b = u[0]
        qh = _dot(cat0(q_t[u], phi[u]), hs[b])
        ys[u] = y0[u] + qh[:CHUNK]
        hs[b] = qh[CHUNK:] + psi[u]
    for b in range(nb):
        h_ref[b] = hs[b]

    y = cat0(*[ys[(b, c)] for b in range(nb) for c in range(nch)])
    flat = lambda ref: ref[...].astype(F32).reshape(nb * rb, LANES)
    rr, kk_, vv = flat(r_ref), flat(k_ref), flat(v_ref)
    sums = _dot(cat0(y, rr * kk_ * rk_ref[...]), head_ones)
    yc = y - sums[:nb * rb] * (1.0 / RWKV_HEAD)
    var = _dot(yc * yc, head_ones) * (1.0 / RWKV_HEAD)
    yn = yc * lax.rsqrt(var + RWKV_LN_EPS) * lnw_ref[...] + lnb_ref[...]
    bonus = sums[nb * rb:] * vv
    o_ref[...] = ((yn + bonus) * flat(g_ref)).astype(BF16).reshape(nb, rb, LANES)


def _rwkv_core(prep, wts, batch, seq):
    rb = RB_RWKV
    tile = pl.BlockSpec((batch, rb, LANES), lambda p, j: (0, j, p))
    vec = pl.BlockSpec((1, LANES), lambda p, j: (0, p))
    out = pl.pallas_call(
        _rwkv_core_kernel,
        grid=(MIX // LANES, seq // rb),
        in_specs=[tile] * 7 + [vec] * 3,
        out_specs=tile,
        out_shape=jax.ShapeDtypeStruct((batch, seq, MIX), BF16),
        scratch_shapes=[pltpu.VMEM((batch, LANES, LANES), F32)],
        compiler_params=_params("parallel", "arbitrary"),
        name="rwkv_core",
    )(*[a.reshape(batch, seq, MIX) for a in prep], wts["r_k"], wts["ln_w"], wts["ln_b"])
    return out.reshape(batch * seq, MIX)


def _gdn_prep_kernel(pq, pk, pv, hq, hk, hv, ps, cwq, cwk, cwv, alog, dtb,
                     q_o, k_o, v_o, gb_o, *, seq, tm):
    first = (pl.program_id(0) * tm) % seq == 0
    ones = jnp.ones((LANES, LANES), BF16)

    def conv_silu(p_ref, h_ref, w_ref):
        halo = jnp.where(first, 0.0, h_ref[...].astype(F32))
        w = w_ref[...]
        y = p_ref[...].astype(F32) * w[3:4]
        for s, xs in zip((1, 2, 3), _shift_rows_bf16(p_ref[...], halo, (1, 2, 3))):
            y = y + xs * w[3 - s:4 - s]
        return _silu(y)

    def l2norm(y, scale):
        blocks = []
        for i in range(GDN_HEADS):
            blk = y[:, i * LANES:(i + 1) * LANES]
            ss = _dot(blk * blk, ones)
            blocks.append(blk * (lax.rsqrt(ss + GDN_EPS) * scale))
        return jnp.concatenate(blocks, axis=1)

    q_o[...] = l2norm(conv_silu(pq, hq, cwq), GDN_HEAD ** -0.5).astype(q_o.dtype)
    k_o[...] = l2norm(conv_silu(pk, hk, cwk), 1.0).astype(k_o.dtype)
    v_o[...] = conv_silu(pv, hv, cwv).astype(v_o.dtype)

    raw = ps[:, N_SMALL - LANES:]
    gval = -jnp.exp(alog[...]) * _softplus(raw + dtb[...])
    gcum = _dot_sel_l(_chunk_cumsum_matrix(tm), gval, 3)
    lane = _iota2(raw.shape, 1)
    is_g = (lane >= GDN_G_LANE) & (lane < GDN_G_LANE + GDN_HEADS)
    is_b = (lane >= GDN_B_LANE) & (lane < GDN_B_LANE + GDN_HEADS)
    gb_o[...] = jnp.where(is_g, gcum, jnp.where(is_b, _sigmoid(raw), 0.0))


def _gdn_prep_io(p, psm, wts):
    m = p.shape[0]
    tm = TM_PREP
    tile = lambda cb: pl.BlockSpec((tm, MIX), lambda i, cb=cb: (i, cb))
    halo = lambda cb: pl.BlockSpec((16, MIX), lambda i, cb=cb: (jnp.maximum(i * (tm // 16) - 1, 0), cb))
    cw = pl.BlockSpec((4, MIX), lambda i: (0, 0))
    lv = pl.BlockSpec((1, LANES), lambda i: (0, 0))
    out = pl.BlockSpec((tm, MIX), lambda i: (i, 0))
    in_specs = [tile(4), tile(5), tile(6), halo(4), halo(5), halo(6),
                pl.BlockSpec((tm, N_SMALL), lambda i: (i, 0)), cw, cw, cw, lv, lv]
    args = [p, p, p, p, p, p, psm, wts["cw_q"], wts["cw_k"], wts["cw_v"], wts["alog"], wts["dtb"]]
    out_shape = [jax.ShapeDtypeStruct((m, MIX), BF16)] * 3 + [jax.ShapeDtypeStruct((m, LANES), F32)]
    return in_specs, args, [out, out, out, pl.BlockSpec((tm, LANES), lambda i: (i, 0))], out_shape


def _mixer_prep_kernel(*refs, n_in, seq, tm, has_vres):
    n_pool, n_rw, n_gd = n_in
    ins, outs = refs[:sum(n_in)], refs[sum(n_in):]
    _pool_kernel(*ins[:n_pool], outs[0], seq=seq, tm=tm)
    _rwkv_prep_kernel(*ins[n_pool:n_pool + n_rw], *outs[1:8], seq=seq, tm=tm, has_vres=has_vres)
    _gdn_prep_kernel(*ins[n_pool + n_rw:], *outs[8:], seq=seq, tm=tm)


def _mixer_prep(p, psm, pool_w, pool_scale, rw, gd, v_first, seq):
    ios = [_pool_io(p, pool_w, pool_scale), _rwkv_prep_io(p, psm, rw, v_first), _gdn_prep_io(p, psm, gd)]
    res = pl.pallas_call(
        functools.partial(_mixer_prep_kernel, n_in=tuple(len(io[0]) for io in ios), seq=seq,
                          tm=TM_PREP, has_vres=v_first is not None),
        grid=(p.shape[0] // TM_PREP,),
        in_specs=[s for io in ios for s in io[0]],
        out_specs=[s for io in ios for s in io[2]],
        out_shape=[s for io in ios for s in io[3]],
        compiler_params=_params("parallel"),
        name="mixer_prep",
    )(*[a for io in ios for a in io[1]])
    return res[0], res[1:8], res[8:]


def _gdn_core_kernel(q_ref, k_ref, v_ref, gb_ref, z_ref, nw_ref, o_ref, s_ref):
    @pl.when(pl.program_id(1) == 0)
    def _():
        s_ref[...] = jnp.zeros_like(s_ref)

    nb, rb, width = q_ref.shape
    nh = width // LANES
    units = [(b, hh) for hh in range(nh) for b in range(nb)]
    assert rb == 2 * CHUNK
    m = _pair_masks()
    ones = jnp.ones((LANES, LANES), BF16)
    cat0 = lambda *xs: jnp.concatenate(xs, axis=0)
    cat1 = lambda *xs: jnp.concatenate(xs, axis=1)

    def lane_bcast(x, lane):
        return jnp.broadcast_to(pltpu.roll(x, LANES - lane, 1)[:, 0:1], x.shape)

    nch = rb // CHUNK
    q_t, y0, phi, psi = {}, {}, {}, {}

    gf, ks, lhs, kbd, rhs, e_g, qs, gamma = [], [], [], [], [], [], [], []
    zero = jnp.zeros((CHUNK, LANES), F32)
    for b, hh in units:
        hl = slice(hh * LANES, (hh + 1) * LANES)
        head = nh * pl.program_id(0) + hh
        gbv = gb_ref[b]
        g = lane_bcast(gbv, GDN_G_LANE + head)
        beta = lane_bcast(gbv, GDN_B_LANE + head)
        q, k, v = (r[b, :, hl].astype(F32) for r in (q_ref, k_ref, v_ref))
        g_col = g[:CHUNK] * m["half0"] + g[CHUNK:] * m["half1"]
        diff = g_col - g.T[:CHUNK]
        gamma.append(jnp.where(m["incl"], jnp.exp(jnp.where(m["incl"], diff, 0.0)), 0.0))
        eg = jnp.exp(g)
        kb = k * beta
        gf.append(g)
        ks.append(k)
        qs.append(q)
        e_g.append(eg)
        lhs.append(cat0(cat1(kb[:CHUNK], kb[CHUNK:]), cat1(q[:CHUNK], q[CHUNK:])))
        kbd.append(cat0(cat1(k[:CHUNK], zero), cat1(zero, k[CHUNK:])))
        rhs.append(cat1(v * beta, kb * eg))
    kq = [_dot_nt(x, y) for x, y in zip(lhs, kbd)]
    a_neg = [-jnp.where(m["strict"], x[:CHUNK] * gm, 0.0) for x, gm in zip(kq, gamma)]
    a_qk = [x[CHUNK:] * gm for x, gm in zip(kq, gamma)]
    k_bar_t = [[(k[c * CHUNK:(c + 1) * CHUNK]
                 * jnp.exp(g[(c + 1) * CHUNK - 1:(c + 1) * CHUNK] - g[c * CHUNK:(c + 1) * CHUNK])).T
                for c in range(nch)] for k, g in zip(ks, gf)]
    t_inv = _tri_inv(a_neg, m)
    uw = [_dot(_stack(t, m), x) for t, x in zip(t_inv, rhs)]
    x3 = [_dot(_stack(a, m), x) for a, x in zip(a_qk, uw)]
    for i, u in enumerate(units):
        for c in range(nch):
            glast = gf[i][(c + 1) * CHUNK - 1:(c + 1) * CHUNK]
            pp = _dot(k_bar_t[i][c], uw[i][c * CHUNK:(c + 1) * CHUNK])
            phi[(u, c)] = m["eye128"] * jnp.exp(glast) - pp[:, LANES:]
            psi[(u, c)] = pp[:, :LANES]
    for i, u in enumerate(units):
        y0[u] = x3[i][:, :LANES]
        q_t[u] = qs[i] * e_g[i] - x3[i][:, LANES:]

    ys = {u: [] for u in units}
    st = {(b, hh): s_ref[hh, b] for b, hh in units}
    for c in range(nch):
        cs = slice(c * CHUNK, (c + 1) * CHUNK)
        for u in units:
            qs_ = _dot(cat0(q_t[u][cs], phi[(u, c)]), st[u])
            ys[u].append(y0[u][cs] + qs_[:CHUNK])
            st[u] = qs_[CHUNK:] + psi[(u, c)]
    for b, hh in units:
        s_ref[hh, b] = st[(b, hh)]

    for hh in range(nh):
        hl = slice(hh * LANES, (hh + 1) * LANES)
        y = cat0(*[yc for b in range(nb) for yc in ys[(b, hh)]])
        ms = _dot(y * y, ones) * (1.0 / GDN_HEAD)
        out = y * lax.rsqrt(ms + GDN_EPS) * nw_ref[...]
        z = z_ref[:, :, hl].astype(F32).reshape(nb * rb, LANES)
        o_ref[:, :, hl] = (out * _silu(z)).astype(BF16).reshape(nb, rb, LANES)


def _gdn_core(q, k, v, gb, p, norm_w, batch, seq):
    rb, nh = RB_GDN, GDN_HEADS_PER_STEP
    width = nh * LANES
    tile = pl.BlockSpec((batch, rb, width), lambda h, j: (0, j, h))
    z_col0 = 7 * (MIX // width)
    r3 = lambda a: a.reshape(batch, seq, a.shape[-1])
    out = pl.pallas_call(
        _gdn_core_kernel,
        grid=(GDN_HEADS // nh, seq // rb),
        in_specs=[tile, tile, tile,
                  pl.BlockSpec((batch, rb, LANES), lambda h, j: (0, j, 0)),
                  pl.BlockSpec((batch, rb, width), lambda h, j: (0, j, z_col0 + h)),
                  pl.BlockSpec((1, LANES), lambda h, j: (0, 0))],
        out_specs=tile,
        out_shape=jax.ShapeDtypeStruct((batch, seq, MIX), BF16),
        scratch_shapes=[pltpu.VMEM((nh, batch, LANES, LANES), F32)],
        compiler_params=_params("parallel", "arbitrary"),
        name="gdn_core",
    )(r3(q), r3(k), r3(v), r3(gb), r3(p), norm_w.reshape(1, LANES))
    return out.reshape(batch * seq, MIX)


def _merge_kernel(ya, yb, yc, wa, wb, wc, ga, gb, gc, o_ref):
    dot = lambda y, w: jnp.dot(y[...], w[0], preferred_element_type=F32)
    acc = ga[...].astype(F32) * dot(ya, wa)
    acc = acc + gb[...].astype(F32) * dot(yb, wb)
    acc = acc + gc[...].astype(F32) * dot(yc, wc)
    o_ref[...] = acc.astype(BF16)


def _merge(ya, yb, yc, wa, wb, wc, l, gates):
    m = ya.shape[0]
    d = wa.shape[2]
    tm = TM_LN
    y = pl.BlockSpec((tm, MIX), lambda i: (i, 0))
    w = pl.BlockSpec((1, MIX, d), lambda i: (l, 0, 0), pipeline_mode=pl.Buffered(1))
    g = lambda br: pl.BlockSpec((tm, d), lambda i, br=br: (i, br))
    return pl.pallas_call(
        _merge_kernel,
        grid=(m // tm,),
        in_specs=[y, y, y, w, w, w, g(0), g(1), g(2)],
        out_specs=pl.BlockSpec((tm, d), lambda i: (i, 0)),
        out_shape=jax.ShapeDtypeStruct((m, d), BF16),
        compiler_params=_params("parallel"),
        name="merge",
    )(ya, yb, yc, wa, wb, wc, gates, gates, gates)


def _proj_ln_kernel(*refs, emit_h):
    a_ref, w_ref, x_ref, gt_ref, lnw_ref, lnb_ref = refs[:6]
    y = jnp.dot(a_ref[...], w_ref[0], preferred_element_type=F32)
    xn = _resid_ln(x_ref[...], y, gt_ref[0], lnw_ref[...], lnb_ref[...])
    if emit_h:
        sc_ref, sh_ref, xo_ref, ho_ref = refs[6:]
        ho_ref[...] = (xn * (1.0 + sc_ref[0]) + sh_ref[0]).astype(BF16)
    else:
        (xo_ref,) = refs[6:]
    xo_ref[...] = xn


def _proj_ln(a, w, l, x, gt, lnw, lnb, nxt, seq, *, tm, name):
    m, k = a.shape
    d = w.shape[2]
    per = seq // tm
    emit_h = nxt is not None
    row = pl.BlockSpec((tm, d), lambda i: (i, 0))
    mod = pl.BlockSpec((1, 1, d), lambda i: (i // per, 0, 0))
    vec = pl.BlockSpec((1, d), lambda i: (0, 0))
    in_specs = [pl.BlockSpec((tm, k), lambda i: (i, 0)),
                pl.BlockSpec((1, k, d), lambda i: (l, 0, 0), pipeline_mode=pl.Buffered(1)),
                row, mod, vec, vec]
    args = [a, w, x, gt, lnw.reshape(1, d), lnb.reshape(1, d)]
    out_specs, out_shape = [row], [jax.ShapeDtypeStruct((m, d), F32)]
    if emit_h:
        in_specs += [mod, mod]
        args += list(nxt)
        out_specs.append(row)
        out_shape.append(jax.ShapeDtypeStruct((m, d), BF16))
    res = pl.pallas_call(
        functools.partial(_proj_ln_kernel, emit_h=emit_h),
        grid=(m // tm,),
        in_specs=in_specs,
        out_specs=out_specs,
        out_shape=out_shape,
        compiler_params=_params("parallel"),
        name=name,
    )(*args)
    return (res[0], res[1]) if emit_h else (res[0], None)


def _swiglu_kernel(a_ref, wg_ref, wu_ref, o_ref, wgb_ref, wub_ref):
    @pl.when(pl.program_id(1) == 0)
    def _():
        wgb_ref[...] = wg_ref[0].astype(BF16)
        wub_ref[...] = wu_ref[0].astype(BF16)

    a = a_ref[...]
    g = jnp.dot(a, wgb_ref[...], preferred_element_type=F32)
    u = jnp.dot(a, wub_ref[...], preferred_element_type=F32)
    o_ref[...] = (_silu(g) * u).astype(BF16)


def _swiglu(h, w_up, l):
    m, k = h.shape
    dff = w_up.shape[2] // 2
    tm, tn = TM_MM, 512
    nj = dff // tn
    return pl.pallas_call(
        _swiglu_kernel,
        grid=(nj, m // tm),
        in_specs=[pl.BlockSpec((tm, k), lambda j, i: (i, 0)),
                  pl.BlockSpec((1, k, tn), lambda j, i: (l, 0, j)),
                  pl.BlockSpec((1, k, tn), lambda j, i: (l, 0, nj + j))],
        out_specs=pl.BlockSpec((tm, tn), lambda j, i: (i, j)),
        out_shape=jax.ShapeDtypeStruct((m, dff), BF16),
        scratch_shapes=[pltpu.VMEM((k, tn), BF16)] * 2,
        compiler_params=_params("parallel", "arbitrary"),
        name="ffn_up_swiglu",
    )(h, w_up, w_up)


def _layer_weights(l, w_in_t, rwkv_mu, rwkv_w0, rwkv_w2, rwkv_a0, rwkv_a2, rwkv_g2, rwkv_k_k, rwkv_k_a,
                   rwkv_r_k, rwkv_ln_w, rwkv_ln_b, rwkv_v0, rwkv_v1, rwkv_v2, gdn_conv_w, gdn_a_log,
                   gdn_dt_bias):
    c = MIX
    rows = lambda lo, hi: w_in_t[l, lo:hi]
    o_rwkv = c
    o_lora = o_rwkv + 3 * c
    n_lora = rwkv_w2.shape[1] + rwkv_a2.shape[1] + rwkv_g2.shape[1]
    o_gdn = o_lora + n_lora
    o_ab = o_gdn + 4 * c
    o_gate = o_ab + 2 * GDN_HEADS
    pad_rows = N_SMALL - n_lora - 2 * GDN_HEADS
    w_main = (o_lora, rows(o_gdn, o_ab))
    assert n_lora - (N_SMALL - LANES) == GDN_G_LANE and GDN_B_LANE == GDN_G_LANE + GDN_HEADS
    w_small = jnp.concatenate([rows(o_lora, o_gdn), rows(o_ab, o_gate),
                               jnp.zeros((pad_rows, w_in_t.shape[2]), BF16)], axis=0)
    w_gate = rows(o_gate, w_in_t.shape[1])

    mu = rwkv_mu[l]
    row = lambda v: v.reshape(1, -1)
    n_w, n_a, n_g = rwkv_w2.shape[1], rwkv_a2.shape[1], rwkv_g2.shape[1]
    padrows = lambda w, lo: jnp.zeros((N_SMALL, c), F32).at[lo:lo + w.shape[0]].set(w).astype(BF16)
    lane_vec = lambda v, lo: jnp.zeros((1, LANES), F32).at[0, lo:lo + v.shape[0]].set(v)
    rw = {
        "mu_r": row(mu[:c]), "mu_k": row(mu[c:2 * c]), "mu_v": row(mu[2 * c:3 * c]),
        "mu_s": jnp.zeros((1, N_SMALL), F32).at[0, :n_lora].set(mu[3 * c:]),
        "w0": row(rwkv_w0[l]), "a0": row(rwkv_a0[l]), "k_k": row(rwkv_k_k[l]), "k_a": row(rwkv_k_a[l]),
        "w2p": padrows(rwkv_w2[l], 0), "a2p": padrows(rwkv_a2[l], n_w), "g2p": padrows(rwkv_g2[l], n_w + n_a),
        "r_k": row(rwkv_r_k[l]), "ln_w": row(rwkv_ln_w[l]), "ln_b": row(rwkv_ln_b[l]),
    }
    if l > 0:
        nv = rwkv_v1.shape[2]
        rw["v0"] = row(rwkv_v0[l - 1])
        rw["v1p"] = jnp.zeros((c, LANES), F32).at[:, :nv].set(rwkv_v1[l - 1]).astype(BF16)
        rw["v2p"] = jnp.zeros((LANES, c), F32).at[:nv].set(rwkv_v2[l - 1]).astype(BF16)
    cw = gdn_conv_w[l]
    gd = {"cw_q": cw[:, :c], "cw_k": cw[:, c:2 * c], "cw_v": cw[:, 2 * c:],
          "alog": lane_vec(gdn_a_log[l], GDN_G_LANE), "dtb": lane_vec(gdn_dt_bias[l], GDN_G_LANE)}
    return w_main, w_small, w_gate, rw, gd


def kernel(x, c, ada_w, ada_b, w_in, pool_w, pool_scale, rwkv_mu, rwkv_w0, rwkv_w2, rwkv_a0, rwkv_a2, rwkv_g2, rwkv_k_k, rwkv_k_a, rwkv_r_k, rwkv_ln_w, rwkv_ln_b, rwkv_v0, rwkv_v1, rwkv_v2, gdn_conv_w, gdn_a_log, gdn_dt_bias, gdn_norm_w, w_branch_a, w_branch_b, w_branch_c, w_out, ln1_w, ln1_b, ffn_w_up, ffn_w_down, ln2_w, ln2_b):
    batch, seq, d = x.shape
    assert seq % TM_MM == 0 and d == 2 * MIX
    m = batch * seq
    xf = x.reshape(m, d)

    mod = _ada(c, ada_w, ada_b)
    mods = [[mod[l, :, i * d:(i + 1) * d].reshape(batch, 1, d) for i in range(6)]
            for l in range(DEPTH)]

    w_in_t = jnp.swapaxes(w_in, 1, 2).astype(BF16)
    wa, wb, wc = (w.astype(BF16) for w in (w_branch_a, w_branch_b, w_branch_c))
    w_out_b, w_down_b = (w.astype(BF16) for w in (w_out, ffn_w_down))
    v_first = None
    for l in range(DEPTH):
        sh_m, sc_m, gt_m, sh_f, sc_f, gt_f = mods[l]
        (n_main1, w_main2), w_small, w_gate, rw, gd = _layer_weights(
            l, w_in_t, rwkv_mu, rwkv_w0, rwkv_w2, rwkv_a0, rwkv_a2, rwkv_g2, rwkv_k_k, rwkv_k_a,
            rwkv_r_k, rwkv_ln_w, rwkv_ln_b, rwkv_v0, rwkv_v1, rwkv_v2, gdn_conv_w, gdn_a_log,
            gdn_dt_bias)

        if l == 0:
            h, psm = _modulate_small(xf, sc_m, sh_m, w_small, seq)
        else:
            psm = _matmul_nt(h, w_small, tn=N_SMALL, out_dtype=F32, name="in_proj_small")
        p = _matmul_nt2(h, w_in_t, l, n_main1, w_main2, tn=2048, out_dtype=BF16, name="in_proj_main")
        gates = _matmul_nt(h, w_gate, tn=2048, out_dtype=BF16, act="sigmoid", name="in_proj_gates")

        y_a, prep, (gq, gk, gv, ggb) = _mixer_prep(p, psm, pool_w[l], pool_scale[l], rw, gd, v_first, seq)
        if l == 0:
            v_first = prep[2]
        y_b = _rwkv_core(prep, rw, batch, seq)
        y_c = _gdn_core(gq, gk, gv, ggb, p, gdn_norm_w[l], batch, seq)

        merged = _merge(y_a, y_b, y_c, wa, wb, wc, l, gates)
        xf, h = _proj_ln(merged, w_out_b, l, xf, gt_m, ln1_w[l], ln1_b[l],
                         (sc_f, sh_f), seq, tm=512, name="out_proj_ln")

        act = _swiglu(h, ffn_w_up, l)
        nxt = (mods[l + 1][1], mods[l + 1][0]) if l + 1 < DEPTH else None
        xf, h = _proj_ln(act, w_down_b, l, xf, gt_f, ln2_w[l], ln2_b[l],
                         nxt, seq, tm=256, name="ffn_down_ln")
    return xf.reshape(batch, seq, d)
```
